```python
import jax, jax.numpy as jnp
from jax import lax
import numpy as np

D_MODEL = 1024
BATCH = 2
SEQ = 8192
DEPTH = 4
DEC_BATCH = 128
DEC_SEQ = 4
PAST_LEN = 2048
PAGE_SIZE = 128

N_A_LAYERS = DEPTH // 2
N_B_LAYERS = DEPTH - N_A_LAYERS
CONV_CH = D_MODEL
CONV_WIDTH = 31
HEAD_DIM = 64
N_KV_HEADS = D_MODEL // 128
WINDOWS = ((128, 1), (512, 4), (2048, 16))
N_GROUPS = len(WINDOWS)
MAX_WINDOW = max(w for w, _ in WINDOWS)
D_FF = 4 * D_MODEL
Q_BLOCK = 128
NORM_EPS = 1e-6

kernel_name = "yoco_conformer_conv_dilated_swa_decoder_step"


def rms_norm(x, g):
    xf = x.astype(jnp.float32)
    y = xf * lax.rsqrt(jnp.mean(xf * xf, axis=-1, keepdims=True) + NORM_EPS)
    return (y * g.astype(jnp.float32)).astype(x.dtype)


def layer_norm(x, g, b):
    xf = x.astype(jnp.float32)
    mu = jnp.mean(xf, axis=-1, keepdims=True)
    xc = xf - mu
    y = xc * lax.rsqrt(jnp.mean(xc * xc, axis=-1, keepdims=True) + NORM_EPS)
    return (y * g.astype(jnp.float32) + b.astype(jnp.float32)).astype(x.dtype)


def alibi_slopes():
    n = N_GROUPS * N_KV_HEADS
    s = 2.0 ** (-8.0 * jnp.arange(1, n + 1, dtype=jnp.float32) / n)
    return s.reshape(N_GROUPS, N_KV_HEADS)


def conv_module(h, conv_state, w_pw1, b_pw1, w_dw, b_dw, ln_g, ln_b, w_pw2, b_pw2):
    u = h @ w_pw1 + b_pw1
    a, gate = jnp.split(u, 2, axis=-1)
    u = a * jax.nn.sigmoid(gate)
    full = jnp.concatenate([conv_state.astype(u.dtype), u], axis=1)
    c = lax.conv_general_dilated(full, w_dw[:, None, :], window_strides=(1,), padding='VALID',
                                 dimension_numbers=('NWC', 'WIO', 'NWC'),
                                 feature_group_count=CONV_CH) + b_dw
    c = layer_norm(c, ln_g, ln_b)
    c = jax.nn.silu(c)
    out = c @ w_pw2 + b_pw2
    return out, full[:, -(CONV_WIDTH - 1):]


def sq_relu_mlp(h, w_up, w_down):
    z = jax.nn.relu(h @ w_up)
    return (z * z) @ w_down


def dilated_attention(q, k, v, q_idx):
    N, T = q.shape[0], q.shape[1]
    slopes = alibi_slopes()
    scale = HEAD_DIM ** -0.5

    def block(args):
        qb, ib = args
        ms, dens, outs = [], [], []
        for g, (w, d) in enumerate(WINDOWS):
            offs = jnp.arange(w // d + 1, dtype=jnp.int32) * d
            kidx = ib[:, None] - offs[None, :]
            valid = kidx >= 0
            kidx = jnp.maximum(kidx, 0)
            kg = jnp.take(k, kidx, axis=1)
            vg = jnp.take(v, kidx, axis=1)
            s = jnp.einsum('nthd,ntkhd->nthk', qb[:, :, g], kg,
                           preferred_element_type=jnp.float32) * scale
            s = s - slopes[g][:, None] * offs.astype(jnp.float32)[None, :]
            s = jnp.where(valid[None, :, None, :], s, -jnp.inf)
            m = jnp.max(s, axis=-1)
            p = jnp.exp(s - m[..., None])
            den = jnp.sum(p, axis=-1)
            o = jnp.einsum('nthk,ntkhd->nthd', p, vg,
                           preferred_element_type=jnp.float32) / den[..., None]
            ms.append(m); dens.append(den); outs.append(o)
        m = jnp.stack(ms)
        den = jnp.stack(dens)
        o = jnp.stack(outs)
        wgt = den * jnp.exp(m - jnp.max(m, axis=0, keepdims=True))
        out = jnp.sum(wgt[..., None] * o, axis=0) / jnp.sum(wgt, axis=0)[..., None]
        return out.astype(q.dtype)

    if T > Q_BLOCK and T % Q_BLOCK == 0:
        nb = T // Q_BLOCK
        qb = jnp.swapaxes(q.reshape(N, nb, Q_BLOCK, N_GROUPS, N_KV_HEADS, HEAD_DIM), 0, 1)
        ib = q_idx.reshape(nb, Q_BLOCK)
        out = lax.map(block, (qb, ib))
        return jnp.swapaxes(out, 0, 1).reshape(N, T, N_KV_HEADS, HEAD_DIM)
    return block((q, q_idx))


def trunk(x, conv_state, kv_past, norm_mix_g, norm_mlp_g, conv_w_pw1, conv_b_pw1, conv_w_dw, conv_b_dw,
          conv_ln_g, conv_ln_b, conv_w_pw2, conv_b_pw2, kv_norm_g, w_kv, k_norm_g, attn_w_q, q_norm_g,
          attn_w_o, mlp_w_up, mlp_w_down):
    N, T, _ = x.shape
    P = kv_past.shape[1]
    new_conv = []
    kv_new = None
    k_all = v_all = q_idx = None
    for layer in range(DEPTH):
        h = rms_norm(x, norm_mix_g[layer])
        if layer < N_A_LAYERS:
            o, st = conv_module(h, conv_state[layer], conv_w_pw1[layer], conv_b_pw1[layer],
                                conv_w_dw[layer], conv_b_dw[layer], conv_ln_g[layer], conv_ln_b[layer],
                                conv_w_pw2[layer], conv_b_pw2[layer])
            new_conv.append(st)
            x = x + o
        else:
            if layer == N_A_LAYERS:
                hk = rms_norm(x, kv_norm_g)
                kv = (hk @ w_kv).reshape(N, T, 2, N_KV_HEADS, HEAD_DIM)
                kv_new = jnp.stack([rms_norm(kv[:, :, 0], k_norm_g), kv[:, :, 1]], axis=2)
                kv_all = jnp.concatenate([kv_past.astype(kv_new.dtype), kv_new], axis=1)
                k_all, v_all = kv_all[:, :, 0], kv_all[:, :, 1]
                q_idx = P + jnp.arange(T, dtype=jnp.int32)
            j = layer - N_A_LAYERS
            q = (h @ attn_w_q[j]).reshape(N, T, N_GROUPS, N_KV_HEADS, HEAD_DIM)
            q = rms_norm(q, q_norm_g[j])
            o = dilated_attention(q, k_all, v_all, q_idx)
            x = x + o.reshape(N, T, N_KV_HEADS * HEAD_DIM) @ attn_w_o[j]
        x = x + sq_relu_mlp(rms_norm(x, norm_mlp_g[layer]), mlp_w_up[layer], mlp_w_down[layer])
    return x, jnp.stack(new_conv, axis=0), kv_new


def setup_inputs(seed: int = 0) -> dict:
    key = jax.random.key(seed)
    ks = jax.random.split(key, 24)
    f32 = jnp.float32
    win_buf = min(MAX_WINDOW, PAST_LEN)
    qw = N_GROUPS * N_KV_HEADS * HEAD_DIM
    ow = N_KV_HEADS * HEAD_DIM
    nrm = lambda k, s: jax.random.normal(k, s, f32)
    return {
        "x_prompt": nrm(ks[0], (BATCH, SEQ, D_MODEL)),
        "x_sample": nrm(ks[1], (DEC_BATCH, DEC_SEQ, D_MODEL)),
        "state_conv": 0.5 * nrm(ks[2], (N_A_LAYERS, DEC_BATCH, CONV_WIDTH - 1, CONV_CH)),
        "cache_kv": nrm(ks[3], (DEC_BATCH, win_buf, 2, N_KV_HEADS, HEAD_DIM)),
        "norm_mix_g": 1.0 + 0.02 * nrm(ks[4], (DEPTH, D_MODEL)),
        "norm_mlp_g": 1.0 + 0.02 * nrm(ks[5], (DEPTH, D_MODEL)),
        "conv_w_pw1": nrm(ks[6], (N_A_LAYERS, D_MODEL, 2 * CONV_CH)) * D_MODEL ** -0.5,
        "conv_b_pw1": 0.02 * nrm(ks[7], (N_A_LAYERS, 2 * CONV_CH)),
        "conv_w_dw": nrm(ks[8], (N_A_LAYERS, CONV_WIDTH, CONV_CH)) * CONV_WIDTH ** -0.5,
        "conv_b_dw": 0.02 * nrm(ks[9], (N_A_LAYERS, CONV_CH)),
        "conv_ln_g": 1.0 + 0.02 * nrm(ks[10], (N_A_LAYERS, CONV_CH)),
        "conv_ln_b": 0.02 * nrm(ks[11], (N_A_LAYERS, CONV_CH)),
        "conv_w_pw2": nrm(ks[12], (N_A_LAYERS, CONV_CH, D_MODEL)) * CONV_CH ** -0.5,
        "conv_b_pw2": 0.02 * nrm(ks[13], (N_A_LAYERS, D_MODEL)),
        "kv_norm_g": 1.0 + 0.02 * nrm(ks[14], (D_MODEL,)),
        "w_kv": nrm(ks[15], (D_MODEL, 2 * N_KV_HEADS * HEAD_DIM)) * D_MODEL ** -0.5,
        "k_norm_g": 1.0 + 0.02 * nrm(ks[16], (HEAD_DIM,)),
        "attn_w_q": nrm(ks[17], (N_B_LAYERS, D_MODEL, qw)) * D_MODEL ** -0.5,
        "q_norm_g": 1.0 + 0.02 * nrm(ks[18], (N_B_LAYERS, HEAD_DIM)),
        "attn_w_o": nrm(ks[19], (N_B_LAYERS, ow, D_MODEL)) * ow ** -0.5,
        "mlp_w_up": nrm(ks[20], (DEPTH, D_MODEL, D_FF)) * D_MODEL ** -0.5,
        "mlp_w_down": nrm(ks[21], (DEPTH, D_FF, D_MODEL)) * D_FF ** -0.5,
    }


def reference(x_prompt, x_sample, state_conv, cache_kv, norm_mix_g, norm_mlp_g, conv_w_pw1, conv_b_pw1,
              conv_w_dw, conv_b_dw, conv_ln_g, conv_ln_b, conv_w_pw2, conv_b_pw2, kv_norm_g, w_kv, k_norm_g,
              attn_w_q, q_norm_g, attn_w_o, mlp_w_up, mlp_w_down):
    weights = (norm_mix_g, norm_mlp_g, conv_w_pw1, conv_b_pw1, conv_w_dw, conv_b_dw, conv_ln_g, conv_ln_b,
               conv_w_pw2, conv_b_pw2, kv_norm_g, w_kv, k_norm_g, attn_w_q, q_norm_g, attn_w_o,
               mlp_w_up, mlp_w_down)
    nb, t_prompt = x_prompt.shape[0], x_prompt.shape[1]
    conv0 = jnp.zeros((N_A_LAYERS, nb, CONV_WIDTH - 1, CONV_CH), x_prompt.dtype)
    kv0 = jnp.zeros((nb, 0, 2, N_KV_HEADS, HEAD_DIM), x_prompt.dtype)
    y_prompt, conv_prompt, kv_rows_prompt = trunk(x_prompt, conv0, kv0, *weights)
    kv_prompt = kv_rows_prompt[:, -min(MAX_WINDOW, t_prompt):]
    y_sample, conv_sample, kv_sample = trunk(x_sample, state_conv, cache_kv, *weights)
    return (y_prompt, y_sample, conv_prompt, conv_sample, kv_prompt, kv_sample)
```

```python
import functools

import jax
import jax.numpy as jnp
from jax import lax
from jax.experimental import pallas as pl
from jax.experimental.pallas import tpu as pltpu

F32 = jnp.float32
BF16 = jnp.bfloat16

D_MODEL = 1024
CONV_CH = 1024
CONV_WIDTH = 31
CONV_HIST = CONV_WIDTH - 1
HIST_PAD = 32
D_FF = 4096
N_HEADS = 8
HEAD_DIM = 64
HEADS_W = N_HEADS * HEAD_DIM
WINDOWS = ((128, 1), (512, 4), (2048, 16))
N_GROUPS = len(WINDOWS)
BAND = 128
N_A_LAYERS = 2
N_B_LAYERS = 2
NORM_EPS = 1e-6
NEG_INF = float("-inf")

ROW_TILE = 512
ATT_OUT_TILE = 256
FF_CHUNK = 1024
ATT_QB = 512
SAMPLE_CONV_NB = 32
SAMPLE_ATT_NB = 2
MIB = 1024 * 1024

assert all(w // d == BAND for w, d in WINDOWS)


def _const_spec(shape):
    nd = len(shape)
    return pl.BlockSpec(shape, lambda *_: (0,) * nd, pipeline_mode=pl.Buffered(1))


def _params(semantics, vmem_mib):
    return pltpu.CompilerParams(dimension_semantics=semantics, vmem_limit_bytes=vmem_mib * MIB)


def _rms(x, g):
    return x * lax.rsqrt(jnp.mean(x * x, axis=-1, keepdims=True) + NORM_EPS) * g


def _mlp_residual(x, g_ref, wup_ref, wdn_ref):
    h = _rms(x, g_ref[...]).astype(BF16)
    acc = x
    for c in range(D_FF // FF_CHUNK):
        sl = slice(c * FF_CHUNK, (c + 1) * FF_CHUNK)
        z = jnp.maximum(jnp.dot(h, wup_ref[:, sl], preferred_element_type=F32), 0.0)
        acc = acc + jnp.dot((z * z).astype(BF16), wdn_ref[sl, :], preferred_element_type=F32)
    return acc


def _head_mean_sq(v, e_ref):
    return jnp.dot((v * v).astype(BF16), e_ref[...], preferred_element_type=F32)


def _mlp_kernel(x_ref, g_ref, wup_ref, wdn_ref, o_ref):
    o_ref[...] = _mlp_residual(x_ref[...], g_ref, wup_ref, wdn_ref)


def _mlp(x, g, wup, wdn):
    m = x.shape[0]
    row = pl.BlockSpec((ROW_TILE, D_MODEL), lambda i: (i, 0))
    return pl.pallas_call(
        _mlp_kernel,
        out_shape=jax.ShapeDtypeStruct((m, D_MODEL), F32),
        grid=(m // ROW_TILE,),
        in_specs=[row, _const_spec((1, D_MODEL)), _const_spec((D_MODEL, D_FF)),
                  _const_spec((D_FF, D_MODEL))],
        out_specs=row,
        compiler_params=_params(("arbitrary",), 48),
        name="mlp",
    )(x, g, wup, wdn)


def _glu(h, w1_ref, b1_ref):
    u = jnp.dot(h, w1_ref[...], preferred_element_type=F32) + b1_ref[...]
    return u[:, :CONV_CH] * jax.nn.sigmoid(u[:, CONV_CH:])


def _conv_tail(c, x, bdw_ref, lng_ref, lnb_ref, w2_ref, b2_ref):
    c = c + bdw_ref[...]
    mu = jnp.mean(c, axis=-1, keepdims=True)
    xc = c - mu
    y = xc * lax.rsqrt(jnp.mean(xc * xc, axis=-1, keepdims=True) + NORM_EPS)
    y = y * lng_ref[...] + lnb_ref[...]
    y = y * jax.nn.sigmoid(y)
    out = jnp.dot(y.astype(BF16), w2_ref[...], preferred_element_type=F32) + b2_ref[...]
    return x + out


def _conv_prompt_kernel(x_ref, st_ref, g_ref, w1_ref, b1_ref, wdw_ref, bdw_ref, lng_ref, lnb_ref,
                        w2_ref, b2_ref, o_ref, sto_ref, buf_ref):
    @pl.when(pl.program_id(1) == 0)
    def _():
        buf_ref[0:HIST_PAD, :] = st_ref[...]

    x = x_ref[...]
    h = _rms(x, g_ref[...]).astype(BF16)
    buf_ref[HIST_PAD:HIST_PAD + ROW_TILE, :] = _glu(h, w1_ref, b1_ref)
    off = HIST_PAD - CONV_HIST
    c = buf_ref[off:off + ROW_TILE, :] * wdw_ref[0:1, :]
    for k in range(1, CONV_WIDTH):
        c = c + buf_ref[off + k:off + k + ROW_TILE, :] * wdw_ref[k:k + 1, :]
    o_ref[...] = _conv_tail(c, x, bdw_ref, lng_ref, lnb_ref, w2_ref, b2_ref)
    last = buf_ref[ROW_TILE:ROW_TILE + HIST_PAD, :]
    sto_ref[...] = last
    buf_ref[0:HIST_PAD, :] = last


def _conv_prompt(x, st, g, w1, b1, wdw, bdw, lng, lnb, w2, b2):
    n, t, _ = x.shape
    row = pl.BlockSpec((None, ROW_TILE, D_MODEL), lambda b, i: (b, i, 0))
    st_spec = pl.BlockSpec((None, HIST_PAD, CONV_CH), lambda b, i: (b, 0, 0))
    return pl.pallas_call(
        _conv_prompt_kernel,
        out_shape=(jax.ShapeDtypeStruct((n, t, D_MODEL), F32),
                   jax.ShapeDtypeStruct((n, HIST_PAD, CONV_CH), F32)),
        grid=(n, t // ROW_TILE),
        in_specs=[row, st_spec, _const_spec((1, D_MODEL)),
                  _const_spec((D_MODEL, 2 * CONV_CH)), _const_spec((1, 2 * CONV_CH)),
                  _const_spec((CONV_WIDTH, CONV_CH)), _const_spec((1, CONV_CH)),
                  _const_spec((1, CONV_CH)), _const_spec((1, CONV_CH)),
                  _const_spec((CONV_CH, D_MODEL)), _const_spec((1, D_MODEL))],
        out_specs=(row, st_spec),
        scratch_shapes=[pltpu.VMEM((HIST_PAD + ROW_TILE, CONV_CH), F32)],
        compiler_params=_params(("arbitrary", "arbitrary"), 48),
        name="conv_prompt",
    )(x, st, g, w1, b1, wdw, bdw, lng, lnb, w2, b2)


def _conv_sample_kernel(x_ref, st_ref, g_ref, w1_ref, b1_ref, wdw_ref, bdw_ref, lng_ref, lnb_ref,
                        w2_ref, b2_ref, o_ref, sto_ref):
    t_new, nb, _ = x_ref.shape
    x = x_ref[...].reshape(t_new * nb, D_MODEL)
    h = _rms(x, g_ref[...]).astype(BF16)
    glu = _glu(h, w1_ref, b1_ref)
    new = [glu[t * nb:(t + 1) * nb, :] for t in range(t_new)]

    def full(j):
        return st_ref[j] if j < CONV_HIST else new[j - CONV_HIST]

    outs = []
    for t in range(t_new):
        c = full(t) * wdw_ref[0:1, :]
        for k in range(1, CONV_WIDTH):
            c = c + full(t + k) * wdw_ref[k:k + 1, :]
        outs.append(c)
    c = jnp.concatenate(outs, axis=0)
    o_ref[...] = _conv_tail(c, x, bdw_ref, lng_ref, lnb_ref, w2_ref, b2_ref).reshape(t_new, nb, D_MODEL)
    for j in range(CONV_HIST):
        sto_ref[j] = full(j + t_new)


def _conv_sample(x, st, g, w1, b1, wdw, bdw, lng, lnb, w2, b2):
    t_new, n, _ = x.shape
    nb = SAMPLE_CONV_NB
    x_spec = pl.BlockSpec((t_new, nb, D_MODEL), lambda i: (0, i, 0))
    st_spec = pl.BlockSpec((CONV_HIST, nb, CONV_CH), lambda i: (0, i, 0))
    return pl.pallas_call(
        _conv_sample_kernel,
        out_shape=(jax.ShapeDtypeStruct((t_new, n, D_MODEL), F32),
                   jax.ShapeDtypeStruct((CONV_HIST, n, CONV_CH), F32)),
        grid=(n // nb,),
        in_specs=[x_spec, st_spec, _const_spec((1, D_MODEL)),
                  _const_spec((D_MODEL, 2 * CONV_CH)), _const_spec((1, 2 * CONV_CH)),
                  _const_spec((CONV_WIDTH, CONV_CH)), _const_spec((1, CONV_CH)),
                  _const_spec((1, CONV_CH)), _const_spec((1, CONV_CH)),
                  _const_spec((CONV_CH, D_MODEL)), _const_spec((1, D_MODEL))],
        out_specs=(x_spec, st_spec),
        compiler_params=_params(("arbitrary",), 48),
        name="conv_sample",
    )(x, st, g, w1, b1, wdw, bdw, lng, lnb, w2, b2)


def _kv_kernel(x_ref, g_ref, wkv_ref, e_ref, gk_ref, kv_ref, kvb_ref):
    h = _rms(x_ref[...], g_ref[...]).astype(BF16)
    k = jnp.dot(h, wkv_ref[:, :HEADS_W], preferred_element_type=F32)
    v = jnp.dot(h, wkv_ref[:, HEADS_W:], preferred_element_type=F32)
    k = k * lax.rsqrt(_head_mean_sq(k, e_ref) + NORM_EPS) * gk_ref[...]
    kv_ref[:, :HEADS_W] = k
    kv_ref[:, HEADS_W:] = v
    kvb_ref[:, :HEADS_W] = k.astype(BF16)
    kvb_ref[:, HEADS_W:] = v.astype(BF16)


def _kv_proj(x, g, wkv, e, gk):
    m = x.shape[0]
    row = pl.BlockSpec((ROW_TILE, D_MODEL), lambda i: (i, 0))
    out = pl.BlockSpec((ROW_TILE, 2 * HEADS_W), lambda i: (i, 0))
    return pl.pallas_call(
        _kv_kernel,
        out_shape=(jax.ShapeDtypeStruct((m, 2 * HEADS_W), F32),
                   jax.ShapeDtypeStruct((m, 2 * HEADS_W), BF16)),
        grid=(m // ROW_TILE,),
        in_specs=[row, _const_spec((1, D_MODEL)), _const_spec((D_MODEL, 2 * HEADS_W)),
                  _const_spec((HEADS_W, HEADS_W)), _const_spec((1, HEADS_W))],
        out_specs=(out, out),
        compiler_params=_params(("arbitrary",), 32),
        name="kv_proj",
    )(x, g, wkv, e, gk)


def _q_kernel(x_ref, g_ref, wq_ref, e_ref, gq_ref, q_ref):
    h = _rms(x_ref[...], g_ref[...]).astype(BF16)
    for grp in range(N_GROUPS):
        sl = slice(grp * HEADS_W, (grp + 1) * HEADS_W)
        q = jnp.dot(h, wq_ref[:, sl], preferred_element_type=F32)
        q = q * lax.rsqrt(_head_mean_sq(q, e_ref) + NORM_EPS) * gq_ref[...]
        q_ref[:, sl] = q.astype(BF16).astype(q_ref.dtype)


def _q_proj(x, g, wq, e, gq, out_dtype):
    m = x.shape[0]
    qw = N_GROUPS * HEADS_W
    row = pl.BlockSpec((ROW_TILE, D_MODEL), lambda i: (i, 0))
    return pl.pallas_call(
        _q_kernel,
        out_shape=jax.ShapeDtypeStruct((m, qw), out_dtype),
        grid=(m // ROW_TILE,),
        in_specs=[row, _const_spec((1, D_MODEL)), _const_spec((D_MODEL, qw)),
                  _const_spec((HEADS_W, HEADS_W)), _const_spec((1, HEADS_W))],
        out_specs=pl.BlockSpec((ROW_TILE, qw), lambda i: (i, 0)),
        compiler_params=_params(("arbitrary",), 32),
        name="q_proj",
    )(x, g, wq, e, gq)


def _attn_group_kernel(q_ref, kc_ref, kp_ref, vc_ref, vp_ref, bias_ref, acc_ref, m_ref, l_ref,
                       kk_ref, vv_ref):
    kk_ref[0:BAND, :] = kp_ref[...]
    kk_ref[BAND:, :] = kc_ref[...]
    vv_ref[0:BAND, :] = vp_ref[...]
    vv_ref[BAND:, :] = vc_ref[...]
    first_block = pl.program_id(2) == 0
    key_col = lax.broadcasted_iota(jnp.int32, (BAND, 2 * BAND), 1)
    lane = lax.broadcasted_iota(jnp.int32, (BAND, 2 * HEAD_DIM), 1)
    left = lane < HEAD_DIM
    lane_row = lax.broadcasted_iota(jnp.int32, (1, 2 * HEAD_DIM), 1)
    head_keep = [jnp.where(lane_row < HEAD_DIM, 1.0, 0.0).astype(BF16),
                 jnp.where(lane_row < HEAD_DIM, 0.0, 1.0).astype(BF16)]

    def sub_block(sb, carry):
        r0 = pl.multiple_of(sb * BAND, BAND)
        no_history = jnp.logical_and(jnp.logical_and(first_block, sb == 0), key_col < BAND)
        for pair in range(N_HEADS // 2):
            ls = slice(pair * 2 * HEAD_DIM, (pair + 1) * 2 * HEAD_DIM)
            q2 = q_ref[pl.ds(r0, BAND), ls]
            k2 = kk_ref[pl.ds(r0, 2 * BAND), ls]
            v2 = vv_ref[pl.ds(r0, 2 * BAND), ls]
            res = []
            for hh in range(2):
                s = lax.dot_general(q2 * head_keep[hh], k2, (((1,), (1,)), ((), ())),
                                    preferred_element_type=F32)
                s = s + bias_ref[2 * pair + hh]
                s = jnp.where(no_history, NEG_INF, s)
                m = jnp.max(s, axis=-1, keepdims=True)
                p = jnp.exp(s - m)
                l = jnp.sum(p, axis=-1, keepdims=True)
                a = jnp.dot(p.astype(BF16), v2, preferred_element_type=F32)
                res.append((a, m, l))
            acc_ref[pl.ds(r0, BAND), ls] = jnp.where(left, res[0][0], res[1][0])
            m_ref[pl.ds(r0, BAND), ls] = jnp.where(left, res[0][1], res[1][1])
            l_ref[pl.ds(r0, BAND), ls] = jnp.where(left, res[0][2], res[1][2])
        return carry

    lax.fori_loop(0, q_ref.shape[0] // BAND, sub_block, 0)


def _attn_group(q, kvb, bias, grp):
    n, t, _ = q.shape
    d = WINDOWS[grp][1]
    rows = t // d
    qb = ATT_QB
    qv = q.reshape(n, rows, d * N_GROUPS * HEADS_W)
    kvv = kvb.reshape(n, rows, d * 2 * HEADS_W)
    per_blk = qb // BAND

    def prev_idx(i):
        return jnp.maximum(i * per_blk - 1, 0)

    q_spec = pl.BlockSpec((None, qb, HEADS_W), lambda b, r, i: (b, i, r * N_GROUPS + grp))
    kc_spec = pl.BlockSpec((None, qb, HEADS_W), lambda b, r, i: (b, i, 2 * r))
    kp_spec = pl.BlockSpec((None, BAND, HEADS_W), lambda b, r, i: (b, prev_idx(i), 2 * r))
    vc_spec = pl.BlockSpec((None, qb, HEADS_W), lambda b, r, i: (b, i, 2 * r + 1))
    vp_spec = pl.BlockSpec((None, BAND, HEADS_W), lambda b, r, i: (b, prev_idx(i), 2 * r + 1))
    out_spec = pl.BlockSpec((None, qb, HEADS_W), lambda b, r, i: (b, i, r))
    out_sds = jax.ShapeDtypeStruct((n, rows, d * HEADS_W), F32)
    acc, m, l = pl.pallas_call(
        _attn_group_kernel,
        out_shape=(out_sds, out_sds, out_sds),
        grid=(n, d, rows // qb),
        in_specs=[q_spec, kc_spec, kp_spec, vc_spec, vp_spec,
                  _const_spec((N_HEADS, BAND, 2 * BAND))],
        out_specs=(out_spec, out_spec, out_spec),
        scratch_shapes=[pltpu.VMEM((BAND + qb, HEADS_W), BF16),
                        pltpu.VMEM((BAND + qb, HEADS_W), BF16)],
        compiler_params=_params(("arbitrary", "arbitrary", "arbitrary"), 32),
        name=f"attn_group{grp}",
    )(qv, kvv, kvv, kvv, kvv, bias)
    shape = (n * t, HEADS_W)
    return acc.reshape(shape), m.reshape(shape), l.reshape(shape)


def _attn_sample_kernel(q_ref, kvn_ref, c0_ref, c1_ref, c2_ref, b0_ref, b12_ref, sl0_ref, o_ref):
    t_new, nb, _ = q_ref.shape
    hsel = (lax.broadcasted_iota(jnp.int32, (N_HEADS, HEADS_W), 1) // HEAD_DIM
            == lax.broadcasted_iota(jnp.int32, (N_HEADS, HEADS_W), 0))
    kw = 2 * HEADS_W
    for b in range(nb):
        k_new = [kvn_ref[t, b:b + 1, :HEADS_W] for t in range(t_new)]
        v_new = [kvn_ref[t, b:b + 1, HEADS_W:] for t in range(t_new)]
        k0 = c0_ref[b, :, :HEADS_W].astype(BF16)
        v0 = c0_ref[b, :, HEADS_W:].astype(BF16)
        for t in range(t_new):
            parts = []
            for grp in range(N_GROUPS):
                sl = slice(grp * HEADS_W, (grp + 1) * HEADS_W)
                qd = jnp.where(hsel, q_ref[t, b:b + 1, sl], 0.0)
                if grp == 0:
                    kb, vb, bias = k0, v0, b0_ref[t]
                    new_t = list(range(t + 1))
                else:
                    c_ref = c1_ref if grp == 1 else c2_ref
                    kb = c_ref[b, :, t * kw:t * kw + HEADS_W].astype(BF16)
                    vb = c_ref[b, :, t * kw + HEADS_W:(t + 1) * kw].astype(BF16)
                    bias = b12_ref[grp - 1]
                    new_t = [t]
                s = lax.dot_general(qd.astype(BF16), kb, (((1,), (1,)), ((), ())),
                                    preferred_element_type=F32) + bias
                s_new = [jnp.sum(qd * k_new[u], axis=-1, keepdims=True)
                         - sl0_ref[...] * float(t - u) for u in new_t]
                m = jnp.max(s, axis=-1, keepdims=True)
                for sn in s_new:
                    m = jnp.maximum(m, sn)
                p = jnp.exp(s - m)
                l = jnp.sum(p, axis=-1, keepdims=True)
                acc = jnp.dot(p.astype(BF16), vb, preferred_element_type=F32)
                for sn, u in zip(s_new, new_t):
                    pn = jnp.exp(sn - m)
                    l = l + pn
                    acc = acc + pn * v_new[u]
                parts.append((acc, m, l))
            m_all = jnp.maximum(jnp.maximum(parts[0][1], parts[1][1]), parts[2][1])
            num = jnp.zeros((N_HEADS, HEADS_W), F32)
            den = jnp.zeros((N_HEADS, 1), F32)
            for acc, m, l in parts:
                e = jnp.exp(m - m_all)
                num = num + e * acc
                den = den + e * l
            out = jnp.where(hsel, num / den, 0.0)
            o_ref[t, b:b + 1, :] = jnp.sum(out, axis=0, keepdims=True)


def _attn_sample(q, kvn, cache, b0, b12, sl0):
    t_new, n, qw = q.shape
    p = cache.shape[1]
    nb = SAMPLE_ATT_NB
    kw = 2 * HEADS_W
    d1, d2 = WINDOWS[1][1], WINDOWS[2][1]
    assert p == WINDOWS[2][0] and t_new <= d1 and p % (BAND * d2) == 0
    q4 = q.reshape(t_new, n // nb, nb, qw)
    kvn4 = kvn.reshape(t_new, n // nb, nb, kw)
    c1 = cache.reshape(n, p // d1, d1 * kw)
    c2 = cache.reshape(n, p // d2, d2 * kw)
    out = pl.pallas_call(
        _attn_sample_kernel,
        out_shape=jax.ShapeDtypeStruct((t_new, n // nb, nb, HEADS_W), F32),
        grid=(n // nb,),
        in_specs=[pl.BlockSpec((t_new, None, nb, qw), lambda i: (0, i, 0, 0)),
                  pl.BlockSpec((t_new, None, nb, kw), lambda i: (0, i, 0, 0)),
                  pl.BlockSpec((nb, BAND, kw), lambda i: (i, p // BAND - 1, 0)),
                  pl.BlockSpec((nb, BAND, t_new * kw), lambda i: (i, p // d1 // BAND - 1, 0)),
                  pl.BlockSpec((nb, BAND, t_new * kw), lambda i: (i, p // d2 // BAND - 1, 0)),
                  _const_spec((t_new, N_HEADS, BAND)), _const_spec((2, N_HEADS, BAND)),
                  _const_spec((N_HEADS, 1))],
        out_specs=pl.BlockSpec((t_new, None, nb, HEADS_W), lambda i: (0, i, 0, 0)),
        compiler_params=_params(("arbitrary",), 40),
        name="attn_sample",
    )(q4, kvn4, cache, c1, c2, b0, b12, sl0)
    return out.reshape(t_new * n, HEADS_W)


def _attn_out_mlp_kernel(*refs, n_parts):
    x_ref = refs[0]
    part_refs = refs[1:1 + 3 * n_parts]
    o_in_ref = None if n_parts else refs[1]
    wo_ref, g_ref, wup_ref, wdn_ref, out_ref = refs[-5:]
    if n_parts:
        ms = [part_refs[3 * i + 1][...] for i in range(n_parts)]
        m_all = functools.reduce(jnp.maximum, ms)
        num = None
        den = None
        for i in range(n_parts):
            e = jnp.exp(ms[i] - m_all)
            na = e * part_refs[3 * i][...]
            da = e * part_refs[3 * i + 2][...]
            num = na if num is None else num + na
            den = da if den is None else den + da
        o = num / den
    else:
        o = o_in_ref[...]
    x = x_ref[...] + jnp.dot(o.astype(BF16), wo_ref[...], preferred_element_type=F32)
    out_ref[...] = _mlp_residual(x, g_ref, wup_ref, wdn_ref)


def _attn_out_mlp(x, att_inputs, wo, g, wup, wdn):
    m = x.shape[0]
    n_parts = len(att_inputs) // 3
    tile = ATT_OUT_TILE if n_parts else ROW_TILE
    row = pl.BlockSpec((tile, D_MODEL), lambda i: (i, 0))
    att = pl.BlockSpec((tile, HEADS_W), lambda i: (i, 0))
    return pl.pallas_call(
        functools.partial(_attn_out_mlp_kernel, n_parts=n_parts),
        out_shape=jax.ShapeDtypeStruct((m, D_MODEL), F32),
        grid=(m // tile,),
        in_specs=[row] + [att] * len(att_inputs)
        + [_const_spec((HEADS_W, D_MODEL)), _const_spec((1, D_MODEL)),
           _const_spec((D_MODEL, D_FF)), _const_spec((D_FF, D_MODEL))],
        out_specs=row,
        compiler_params=_params(("arbitrary",), 56),
        name="attn_out_mlp",
    )(x, *att_inputs, wo, g, wup, wdn)


def _alibi_slopes():
    n = N_GROUPS * N_HEADS
    s = 2.0 ** (-8.0 * jnp.arange(1, n + 1, dtype=F32) / n)
    return s.reshape(N_GROUPS, N_HEADS)


def _prompt_bias(slopes, grp):
    d = WINDOWS[grp][1]
    iq = jnp.arange(BAND, dtype=jnp.int32)[:, None]
    ik = jnp.arange(2 * BAND, dtype=jnp.int32)[None, :]
    steps = iq + BAND - ik
    valid = (steps >= 0) & (steps <= BAND)
    offs = (steps * d).astype(F32)
    return jnp.where(valid[None], -slopes[grp][:, None, None] * offs[None], NEG_INF)


def _sample_bias(slopes, t_new):
    row = jnp.arange(BAND, dtype=jnp.int32)
    steps12 = (BAND - row).astype(F32)
    b12 = jnp.stack([-slopes[g][:, None] * (steps12 * WINDOWS[g][1])[None, :] for g in (1, 2)])
    steps0 = BAND + jnp.arange(t_new, dtype=jnp.int32)[:, None] - row[None, :]
    b0 = jnp.where((steps0 <= BAND)[:, None, :],
                   -slopes[0][None, :, None] * steps0.astype(F32)[:, None, :], NEG_INF)
    return b0, b12, slopes[0][:, None]


def _head_mean_matrix():
    idx = jnp.arange(HEADS_W) // HEAD_DIM
    return jnp.where(idx[:, None] == idx[None, :], 1.0 / HEAD_DIM, 0.0).astype(BF16)


def kernel(x_prompt, x_sample, state_conv, cache_kv, norm_mix_g, norm_mlp_g, conv_w_pw1, conv_b_pw1,
           conv_w_dw, conv_b_dw, conv_ln_g, conv_ln_b, conv_w_pw2, conv_b_pw2, kv_norm_g, w_kv,
           k_norm_g, attn_w_q, q_norm_g, attn_w_o, mlp_w_up, mlp_w_down):
    n_p, t_p, _ = x_prompt.shape
    n_s, t_s, _ = x_sample.shape
    past = cache_kv.shape[1]
    scale = HEAD_DIM ** -0.5

    w1 = conv_w_pw1.astype(BF16)
    w2 = conv_w_pw2.astype(BF16)
    wkv = w_kv.astype(BF16)
    wq = attn_w_q.astype(BF16)
    wo = attn_w_o.astype(BF16)
    wup = mlp_w_up.astype(BF16)
    wdn = mlp_w_down.astype(BF16)
    row = lambda a: a.reshape(1, -1)
    e_mat = _head_mean_matrix()
    gk = row(jnp.tile(k_norm_g, N_HEADS))
    gq = [row(jnp.tile(q_norm_g[j], N_HEADS) * scale) for j in range(N_B_LAYERS)]
    slopes = _alibi_slopes()
    bias_p = [_prompt_bias(slopes, g) for g in range(N_GROUPS)]
    b0, b12, sl0 = _sample_bias(slopes, t_s)

    def conv_args(layer):
        return (row(norm_mix_g[layer]), w1[layer], row(conv_b_pw1[layer]), conv_w_dw[layer],
                row(conv_b_dw[layer]), row(conv_ln_g[layer]), row(conv_ln_b[layer]),
                w2[layer], row(conv_b_pw2[layer]))

    xp = x_prompt
    conv_p = []
    zero_hist = jnp.zeros((n_p, HIST_PAD, CONV_CH), F32)
    for layer in range(N_A_LAYERS):
        xp, st = _conv_prompt(xp, zero_hist, *conv_args(layer))
        conv_p.append(st[:, HIST_PAD - CONV_HIST:])
        xp = _mlp(xp.reshape(n_p * t_p, D_MODEL), row(norm_mlp_g[layer]), wup[layer],
                  wdn[layer]).reshape(n_p, t_p, D_MODEL)
    xp = xp.reshape(n_p * t_p, D_MODEL)
    kv_p, kvb_p = _kv_proj(xp, row(kv_norm_g), wkv, e_mat, gk)
    kvb_p = kvb_p.reshape(n_p, t_p, 2 * HEADS_W)
    for j in range(N_B_LAYERS):
        layer = N_A_LAYERS + j
        q = _q_proj(xp, row(norm_mix_g[layer]), wq[j], e_mat, gq[j], BF16)
        q = q.reshape(n_p, t_p, N_GROUPS * HEADS_W)
        parts = []
        for g in range(N_GROUPS):
            parts.extend(_attn_group(q, kvb_p, bias_p[g], g))
        xp = _attn_out_mlp(xp, parts, wo[j], row(norm_mlp_g[layer]), wup[layer], wdn[layer])
    y_prompt = xp.reshape(n_p, t_p, D_MODEL)
    conv_prompt = jnp.stack(conv_p, axis=0)
    win = min(WINDOWS[-1][0], t_p)
    kv_prompt = kv_p.reshape(n_p, t_p, 2, N_HEADS, HEAD_DIM)[:, t_p - win:]

    xs = jnp.transpose(x_sample, (1, 0, 2))
    st_s = jnp.transpose(state_conv, (0, 2, 1, 3))
    conv_s = []
    for layer in range(N_A_LAYERS):
        xs, st = _conv_sample(xs, st_s[layer], *conv_args(layer))
        conv_s.append(jnp.transpose(st, (1, 0, 2)))
        xs = _mlp(xs.reshape(t_s * n_s, D_MODEL), row(norm_mlp_g[layer]), wup[layer],
                  wdn[layer]).reshape(t_s, n_s, D_MODEL)
    xs = xs.reshape(t_s * n_s, D_MODEL)
    kv_s, _ = _kv_proj(xs, row(kv_norm_g), wkv, e_mat, gk)
    cache = cache_kv.reshape(n_s, past, 2 * HEADS_W)
    kvn = kv_s.reshape(t_s, n_s, 2 * HEADS_W)
    for j in range(N_B_LAYERS):
        layer = N_A_LAYERS + j
        q = _q_proj(xs, row(norm_mix_g[layer]), wq[j], e_mat, gq[j], F32)
        o = _attn_sample(q.reshape(t_s, n_s, N_GROUPS * HEADS_W), kvn, cache, b0, b12, sl0)
        xs = _attn_out_mlp(xs, [o], wo[j], row(norm_mlp_g[layer]), wup[layer], wdn[layer])
    y_sample = jnp.transpose(xs.reshape(t_s, n_s, D_MODEL), (1, 0, 2))
    conv_sample = jnp.stack(conv_s, axis=0)
    kv_sample = jnp.transpose(kvn, (1, 0, 2)).reshape(n_s, t_s, 2, N_HEADS, HEAD_DIM)

    return (y_prompt, y_sample, conv_prompt, conv_sample, kv_prompt, kv_sample)
```

```python
import functools

import jax
import jax.numpy as jnp
from jax import lax
from jax.experimental import pallas as pl
from jax.experimental.pallas import tpu as pltpu

F32 = jnp.float32
BF16 = jnp.bfloat16

D_MODEL = 1024
CONV_CH = 1024
CONV_WIDTH = 31
CONV_HIST = CONV_WIDTH - 1
HIST_PAD = 32
D_FF = 4096
N_HEADS = 8
HEAD_DIM = 64
HEADS_W = N_HEADS * HEAD_DIM
KV_W = 2 * HEADS_W
WINDOWS = ((128, 1), (512, 4), (2048, 16))
N_GROUPS = len(WINDOWS)
BAND = 128
N_A_LAYERS = 2
N_B_LAYERS = 2
NORM_EPS = 1e-6
NEG_INF = float("-inf")
LANES = 128
SUBLANES = 8

ROW_TILE = 512
ATT_OUT_TILE = 256
FF_CHUNK = 1024
ATT_QB = 512
SAMPLE_CONV_NB = 32
SAMPLE_ROWS = 8
MIB = 1024 * 1024

assert all(w // d == BAND for w, d in WINDOWS)


def _const_spec(shape):
    nd = len(shape)
    return pl.BlockSpec(shape, lambda *_: (0,) * nd, pipeline_mode=pl.Buffered(1))


def _params(semantics, vmem_mib):
    return pltpu.CompilerParams(dimension_semantics=semantics, vmem_limit_bytes=vmem_mib * MIB)


def _lane_block(s, width=LANES):
    return slice(s * width, (s + 1) * width)


def _rms(x, g):
    return x * lax.rsqrt(jnp.mean(x * x, axis=-1, keepdims=True) + NORM_EPS) * g


def _mlp_residual(x, g_ref, wup_ref, wdn_ref):
    h = _rms(x, g_ref[...]).astype(BF16)
    acc = x
    for c in range(D_FF // FF_CHUNK):
        sl = _lane_block(c, FF_CHUNK)
        z = jnp.maximum(jnp.dot(h, wup_ref[:, sl], preferred_element_type=F32), 0.0)
        acc = acc + jnp.dot((z * z).astype(BF16), wdn_ref[sl, :], preferred_element_type=F32)
    return acc


def _head_mean_sq(v, e_ref):
    return jnp.dot((v * v).astype(BF16), e_ref[...], preferred_element_type=F32)


def _dot_t(a, b):
    return lax.dot_general(a, b, (((1,), (1,)), ((), ())), preferred_element_type=F32)


def _deinterleave_store(val, slab_ref, out_ref, d):
    rows, width = val.shape
    for s in range(width // LANES):
        slab_ref[s] = val[:, _lane_block(s)]
    for s in range(width // LANES):
        for r in range(d):
            out_ref[r, :, _lane_block(s)] = slab_ref[s, pl.ds(r, rows // d, stride=d), :].astype(BF16)


def _mlp_kernel(x_ref, g_ref, wup_ref, wdn_ref, o_ref):
    o_ref[...] = _mlp_residual(x_ref[...], g_ref, wup_ref, wdn_ref)


def _mlp(x, g, wup, wdn):
    m = x.shape[0]
    row = pl.BlockSpec((ROW_TILE, D_MODEL), lambda i: (i, 0))
    return pl.pallas_call(
        _mlp_kernel,
        out_shape=jax.ShapeDtypeStruct((m, D_MODEL), F32),
        grid=(m // ROW_TILE,),
        in_specs=[row, _const_spec((1, D_MODEL)), _const_spec((D_MODEL, D_FF)),
                  _const_spec((D_FF, D_MODEL))],
        out_specs=row,
        compiler_params=_params(("arbitrary",), 48),
        name="mlp",
    )(x, g, wup, wdn)


def _glu(h, w1_ref, b1_ref):
    u = jnp.dot(h, w1_ref[...], preferred_element_type=F32) + b1_ref[...]
    return u[:, :CONV_CH] * jax.nn.sigmoid(u[:, CONV_CH:])


def _conv_tail(c, x, bdw_ref, lng_ref, lnb_ref, w2_ref, b2_ref):
    c = c + bdw_ref[...]
    mu = jnp.mean(c, axis=-1, keepdims=True)
    xc = c - mu
    y = xc * lax.rsqrt(jnp.mean(xc * xc, axis=-1, keepdims=True) + NORM_EPS)
    y = y * lng_ref[...] + lnb_ref[...]
    y = y * jax.nn.sigmoid(y)
    out = jnp.dot(y.astype(BF16), w2_ref[...], preferred_element_type=F32) + b2_ref[...]
    return x + out


def _conv_prompt_kernel(x_ref, st_ref, g_ref, w1_ref, b1_ref, wdw_ref, bdw_ref, lng_ref, lnb_ref,
                        w2_ref, b2_ref, o_ref, sto_ref, buf_ref):
    n_slabs = CONV_CH // LANES

    @pl.when(pl.program_id(1) == 0)
    def _():
        for s in range(n_slabs):
            buf_ref[s, 0:HIST_PAD, :] = st_ref[:, _lane_block(s)]

    x = x_ref[...]
    h = _rms(x, g_ref[...]).astype(BF16)
    u = _glu(h, w1_ref, b1_ref)
    for s in range(n_slabs):
        buf_ref[s, HIST_PAD:HIST_PAD + ROW_TILE, :] = u[:, _lane_block(s)]
    off = HIST_PAD - CONV_HIST
    cs = []
    for s in range(n_slabs):
        ls = _lane_block(s)
        c = buf_ref[s, off:off + ROW_TILE, :] * wdw_ref[0:1, ls]
        for k in range(1, CONV_WIDTH):
            c = c + buf_ref[s, off + k:off + k + ROW_TILE, :] * wdw_ref[k:k + 1, ls]
        cs.append(c)
    c = jnp.concatenate(cs, axis=1)
    o_ref[...] = _conv_tail(c, x, bdw_ref, lng_ref, lnb_ref, w2_ref, b2_ref)
    for s in range(n_slabs):
        last = buf_ref[s, ROW_TILE:ROW_TILE + HIST_PAD, :]
        sto_ref[:, _lane_block(s)] = last
        buf_ref[s, 0:HIST_PAD, :] = last


def _conv_prompt(x, st, g, w1, b1, wdw, bdw, lng, lnb, w2, b2):
    n, t, _ = x.shape
    row = pl.BlockSpec((None, ROW_TILE, D_MODEL), lambda b, i: (b, i, 0))
    st_spec = pl.BlockSpec((None, HIST_PAD, CONV_CH), lambda b, i: (b, 0, 0))
    return pl.pallas_call(
        _conv_prompt_kernel,
        out_shape=(jax.ShapeDtypeStruct((n, t, D_MODEL), F32),
                   jax.ShapeDtypeStruct((n, HIST_PAD, CONV_CH), F32)),
        grid=(n, t // ROW_TILE),
        in_specs=[row, st_spec, _const_spec((1, D_MODEL)),
                  _const_spec((D_MODEL, 2 * CONV_CH)), _const_spec((1, 2 * CONV_CH)),
                  _const_spec((CONV_WIDTH, CONV_CH)), _const_spec((1, CONV_CH)),
                  _const_spec((1, CONV_CH)), _const_spec((1, CONV_CH)),
                  _const_spec((CONV_CH, D_MODEL)), _const_spec((1, D_MODEL))],
        out_specs=(row, st_spec),
        scratch_shapes=[pltpu.VMEM((CONV_CH // LANES, HIST_PAD + ROW_TILE, LANES), F32)],
        compiler_params=_params(("arbitrary", "arbitrary"), 48),
        name="conv_prompt",
    )(x, st, g, w1, b1, wdw, bdw, lng, lnb, w2, b2)


def _conv_sample_kernel(x_ref, st_ref, g_ref, w1_ref, b1_ref, wdw_ref, bdw_ref, lng_ref, lnb_ref,
                        w2_ref, b2_ref, o_ref, sto_ref):
    t_new, nb, _ = x_ref.shape
    x = x_ref[...].reshape(t_new * nb, D_MODEL)
    h = _rms(x, g_ref[...]).astype(BF16)
    glu = _glu(h, w1_ref, b1_ref)
    new = [glu[t * nb:(t + 1) * nb, :] for t in range(t_new)]

    def full(j):
        return st_ref[j] if j < CONV_HIST else new[j - CONV_HIST]

    outs = []
    for t in range(t_new):
        c = full(t) * wdw_ref[0:1, :]
        for k in range(1, CONV_WIDTH):
            c = c + full(t + k) * wdw_ref[k:k + 1, :]
        outs.append(c)
    c = jnp.concatenate(outs, axis=0)
    o_ref[...] = _conv_tail(c, x, bdw_ref, lng_ref, lnb_ref, w2_ref, b2_ref).reshape(t_new, nb, D_MODEL)
    for j in range(CONV_HIST):
        sto_ref[j] = full(j + t_new)


def _conv_sample(x, st, g, w1, b1, wdw, bdw, lng, lnb, w2, b2):
    t_new, n, _ = x.shape
    nb = SAMPLE_CONV_NB
    x_spec = pl.BlockSpec((t_new, nb, D_MODEL), lambda i: (0, i, 0))
    st_spec = pl.BlockSpec((CONV_HIST, nb, CONV_CH), lambda i: (0, i, 0))
    return pl.pallas_call(
        _conv_sample_kernel,
        out_shape=(jax.ShapeDtypeStruct((t_new, n, D_MODEL), F32),
                   jax.ShapeDtypeStruct((CONV_HIST, n, CONV_CH), F32)),
        grid=(n // nb,),
        in_specs=[x_spec, st_spec, _const_spec((1, D_MODEL)),
                  _const_spec((D_MODEL, 2 * CONV_CH)), _const_spec((1, 2 * CONV_CH)),
                  _const_spec((CONV_WIDTH, CONV_CH)), _const_spec((1, CONV_CH)),
                  _const_spec((1, CONV_CH)), _const_spec((1, CONV_CH)),
                  _const_spec((CONV_CH, D_MODEL)), _const_spec((1, D_MODEL))],
        out_specs=(x_spec, st_spec),
        compiler_params=_params(("arbitrary",), 48),
        name="conv_sample",
    )(x, st, g, w1, b1, wdw, bdw, lng, lnb, w2, b2)


def _kv_rows(x_ref, g_ref, wkv_ref, e_ref, gk_ref):
    h = _rms(x_ref[...], g_ref[...]).astype(BF16)
    k = jnp.dot(h, wkv_ref[:, :HEADS_W], preferred_element_type=F32)
    v = jnp.dot(h, wkv_ref[:, HEADS_W:], preferred_element_type=F32)
    k = k * lax.rsqrt(_head_mean_sq(k, e_ref) + NORM_EPS) * gk_ref[...]
    return jnp.concatenate([k, v], axis=1)


def _kv_sample_kernel(x_ref, g_ref, wkv_ref, e_ref, gk_ref, kv_ref):
    kv_ref[...] = _kv_rows(x_ref, g_ref, wkv_ref, e_ref, gk_ref)


def _kv_prompt_kernel(x_ref, g_ref, wkv_ref, e_ref, gk_ref, kv_ref, kb0_ref, kb1_ref, kb2_ref, slab_ref):
    kv = _kv_rows(x_ref, g_ref, wkv_ref, e_ref, gk_ref)
    kv_ref[...] = kv
    kb0_ref[0] = kv.astype(BF16)
    _deinterleave_store(kv, slab_ref, kb1_ref, WINDOWS[1][1])
    _deinterleave_store(kv, slab_ref, kb2_ref, WINDOWS[2][1])


def _kv_weight_specs():
    return [_const_spec((1, D_MODEL)), _const_spec((D_MODEL, KV_W)),
            _const_spec((HEADS_W, HEADS_W)), _const_spec((1, HEADS_W))]


def _kv_proj_sample(x, g, wkv, e, gk):
    m = x.shape[0]
    return pl.pallas_call(
        _kv_sample_kernel,
        out_shape=jax.ShapeDtypeStruct((m, KV_W), F32),
        grid=(m // ROW_TILE,),
        in_specs=[pl.BlockSpec((ROW_TILE, D_MODEL), lambda i: (i, 0))] + _kv_weight_specs(),
        out_specs=pl.BlockSpec((ROW_TILE, KV_W), lambda i: (i, 0)),
        compiler_params=_params(("arbitrary",), 32),
        name="kv_proj_sample",
    )(x, g, wkv, e, gk)


def _dilated_shapes(n, t, width, dtype):
    sds, specs = [], []
    for _, d in WINDOWS:
        sds.append(jax.ShapeDtypeStruct((n, d, t // d, width), dtype))
        specs.append(pl.BlockSpec((None, d, ROW_TILE // d, width), lambda b, i: (b, 0, i, 0)))
    return sds, specs


def _kv_proj_prompt(x, g, wkv, e, gk):
    n, t, _ = x.shape
    sds, specs = _dilated_shapes(n, t, KV_W, BF16)
    return pl.pallas_call(
        _kv_prompt_kernel,
        out_shape=[jax.ShapeDtypeStruct((n, t, KV_W), F32)] + sds,
        grid=(n, t // ROW_TILE),
        in_specs=[pl.BlockSpec((None, ROW_TILE, D_MODEL), lambda b, i: (b, i, 0))] + _kv_weight_specs(),
        out_specs=[pl.BlockSpec((None, ROW_TILE, KV_W), lambda b, i: (b, i, 0))] + specs,
        scratch_shapes=[pltpu.VMEM((KV_W // LANES, ROW_TILE, LANES), F32)],
        compiler_params=_params(("arbitrary", "arbitrary"), 40),
        name="kv_proj_prompt",
    )(x, g, wkv, e, gk)


def _q_group(h, wq_ref, e_ref, gq_ref, grp):
    q = jnp.dot(h, wq_ref[:, _lane_block(grp, HEADS_W)], preferred_element_type=F32)
    return q * lax.rsqrt(_head_mean_sq(q, e_ref) + NORM_EPS) * gq_ref[...]


def _q_sample_kernel(x_ref, g_ref, wq_ref, e_ref, gq_ref, q_ref):
    h = _rms(x_ref[...], g_ref[...]).astype(BF16)
    for grp in range(N_GROUPS):
        q = _q_group(h, wq_ref, e_ref, gq_ref, grp)
        q_ref[:, _lane_block(grp, HEADS_W)] = q.astype(BF16).astype(F32)


def _q_prompt_kernel(x_ref, g_ref, wq_ref, e_ref, gq_ref, q0_ref, q1_ref, q2_ref, slab_ref):
    h = _rms(x_ref[...], g_ref[...]).astype(BF16)
    q0_ref[0] = _q_group(h, wq_ref, e_ref, gq_ref, 0).astype(BF16)
    _deinterleave_store(_q_group(h, wq_ref, e_ref, gq_ref, 1), slab_ref, q1_ref, WINDOWS[1][1])
    _deinterleave_store(_q_group(h, wq_ref, e_ref, gq_ref, 2), slab_ref, q2_ref, WINDOWS[2][1])


def _q_weight_specs():
    return [_const_spec((1, D_MODEL)), _const_spec((D_MODEL, N_GROUPS * HEADS_W)),
            _const_spec((HEADS_W, HEADS_W)), _const_spec((1, HEADS_W))]


def _q_proj_sample(x, g, wq, e, gq):
    m = x.shape[0]
    qw = N_GROUPS * HEADS_W
    return pl.pallas_call(
        _q_sample_kernel,
        out_shape=jax.ShapeDtypeStruct((m, qw), F32),
        grid=(m // ROW_TILE,),
        in_specs=[pl.BlockSpec((ROW_TILE, D_MODEL), lambda i: (i, 0))] + _q_weight_specs(),
        out_specs=pl.BlockSpec((ROW_TILE, qw), lambda i: (i, 0)),
        compiler_params=_params(("arbitrary",), 32),
        name="q_proj_sample",
    )(x, g, wq, e, gq)


def _q_proj_prompt(x, g, wq, e, gq):
    n, t, _ = x.shape
    sds, specs = _dilated_shapes(n, t, HEADS_W, BF16)
    return pl.pallas_call(
        _q_prompt_kernel,
        out_shape=sds,
        grid=(n, t // ROW_TILE),
        in_specs=[pl.BlockSpec((None, ROW_TILE, D_MODEL), lambda b, i: (b, i, 0))] + _q_weight_specs(),
        out_specs=specs,
        scratch_shapes=[pltpu.VMEM((HEADS_W // LANES, ROW_TILE, LANES), F32)],
        compiler_params=_params(("arbitrary", "arbitrary"), 32),
        name="q_proj_prompt",
    )(x, g, wq, e, gq)


def _attn_group_kernel(q_ref, kc_ref, kp_ref, vc_ref, vp_ref, bias_ref, o_ref, lse_ref, kk_ref, vv_ref):
    kk_ref[0:BAND, :] = kp_ref[...]
    kk_ref[BAND:, :] = kc_ref[...]
    vv_ref[0:BAND, :] = vp_ref[...]
    vv_ref[BAND:, :] = vc_ref[...]
    first_block = pl.program_id(2) == 0
    key_col = lax.broadcasted_iota(jnp.int32, (BAND, 2 * BAND), 1)
    lane = lax.broadcasted_iota(jnp.int32, (BAND, LANES), 1)
    left = lane < HEAD_DIM
    lane_row = lax.broadcasted_iota(jnp.int32, (1, LANES), 1)
    head_keep = [jnp.where(lane_row < HEAD_DIM, 1.0, 0.0).astype(BF16),
                 jnp.where(lane_row < HEAD_DIM, 0.0, 1.0).astype(BF16)]
    no_history = jnp.logical_and(first_block, key_col < BAND)

    for sb in range(q_ref.shape[0] // BAND):
        r0 = sb * BAND
        stats = jnp.zeros((BAND, LANES), F32)
        for pair in range(N_HEADS // 2):
            ls = _lane_block(pair)
            q2 = q_ref[r0:r0 + BAND, ls]
            k2 = kk_ref[r0:r0 + 2 * BAND, ls]
            v2 = vv_ref[r0:r0 + 2 * BAND, ls]
            outs = []
            for hh in range(2):
                s = _dot_t(q2 * head_keep[hh], k2) + bias_ref[2 * pair + hh]
                if sb == 0:
                    s = jnp.where(no_history, NEG_INF, s)
                m = jnp.max(s, axis=-1, keepdims=True)
                p = jnp.exp(s - m)
                l = jnp.sum(p, axis=-1, keepdims=True)
                a = jnp.dot(p.astype(BF16), v2, preferred_element_type=F32)
                outs.append(a * (1.0 / l))
                stats = jnp.where(lane == 2 * pair + hh, m + jnp.log(l), stats)
            o_ref[r0:r0 + BAND, ls] = jnp.where(left, outs[0], outs[1])
        lse_ref[r0:r0 + BAND, :] = stats


def _attn_group(q, kvb, bias, grp):
    n, d, rows, _ = q.shape
    qb = ATT_QB
    per_blk = qb // BAND

    def prev_idx(i):
        return jnp.maximum(i * per_blk - 1, 0)

    def cur(col):
        return pl.BlockSpec((None, None, qb, HEADS_W), lambda b, r, i: (b, r, i, col))

    def prev(col):
        return pl.BlockSpec((None, None, BAND, HEADS_W), lambda b, r, i: (b, r, prev_idx(i), col))

    return pl.pallas_call(
        _attn_group_kernel,
        out_shape=(jax.ShapeDtypeStruct((n, d, rows, HEADS_W), F32),
                   jax.ShapeDtypeStruct((n, d, rows, LANES), F32)),
        grid=(n, d, rows // qb),
        in_specs=[cur(0), cur(0), prev(0), cur(1), prev(1), _const_spec((N_HEADS, BAND, 2 * BAND))],
        out_specs=(cur(0), pl.BlockSpec((None, None, qb, LANES), lambda b, r, i: (b, r, i, 0))),
        scratch_shapes=[pltpu.VMEM((BAND + qb, HEADS_W), BF16),
                        pltpu.VMEM((BAND + qb, HEADS_W), BF16)],
        compiler_params=_params(("arbitrary", "arbitrary", "arbitrary"), 32),
        name=f"attn_group{grp}",
    )(q, kvb, kvb, kvb, kvb, bias)


def _attn_sample_kernel(q_ref, kvn_ref, ca_ref, cb_ref, b0_ref, b1_ref, b2_ref, b2n_ref, o_ref):
    t_new = q_ref.shape[0]
    b = pl.program_id(0) % SAMPLE_ROWS
    n_rows = t_new * N_HEADS
    hsel = (lax.broadcasted_iota(jnp.int32, (N_HEADS, HEADS_W), 1) // HEAD_DIM
            == lax.broadcasted_iota(jnp.int32, (N_HEADS, HEADS_W), 0))

    def q_rows(grp):
        rows = [jnp.where(hsel, q_ref[t, pl.ds(b, 1), _lane_block(grp, HEADS_W)], 0.0)
                for t in range(t_new)]
        return jnp.concatenate(rows, axis=0).astype(BF16)

    row8 = lax.broadcasted_iota(jnp.int32, (SUBLANES, KV_W), 0)
    new8 = jnp.zeros((SUBLANES, KV_W), F32)
    for t in range(t_new):
        new8 = jnp.where(row8 == t, kvn_ref[t, pl.ds(b, 1), :], new8)
    new = jnp.concatenate([new8, jnp.zeros((BAND - SUBLANES, KV_W), F32)], axis=0).astype(BF16)
    recent = ca_ref[...].astype(BF16)
    ka = jnp.concatenate([recent[:, :HEADS_W], new[:, :HEADS_W]], axis=0)
    va = jnp.concatenate([recent[:, HEADS_W:], new[:, HEADS_W:]], axis=0)
    far = cb_ref[...].reshape(-1, KV_W).astype(BF16)
    kb, vb = far[:, :HEADS_W], far[:, HEADS_W:]
    n_a = ka.shape[0]

    def softmax_pv(pieces):
        m = functools.reduce(jnp.maximum, [jnp.max(s, axis=-1, keepdims=True) for s, _ in pieces])
        l = jnp.zeros((n_rows, 1), F32)
        acc = jnp.zeros((n_rows, HEADS_W), F32)
        for s, v in pieces:
            p = jnp.exp(s - m)
            l = l + jnp.sum(p, axis=-1, keepdims=True)
            acc = acc + jnp.dot(p.astype(BF16), v, preferred_element_type=F32)
        return acc, m, l

    q0, q1, q2 = q_rows(0), q_rows(1), q_rows(2)
    lo0 = n_a - 2 * BAND
    parts = [
        softmax_pv([(_dot_t(q0, ka[lo0:]) + b0_ref[...], va[lo0:])]),
        softmax_pv([(_dot_t(q1, ka) + b1_ref[...], va)]),
        softmax_pv([(_dot_t(q2, kb) + b2_ref[...], vb),
                    (_dot_t(q2, ka[n_a - BAND:]) + b2n_ref[...], va[n_a - BAND:])]),
    ]
    m_all = functools.reduce(jnp.maximum, [m for _, m, _ in parts])
    num = jnp.zeros((n_rows, HEADS_W), F32)
    den = jnp.zeros((n_rows, 1), F32)
    for acc, m, l in parts:
        e = jnp.exp(m - m_all)
        num = num + e * acc
        den = den + e * l
    out = num * (1.0 / den)
    for t in range(t_new):
        own = jnp.where(hsel, out[t * N_HEADS:(t + 1) * N_HEADS, :], 0.0)
        o_ref[t, pl.ds(b, 1), :] = jnp.sum(own, axis=0, keepdims=True)


def _attn_sample(q, kvn, cache, biases):
    t_new, n, qw = q.shape
    p = cache.shape[1]
    d1, d2 = WINDOWS[1][1], WINDOWS[2][1]
    assert p == WINDOWS[2][0] and p % d2 == 0 and t_new <= SUBLANES and d1 * BAND <= p
    cache_b = cache.reshape(n, p // d2, d2, KV_W)
    blk = lambda width: pl.BlockSpec((t_new, SAMPLE_ROWS, width), lambda i: (0, i // SAMPLE_ROWS, 0))
    n_rows = t_new * N_HEADS
    return pl.pallas_call(
        _attn_sample_kernel,
        out_shape=jax.ShapeDtypeStruct((t_new, n, HEADS_W), F32),
        grid=(n,),
        in_specs=[blk(qw), blk(KV_W),
                  pl.BlockSpec((None, d1 * BAND, KV_W), lambda i: (i, p // (d1 * BAND) - 1, 0)),
                  pl.BlockSpec((None, p // d2, SUBLANES, KV_W), lambda i: (i, 0, 0, 0)),
                  _const_spec((n_rows, 2 * BAND)), _const_spec((n_rows, (d1 + 1) * BAND)),
                  _const_spec((n_rows, p // d2 * SUBLANES)), _const_spec((n_rows, BAND))],
        out_specs=blk(HEADS_W),
        compiler_params=_params(("arbitrary",), 40),
        name="attn_sample",
    )(q, kvn, cache, cache_b, *biases)


def _attn_out_sample_kernel(x_ref, o_ref, wo_ref, g_ref, wup_ref, wdn_ref, out_ref):
    x = x_ref[...] + jnp.dot(o_ref[...].astype(BF16), wo_ref[...], preferred_element_type=F32)
    out_ref[...] = _mlp_residual(x, g_ref, wup_ref, wdn_ref)


def _attn_out_prompt_kernel(x_ref, o0_ref, l0_ref, o1_ref, l1_ref, o2_ref, l2_ref, ex_ref, wo_ref,
                            g_ref, wup_ref, wdn_ref, out_ref, slab_ref):
    rows = x_ref.shape[0]
    n_o = HEADS_W // LANES
    for gi, (o_ref, l_ref) in enumerate(((o1_ref, l1_ref), (o2_ref, l2_ref))):
        d = WINDOWS[gi + 1][1]
        for r in range(d):
            dst = pl.ds(r, rows // d, stride=d)
            for s in range(n_o):
                slab_ref[gi, s, dst, :] = o_ref[r, :, _lane_block(s)]
            slab_ref[gi, n_o, dst, :] = l_ref[r]
    outs = [o0_ref[0]] + [jnp.concatenate([slab_ref[gi, s] for s in range(n_o)], axis=1) for gi in range(2)]
    lses = [l0_ref[0], slab_ref[0, n_o], slab_ref[1, n_o]]
    mx = functools.reduce(jnp.maximum, lses)
    es = [jnp.exp(l - mx) for l in lses]
    inv = 1.0 / functools.reduce(jnp.add, es)
    o = jnp.zeros((rows, HEADS_W), F32)
    for e, og in zip(es, outs):
        w = e * inv
        hi = w.astype(BF16)
        lo = (w - hi.astype(F32)).astype(BF16)
        wide = (jnp.dot(hi, ex_ref[...], preferred_element_type=F32)
                + jnp.dot(lo, ex_ref[...], preferred_element_type=F32))
        o = o + wide * og
    x = x_ref[...] + jnp.dot(o.astype(BF16), wo_ref[...], preferred_element_type=F32)
    out_ref[...] = _mlp_residual(x, g_ref, wup_ref, wdn_ref)


def _mlp_weight_specs():
    return [_const_spec((HEADS_W, D_MODEL)), _const_spec((1, D_MODEL)),
            _const_spec((D_MODEL, D_FF)), _const_spec((D_FF, D_MODEL))]


def _attn_out_sample(x, o, wo, g, wup, wdn):
    m = x.shape[0]
    row = pl.BlockSpec((ROW_TILE, D_MODEL), lambda i: (i, 0))
    return pl.pallas_call(
        _attn_out_sample_kernel,
        out_shape=jax.ShapeDtypeStruct((m, D_MODEL), F32),
        grid=(m // ROW_TILE,),
        in_specs=[row, pl.BlockSpec((ROW_TILE, HEADS_W), lambda i: (i, 0))] + _mlp_weight_specs(),
        out_specs=row,
        compiler_params=_params(("arbitrary",), 48),
        name="attn_out_sample",
    )(x, o, wo, g, wup, wdn)


def _attn_out_prompt(x, parts, ex, wo, g, wup, wdn):
    n, t, _ = x.shape
    tile = ATT_OUT_TILE
    row = pl.BlockSpec((None, tile, D_MODEL), lambda b, i: (b, i, 0))
    part_specs = []
    for _, d in WINDOWS:
        for width in (HEADS_W, LANES):
            part_specs.append(pl.BlockSpec((None, d, tile // d, width), lambda b, i: (b, 0, i, 0)))
    flat = [a for pair in parts for a in pair]
    return pl.pallas_call(
        _attn_out_prompt_kernel,
        out_shape=jax.ShapeDtypeStruct((n, t, D_MODEL), F32),
        grid=(n, t // tile),
        in_specs=[row] + part_specs + [_const_spec((LANES, HEADS_W))] + _mlp_weight_specs(),
        out_specs=row,
        scratch_shapes=[pltpu.VMEM((2, HEADS_W // LANES + 1, tile, LANES), F32)],
        compiler_params=_params(("arbitrary", "arbitrary"), 48),
        name="attn_out_prompt",
    )(x, *flat, ex, wo, g, wup, wdn)


def _alibi_slopes():
    n = N_GROUPS * N_HEADS
    s = 2.0 ** (-8.0 * jnp.arange(1, n + 1, dtype=F32) / n)
    return s.reshape(N_GROUPS, N_HEADS)


def _prompt_bias(slopes, grp):
    d = WINDOWS[grp][1]
    iq = jnp.arange(BAND, dtype=jnp.int32)[:, None]
    ik = jnp.arange(2 * BAND, dtype=jnp.int32)[None, :]
    steps = iq + BAND - ik
    valid = (steps >= 0) & (steps <= BAND)
    offs = (steps * d).astype(F32)
    return jnp.where(valid[None], -slopes[grp][:, None, None] * offs[None], NEG_INF)


def _sample_bias(slopes, t_new, past):
    d1, d2 = WINDOWS[1][1], WINDOWS[2][1]
    t = jnp.arange(t_new, dtype=jnp.int32)[:, None]

    def table(slope, dist, valid):
        b = jnp.where(valid[:, None, :], -slope[None, :, None] * dist.astype(F32)[:, None, :], NEG_INF)
        return b.reshape(t_new * N_HEADS, -1)

    u = jnp.arange(BAND, dtype=jnp.int32)[None, :]
    dist_n = t - u
    ok_n = (u < t_new) & (dist_n >= 0)
    c = jnp.arange(BAND, dtype=jnp.int32)[None, :]
    dist_c = BAND + t - c
    b0 = table(slopes[0], jnp.concatenate([dist_c, dist_n], axis=1),
               jnp.concatenate([dist_c <= WINDOWS[0][0], ok_n], axis=1))
    c = jnp.arange(d1 * BAND, dtype=jnp.int32)[None, :]
    dist_c = d1 * BAND + t - c
    ok_c = (dist_c % d1 == 0) & (dist_c <= WINDOWS[1][0])
    b1 = table(slopes[1], jnp.concatenate([dist_c, dist_n], axis=1),
               jnp.concatenate([ok_c, ok_n & (dist_n % d1 == 0)], axis=1))
    col = jnp.arange(past // d2 * SUBLANES, dtype=jnp.int32)[None, :]
    pos = (col // SUBLANES) * d2 + col % SUBLANES
    dist_c = past + t - pos
    ok_c = (dist_c % d2 == 0) & (dist_c <= WINDOWS[2][0])
    b2 = table(slopes[2], dist_c, ok_c)
    b2n = table(slopes[2], dist_n, ok_n & (dist_n % d2 == 0))
    return b0, b1, b2, b2n


def _head_mean_matrix():
    idx = jnp.arange(HEADS_W) // HEAD_DIM
    return jnp.where(idx[:, None] == idx[None, :], 1.0 / HEAD_DIM, 0.0).astype(BF16)


def _head_expand_matrix():
    lane = jnp.arange(LANES)[:, None]
    col = jnp.arange(HEADS_W)[None, :] // HEAD_DIM
    return jnp.where(lane == col, 1.0, 0.0).astype(BF16)


def kernel(x_prompt, x_sample, state_conv, cache_kv, norm_mix_g, norm_mlp_g, conv_w_pw1, conv_b_pw1,
           conv_w_dw, conv_b_dw, conv_ln_g, conv_ln_b, conv_w_pw2, conv_b_pw2, kv_norm_g, w_kv,
           k_norm_g, attn_w_q, q_norm_g, attn_w_o, mlp_w_up, mlp_w_down):
    n_p, t_p, _ = x_prompt.shape
    n_s, t_s, _ = x_sample.shape
    past = cache_kv.shape[1]
    scale = HEAD_DIM ** -0.5

    w1 = conv_w_pw1.astype(BF16)
    w2 = conv_w_pw2.astype(BF16)
    wkv = w_kv.astype(BF16)
    wq = attn_w_q.astype(BF16)
    wo = attn_w_o.astype(BF16)
    wup = mlp_w_up.astype(BF16)
    wdn = mlp_w_down.astype(BF16)
    row = lambda a: a.reshape(1, -1)
    e_mat = _head_mean_matrix()
    ex_mat = _head_expand_matrix()
    gk = row(jnp.tile(k_norm_g, N_HEADS))
    gq = [row(jnp.tile(q_norm_g[j], N_HEADS) * scale) for j in range(N_B_LAYERS)]
    slopes = _alibi_slopes()
    bias_p = [_prompt_bias(slopes, g) for g in range(N_GROUPS)]
    bias_s = _sample_bias(slopes, t_s, past)

    def conv_args(layer):
        return (row(norm_mix_g[layer]), w1[layer], row(conv_b_pw1[layer]), conv_w_dw[layer],
                row(conv_b_dw[layer]), row(conv_ln_g[layer]), row(conv_ln_b[layer]),
                w2[layer], row(conv_b_pw2[layer]))

    xp = x_prompt
    conv_p = []
    zero_hist = jnp.zeros((n_p, HIST_PAD, CONV_CH), F32)
    for layer in range(N_A_LAYERS):
        xp, st = _conv_prompt(xp, zero_hist, *conv_args(layer))
        conv_p.append(st[:, HIST_PAD - CONV_HIST:])
        xp = _mlp(xp.reshape(n_p * t_p, D_MODEL), row(norm_mlp_g[layer]), wup[layer],
                  wdn[layer]).reshape(n_p, t_p, D_MODEL)
    kv_p, *kvb_p = _kv_proj_prompt(xp, row(kv_norm_g), wkv, e_mat, gk)
    for j in range(N_B_LAYERS):
        layer = N_A_LAYERS + j
        qs = _q_proj_prompt(xp, row(norm_mix_g[layer]), wq[j], e_mat, gq[j])
        parts = [_attn_group(qs[g], kvb_p[g], bias_p[g], g) for g in range(N_GROUPS)]
        xp = _attn_out_prompt(xp, parts, ex_mat, wo[j], row(norm_mlp_g[layer]), wup[layer], wdn[layer])
    y_prompt = xp
    conv_prompt = jnp.stack(conv_p, axis=0)
    win = min(WINDOWS[-1][0], t_p)
    kv_prompt = kv_p.reshape(n_p, t_p, 2, N_HEADS, HEAD_DIM)[:, t_p - win:]

    xs = jnp.transpose(x_sample, (1, 0, 2))
    st_s = jnp.transpose(state_conv, (0, 2, 1, 3))
    conv_s = []
    for layer in range(N_A_LAYERS):
        xs, st = _conv_sample(xs, st_s[layer], *conv_args(layer))
        conv_s.append(jnp.transpose(st, (1, 0, 2)))
        xs = _mlp(xs.reshape(t_s * n_s, D_MODEL), row(norm_mlp_g[layer]), wup[layer],
                  wdn[layer]).reshape(t_s, n_s, D_MODEL)
    xs = xs.reshape(t_s * n_s, D_MODEL)
    kvn = _kv_proj_sample(xs, row(kv_norm_g), wkv, e_mat, gk).reshape(t_s, n_s, KV_W)
    cache = cache_kv.reshape(n_s, past, KV_W)
    for j in range(N_B_LAYERS):
        layer = N_A_LAYERS + j
        q = _q_proj_sample(xs, row(norm_mix_g[layer]), wq[j], e_mat, gq[j])
        o = _attn_sample(q.reshape(t_s, n_s, N_GROUPS * HEADS_W), kvn, cache, bias_s)
        xs = _attn_out_sample(xs, o.reshape(t_s * n_s, HEADS_W), wo[j], row(norm_mlp_g[layer]),
                              wup[layer], wdn[layer])
    y_sample = jnp.transpose(xs.reshape(t_s, n_s, D_MODEL), (1, 0, 2))
    conv_sample = jnp.stack(conv_s, axis=0)
    kv_sample = jnp.transpose(kvn, (1, 0, 2)).reshape(n_s, t_s, 2, N_HEADS, HEAD_DIM)

    return (y_prompt, y_sample, conv_prompt, conv_sample, kv_prompt, kv_sample)
```

```python
import functools

import jax
import jax.numpy as jnp
from jax import lax
from jax.experimental import pallas as pl
from jax.experimental.pallas import tpu as pltpu

F32 = jnp.float32
BF16 = jnp.bfloat16

D_MODEL = 1024
CONV_CH = 1024
CONV_WIDTH = 31
CONV_HIST = CONV_WIDTH - 1
HIST_PAD = 32
D_FF = 4096
N_HEADS = 8
HEAD_DIM = 64
HEADS_W = N_HEADS * HEAD_DIM
KV_W = 2 * HEADS_W
WINDOWS = ((128, 1), (512, 4), (2048, 16))
N_GROUPS = len(WINDOWS)
BAND = 128
N_A_LAYERS = 2
N_B_LAYERS = 2
NORM_EPS = 1e-6
NEG_INF = float("-inf")
LANES = 128
SUBLANES = 8

ROW_TILE = 512
ATT_OUT_TILE = 256
FF_CHUNK = 1024
ATT_QB = 512
SAMPLE_CONV_NB = 32
SAMPLE_ROWS = 8
SAMPLE_SEQ_STEP = 4
MXU_DIM = 256
MIB = 1024 * 1024

assert all(w // d == BAND for w, d in WINDOWS)


def _const_spec(shape):
    nd = len(shape)
    return pl.BlockSpec(shape, lambda *_: (0,) * nd, pipeline_mode=pl.Buffered(1))


def _params(semantics, vmem_mib):
    return pltpu.CompilerParams(dimension_semantics=semantics, vmem_limit_bytes=vmem_mib * MIB)


def _lane_block(s, width=LANES):
    return slice(s * width, (s + 1) * width)


def _rms(x, g):
    return x * lax.rsqrt(jnp.mean(x * x, axis=-1, keepdims=True) + NORM_EPS) * g


def _mlp_residual(x, g_ref, wup_ref, wdn_ref):
    h = _rms(x, g_ref[...]).astype(BF16)
    acc = x
    for c in range(D_FF // FF_CHUNK):
        sl = _lane_block(c, FF_CHUNK)
        z = jnp.maximum(jnp.dot(h, wup_ref[:, sl], preferred_element_type=F32), 0.0)
        acc = acc + jnp.dot((z * z).astype(BF16), wdn_ref[sl, :], preferred_element_type=F32)
    return acc


def _head_mean_sq(v, e_ref):
    return jnp.dot((v * v).astype(BF16), e_ref[...], preferred_element_type=F32)


def _dot_t(a, b):
    return lax.dot_general(a, b, (((1,), (1,)), ((), ())), preferred_element_type=F32)


def _deinterleave_store(val, slab_ref, out_ref, d):
    rows, width = val.shape
    for s in range(width // LANES):
        slab_ref[s] = val[:, _lane_block(s)]
    for s in range(width // LANES):
        for r in range(d):
            out_ref[r, :, _lane_block(s)] = slab_ref[s, pl.ds(r, rows // d, stride=d), :].astype(BF16)


def _mlp_kernel(x_ref, g_ref, wup_ref, wdn_ref, o_ref):
    o_ref[...] = _mlp_residual(x_ref[...], g_ref, wup_ref, wdn_ref)


def _mlp(x, g, wup, wdn):
    m = x.shape[0]
    row = pl.BlockSpec((ROW_TILE, D_MODEL), lambda i: (i, 0))
    return pl.pallas_call(
        _mlp_kernel,
        out_shape=jax.ShapeDtypeStruct((m, D_MODEL), F32),
        grid=(m // ROW_TILE,),
        in_specs=[row, _const_spec((1, D_MODEL)), _const_spec((D_MODEL, D_FF)),
                  _const_spec((D_FF, D_MODEL))],
        out_specs=row,
        compiler_params=_params(("arbitrary",), 48),
        name="mlp",
    )(x, g, wup, wdn)


def _glu(h, w1_ref, b1_ref):
    u = jnp.dot(h, w1_ref[...], preferred_element_type=F32) + b1_ref[...]
    return u[:, :CONV_CH] * jax.nn.sigmoid(u[:, CONV_CH:])


def _conv_tail(c, x, bdw_ref, lng_ref, lnb_ref, w2_ref, b2_ref):
    c = c + bdw_ref[...]
    mu = jnp.mean(c, axis=-1, keepdims=True)
    xc = c - mu
    y = xc * lax.rsqrt(jnp.mean(xc * xc, axis=-1, keepdims=True) + NORM_EPS)
    y = y * lng_ref[...] + lnb_ref[...]
    y = y * jax.nn.sigmoid(y)
    out = jnp.dot(y.astype(BF16), w2_ref[...], preferred_element_type=F32) + b2_ref[...]
    return x + out


def _conv_prompt_kernel(x_ref, st_ref, g_ref, w1_ref, b1_ref, wdw_ref, bdw_ref, lng_ref, lnb_ref,
                        w2_ref, b2_ref, o_ref, sto_ref, buf_ref):
    n_slabs = CONV_CH // LANES

    @pl.when(pl.program_id(1) == 0)
    def _():
        for s in range(n_slabs):
            buf_ref[s, 0:HIST_PAD, :] = st_ref[:, _lane_block(s)]

    x = x_ref[...]
    h = _rms(x, g_ref[...]).astype(BF16)
    u = _glu(h, w1_ref, b1_ref)
    for s in range(n_slabs):
        buf_ref[s, HIST_PAD:HIST_PAD + ROW_TILE, :] = u[:, _lane_block(s)]
    off = HIST_PAD - CONV_HIST
    cs = []
    for s in range(n_slabs):
        ls = _lane_block(s)
        c = buf_ref[s, off:off + ROW_TILE, :] * wdw_ref[0:1, ls]
        for k in range(1, CONV_WIDTH):
            c = c + buf_ref[s, off + k:off + k + ROW_TILE, :] * wdw_ref[k:k + 1, ls]
        cs.append(c)
    c = jnp.concatenate(cs, axis=1)
    o_ref[...] = _conv_tail(c, x, bdw_ref, lng_ref, lnb_ref, w2_ref, b2_ref)
    for s in range(n_slabs):
        last = buf_ref[s, ROW_TILE:ROW_TILE + HIST_PAD, :]
        sto_ref[:, _lane_block(s)] = last
        buf_ref[s, 0:HIST_PAD, :] = last


def _conv_prompt(x, st, g, w1, b1, wdw, bdw, lng, lnb, w2, b2):
    n, t, _ = x.shape
    row = pl.BlockSpec((None, ROW_TILE, D_MODEL), lambda b, i: (b, i, 0))
    st_spec = pl.BlockSpec((None, HIST_PAD, CONV_CH), lambda b, i: (b, 0, 0))
    return pl.pallas_call(
        _conv_prompt_kernel,
        out_shape=(jax.ShapeDtypeStruct((n, t, D_MODEL), F32),
                   jax.ShapeDtypeStruct((n, HIST_PAD, CONV_CH), F32)),
        grid=(n, t // ROW_TILE),
        in_specs=[row, st_spec, _const_spec((1, D_MODEL)),
                  _const_spec((D_MODEL, 2 * CONV_CH)), _const_spec((1, 2 * CONV_CH)),
                  _const_spec((CONV_WIDTH, CONV_CH)), _const_spec((1, CONV_CH)),
                  _const_spec((1, CONV_CH)), _const_spec((1, CONV_CH)),
                  _const_spec((CONV_CH, D_MODEL)), _const_spec((1, D_MODEL))],
        out_specs=(row, st_spec),
        scratch_shapes=[pltpu.VMEM((CONV_CH // LANES, HIST_PAD + ROW_TILE, LANES), F32)],
        compiler_params=_params(("arbitrary", "arbitrary"), 48),
        name="conv_prompt",
    )(x, st, g, w1, b1, wdw, bdw, lng, lnb, w2, b2)


def _conv_sample_kernel(x_ref, st_ref, g_ref, w1_ref, b1_ref, wdw_ref, bdw_ref, lng_ref, lnb_ref,
                        w2_ref, b2_ref, o_ref, sto_ref):
    t_new, nb, _ = x_ref.shape
    x = x_ref[...].reshape(t_new * nb, D_MODEL)
    h = _rms(x, g_ref[...]).astype(BF16)
    glu = _glu(h, w1_ref, b1_ref)
    new = [glu[t * nb:(t + 1) * nb, :] for t in range(t_new)]

    def full(j):
        return st_ref[j] if j < CONV_HIST else new[j - CONV_HIST]

    outs = []
    for t in range(t_new):
        c = full(t) * wdw_ref[0:1, :]
        for k in range(1, CONV_WIDTH):
            c = c + full(t + k) * wdw_ref[k:k + 1, :]
        outs.append(c)
    c = jnp.concatenate(outs, axis=0)
    o_ref[...] = _conv_tail(c, x, bdw_ref, lng_ref, lnb_ref, w2_ref, b2_ref).reshape(t_new, nb, D_MODEL)
    for j in range(CONV_HIST):
        sto_ref[j] = full(j + t_new)


def _conv_sample(x, st, g, w1, b1, wdw, bdw, lng, lnb, w2, b2):
    t_new, n, _ = x.shape
    nb = SAMPLE_CONV_NB
    x_spec = pl.BlockSpec((t_new, nb, D_MODEL), lambda i: (0, i, 0))
    st_spec = pl.BlockSpec((CONV_HIST, nb, CONV_CH), lambda i: (0, i, 0))
    return pl.pallas_call(
        _conv_sample_kernel,
        out_shape=(jax.ShapeDtypeStruct((t_new, n, D_MODEL), F32),
                   jax.ShapeDtypeStruct((CONV_HIST, n, CONV_CH), F32)),
        grid=(n // nb,),
        in_specs=[x_spec, st_spec, _const_spec((1, D_MODEL)),
                  _const_spec((D_MODEL, 2 * CONV_CH)), _const_spec((1, 2 * CONV_CH)),
                  _const_spec((CONV_WIDTH, CONV_CH)), _const_spec((1, CONV_CH)),
                  _const_spec((1, CONV_CH)), _const_spec((1, CONV_CH)),
                  _const_spec((CONV_CH, D_MODEL)), _const_spec((1, D_MODEL))],
        out_specs=(x_spec, st_spec),
        compiler_params=_params(("arbitrary",), 48),
        name="conv_sample",
    )(x, st, g, w1, b1, wdw, bdw, lng, lnb, w2, b2)


def _kv_rows(x_ref, g_ref, wkv_ref, e_ref, gk_ref):
    h = _rms(x_ref[...], g_ref[...]).astype(BF16)
    k = jnp.dot(h, wkv_ref[:, :HEADS_W], preferred_element_type=F32)
    v = jnp.dot(h, wkv_ref[:, HEADS_W:], preferred_element_type=F32)
    k = k * lax.rsqrt(_head_mean_sq(k, e_ref) + NORM_EPS) * gk_ref[...]
    return jnp.concatenate([k, v], axis=1)


def _kv_sample_kernel(x_ref, g_ref, wkv_ref, e_ref, gk_ref, kv_ref):
    kv_ref[...] = _kv_rows(x_ref, g_ref, wkv_ref, e_ref, gk_ref)


def _kv_prompt_kernel(x_ref, g_ref, wkv_ref, e_ref, gk_ref, kv_ref, kb0_ref, kb1_ref, kb2_ref, slab_ref):
    kv = _kv_rows(x_ref, g_ref, wkv_ref, e_ref, gk_ref)
    kv_ref[...] = kv
    kb0_ref[0] = kv.astype(BF16)
    _deinterleave_store(kv, slab_ref, kb1_ref, WINDOWS[1][1])
    _deinterleave_store(kv, slab_ref, kb2_ref, WINDOWS[2][1])


def _kv_weight_specs():
    return [_const_spec((1, D_MODEL)), _const_spec((D_MODEL, KV_W)),
            _const_spec((HEADS_W, HEADS_W)), _const_spec((1, HEADS_W))]


def _kv_proj_sample(x, g, wkv, e, gk):
    m = x.shape[0]
    return pl.pallas_call(
        _kv_sample_kernel,
        out_shape=jax.ShapeDtypeStruct((m, KV_W), F32),
        grid=(m // ROW_TILE,),
        in_specs=[pl.BlockSpec((ROW_TILE, D_MODEL), lambda i: (i, 0))] + _kv_weight_specs(),
        out_specs=pl.BlockSpec((ROW_TILE, KV_W), lambda i: (i, 0)),
        compiler_params=_params(("arbitrary",), 32),
        name="kv_proj_sample",
    )(x, g, wkv, e, gk)


def _dilated_shapes(n, t, width, dtype):
    sds, specs = [], []
    for _, d in WINDOWS:
        sds.append(jax.ShapeDtypeStruct((n, d, t // d, width), dtype))
        specs.append(pl.BlockSpec((None, d, ROW_TILE // d, width), lambda b, i: (b, 0, i, 0)))
    return sds, specs


def _kv_proj_prompt(x, g, wkv, e, gk):
    n, t, _ = x.shape
    sds, specs = _dilated_shapes(n, t, KV_W, BF16)
    return pl.pallas_call(
        _kv_prompt_kernel,
        out_shape=[jax.ShapeDtypeStruct((n, t, KV_W), F32)] + sds,
        grid=(n, t // ROW_TILE),
        in_specs=[pl.BlockSpec((None, ROW_TILE, D_MODEL), lambda b, i: (b, i, 0))] + _kv_weight_specs(),
        out_specs=[pl.BlockSpec((None, ROW_TILE, KV_W), lambda b, i: (b, i, 0))] + specs,
        scratch_shapes=[pltpu.VMEM((KV_W // LANES, ROW_TILE, LANES), F32)],
        compiler_params=_params(("arbitrary", "arbitrary"), 40),
        name="kv_proj_prompt",
    )(x, g, wkv, e, gk)


def _q_group(h, wq_ref, e_ref, gq_ref, grp):
    q = jnp.dot(h, wq_ref[:, _lane_block(grp, HEADS_W)], preferred_element_type=F32)
    return q * lax.rsqrt(_head_mean_sq(q, e_ref) + NORM_EPS) * gq_ref[...]


def _q_sample_kernel(x_ref, g_ref, wq_ref, e_ref, gq_ref, q_ref):
    h = _rms(x_ref[...], g_ref[...]).astype(BF16)
    for grp in range(N_GROUPS):
        q = _q_group(h, wq_ref, e_ref, gq_ref, grp)
        q_ref[:, _lane_block(grp, HEADS_W)] = q.astype(BF16).astype(F32)


def _q_prompt_kernel(x_ref, g_ref, wq_ref, e_ref, gq_ref, q0_ref, q1_ref, q2_ref, slab_ref):
    h = _rms(x_ref[...], g_ref[...]).astype(BF16)
    q0_ref[0] = _q_group(h, wq_ref, e_ref, gq_ref, 0).astype(BF16)
    _deinterleave_store(_q_group(h, wq_ref, e_ref, gq_ref, 1), slab_ref, q1_ref, WINDOWS[1][1])
    _deinterleave_store(_q_group(h, wq_ref, e_ref, gq_ref, 2), slab_ref, q2_ref, WINDOWS[2][1])


def _q_weight_specs():
    return [_const_spec((1, D_MODEL)), _const_spec((D_MODEL, N_GROUPS * HEADS_W)),
            _const_spec((HEADS_W, HEADS_W)), _const_spec((1, HEADS_W))]


def _q_proj_sample(x, g, wq, e, gq):
    m = x.shape[0]
    qw = N_GROUPS * HEADS_W
    return pl.pallas_call(
        _q_sample_kernel,
        out_shape=jax.ShapeDtypeStruct((m, qw), F32),
        grid=(m // ROW_TILE,),
        in_specs=[pl.BlockSpec((ROW_TILE, D_MODEL), lambda i: (i, 0))] + _q_weight_specs(),
        out_specs=pl.BlockSpec((ROW_TILE, qw), lambda i: (i, 0)),
        compiler_params=_params(("arbitrary",), 32),
        name="q_proj_sample",
    )(x, g, wq, e, gq)


def _q_proj_prompt(x, g, wq, e, gq):
    n, t, _ = x.shape
    sds, specs = _dilated_shapes(n, t, HEADS_W, BF16)
    return pl.pallas_call(
        _q_prompt_kernel,
        out_shape=sds,
        grid=(n, t // ROW_TILE),
        in_specs=[pl.BlockSpec((None, ROW_TILE, D_MODEL), lambda b, i: (b, i, 0))] + _q_weight_specs(),
        out_specs=specs,
        scratch_shapes=[pltpu.VMEM((HEADS_W // LANES, ROW_TILE, LANES), F32)],
        compiler_params=_params(("arbitrary", "arbitrary"), 32),
        name="q_proj_prompt",
    )(x, g, wq, e, gq)


def _attn_group_kernel(q_ref, kc_ref, kp_ref, vc_ref, vp_ref, bias_ref, o_ref, lse_ref, kk_ref, vv_ref):
    kk_ref[0:BAND, :] = kp_ref[...]
    kk_ref[BAND:, :] = kc_ref[...]
    vv_ref[0:BAND, :] = vp_ref[...]
    vv_ref[BAND:, :] = vc_ref[...]
    first_block = pl.program_id(2) == 0
    key_col = lax.broadcasted_iota(jnp.int32, (BAND, 2 * BAND), 1)
    lane = lax.broadcasted_iota(jnp.int32, (BAND, LANES), 1)
    left = lane < HEAD_DIM
    lane_row = lax.broadcasted_iota(jnp.int32, (1, LANES), 1)
    head_keep = [jnp.where(lane_row < HEAD_DIM, 1.0, 0.0).astype(BF16),
                 jnp.where(lane_row < HEAD_DIM, 0.0, 1.0).astype(BF16)]
    no_history = jnp.logical_and(first_block, key_col < BAND)

    for sb in range(q_ref.shape[0] // BAND):
        r0 = sb * BAND
        stats = jnp.zeros((BAND, LANES), F32)
        for pair in range(N_HEADS // 2):
            ls = _lane_block(pair)
            q2 = q_ref[r0:r0 + BAND, ls]
            k2 = kk_ref[r0:r0 + 2 * BAND, ls]
            v2 = vv_ref[r0:r0 + 2 * BAND, ls]
            outs = []
            for hh in range(2):
                s = _dot_t(q2 * head_keep[hh], k2) + bias_ref[2 * pair + hh]
                if sb == 0:
                    s = jnp.where(no_history, NEG_INF, s)
                m = jnp.max(s, axis=-1, keepdims=True)
                p = jnp.exp(s - m)
                l = jnp.sum(p, axis=-1, keepdims=True)
                a = jnp.dot(p.astype(BF16), v2, preferred_element_type=F32)
                outs.append(a * (1.0 / l))
                stats = jnp.where(lane == 2 * pair + hh, m + jnp.log(l), stats)
            o_ref[r0:r0 + BAND, ls] = jnp.where(left, outs[0], outs[1])
        lse_ref[r0:r0 + BAND, :] = stats


def _attn_group(q, kvb, bias, grp):
    n, d, rows, _ = q.shape
    qb = ATT_QB
    per_blk = qb // BAND

    def prev_idx(i):
        return jnp.maximum(i * per_blk - 1, 0)

    def cur(col):
        return pl.BlockSpec((None, None, qb, HEADS_W), lambda b, r, i: (b, r, i, col))

    def prev(col):
        return pl.BlockSpec((None, None, BAND, HEADS_W), lambda b, r, i: (b, r, prev_idx(i), col))

    return pl.pallas_call(
        _attn_group_kernel,
        out_shape=(jax.ShapeDtypeStruct((n, d, rows, HEADS_W), F32),
                   jax.ShapeDtypeStruct((n, d, rows, LANES), F32)),
        grid=(n, d, rows // qb),
        in_specs=[cur(0), cur(0), prev(0), cur(1), prev(1), _const_spec((N_HEADS, BAND, 2 * BAND))],
        out_specs=(cur(0), pl.BlockSpec((None, None, qb, LANES), lambda b, r, i: (b, r, i, 0))),
        scratch_shapes=[pltpu.VMEM((BAND + qb, HEADS_W), BF16),
                        pltpu.VMEM((BAND + qb, HEADS_W), BF16)],
        compiler_params=_params(("arbitrary", "arbitrary", "arbitrary"), 32),
        name=f"attn_group{grp}",
    )(q, kvb, kvb, kvb, kvb, bias)


def _compact_cache_kernel(c_ref, eye_ref, sel_ref, recent_ref, far_ref, *, t_new):
    p = c_ref.shape[-1]
    d1, d2 = WINDOWS[1][1], WINDOWS[2][1]
    x = c_ref[...].reshape(KV_W, p).astype(BF16)
    lo = p - d1 * BAND
    for j in range(d1 * BAND // MXU_DIM):
        cols = x[:, lo + j * MXU_DIM:lo + (j + 1) * MXU_DIM]
        recent_ref[j * MXU_DIM:(j + 1) * MXU_DIM, :] = _dot_t(eye_ref[...], cols).astype(BF16)
    per = MXU_DIM // d2
    for c in range(p // MXU_DIM):
        picked = _dot_t(sel_ref[...], x[:, c * MXU_DIM:(c + 1) * MXU_DIM]).astype(BF16)
        for t in range(t_new):
            far_ref[t * BAND + c * per:t * BAND + (c + 1) * per, :] = picked[t * per:(t + 1) * per, :]


def _compact_cache(cache_kv, t_new):
    n, p = cache_kv.shape[:2]
    d1, d2 = WINDOWS[1][1], WINDOWS[2][1]
    per = MXU_DIM // d2
    assert p == WINDOWS[2][0] and p // d2 == BAND and t_new <= d1 and p % MXU_DIM == 0
    feat_major = jnp.transpose(cache_kv, (0, 2, 3, 4, 1))
    eye = jnp.eye(MXU_DIM, dtype=BF16)
    row = jnp.arange(t_new * per)[:, None]
    col = jnp.arange(MXU_DIM)[None, :]
    sel = jnp.where(col == (row % per) * d2 + row // per, 1.0, 0.0).astype(BF16)
    out_spec = lambda rows: pl.BlockSpec((None, rows, KV_W), lambda i: (i, 0, 0))
    return pl.pallas_call(
        functools.partial(_compact_cache_kernel, t_new=t_new),
        out_shape=(jax.ShapeDtypeStruct((n, d1 * BAND, KV_W), BF16),
                   jax.ShapeDtypeStruct((n, t_new * BAND, KV_W), BF16)),
        grid=(n,),
        in_specs=[pl.BlockSpec((None, 2, N_HEADS, HEAD_DIM, p), lambda i: (i, 0, 0, 0, 0)),
                  _const_spec((MXU_DIM, MXU_DIM)), _const_spec((t_new * per, MXU_DIM))],
        out_specs=(out_spec(d1 * BAND), out_spec(t_new * BAND)),
        compiler_params=_params(("arbitrary",), 48),
        name="compact_cache",
    )(feat_major, eye, sel)


def _attn_sample_kernel(q_ref, kvn_ref, recent_ref, far_ref, b0_ref, b1_ref, b2_ref, b2n_ref, o_ref):
    t_new = q_ref.shape[0]
    base = (pl.program_id(0) % (SAMPLE_ROWS // SAMPLE_SEQ_STEP)) * SAMPLE_SEQ_STEP
    n_rows = t_new * N_HEADS
    hsel = (lax.broadcasted_iota(jnp.int32, (N_HEADS, HEADS_W), 1) // HEAD_DIM
            == lax.broadcasted_iota(jnp.int32, (N_HEADS, HEADS_W), 0))
    row8 = lax.broadcasted_iota(jnp.int32, (SUBLANES, KV_W), 0)

    def softmax_pv(pieces):
        m = functools.reduce(jnp.maximum, [jnp.max(s, axis=-1, keepdims=True) for s, _ in pieces])
        l = jnp.zeros((n_rows, 1), F32)
        acc = jnp.zeros((n_rows, HEADS_W), F32)
        for s, v in pieces:
            p = jnp.exp(s - m)
            l = l + jnp.sum(p, axis=-1, keepdims=True)
            acc = acc + jnp.dot(p.astype(BF16), v, preferred_element_type=F32)
        return acc, m, l

    for j in range(SAMPLE_SEQ_STEP):
        b = base + j

        def q_rows(grp):
            rows = [jnp.where(hsel, q_ref[t, pl.ds(b, 1), _lane_block(grp, HEADS_W)], 0.0)
                    for t in range(t_new)]
            return jnp.concatenate(rows, axis=0).astype(BF16)

        new8 = jnp.zeros((SUBLANES, KV_W), F32)
        for t in range(t_new):
            new8 = jnp.where(row8 == t, kvn_ref[t, pl.ds(b, 1), :], new8)
        new = jnp.concatenate([new8, jnp.zeros((BAND - SUBLANES, KV_W), F32)], axis=0).astype(BF16)
        recent = recent_ref[j]
        ka = jnp.concatenate([recent[:, :HEADS_W], new[:, :HEADS_W]], axis=0)
        va = jnp.concatenate([recent[:, HEADS_W:], new[:, HEADS_W:]], axis=0)
        kb, vb = far_ref[j, :, :HEADS_W], far_ref[j, :, HEADS_W:]
        n_a = ka.shape[0]
        q0, q1, q2 = q_rows(0), q_rows(1), q_rows(2)
        lo0 = n_a - 2 * BAND
        parts = [
            softmax_pv([(_dot_t(q0, ka[lo0:]) + b0_ref[...], va[lo0:])]),
            softmax_pv([(_dot_t(q1, ka) + b1_ref[...], va)]),
            softmax_pv([(_dot_t(q2, kb) + b2_ref[...], vb),
                        (_dot_t(q2, ka[n_a - BAND:]) + b2n_ref[...], va[n_a - BAND:])]),
        ]
        m_all = functools.reduce(jnp.maximum, [m for _, m, _ in parts])
        num = jnp.zeros((n_rows, HEADS_W), F32)
        den = jnp.zeros((n_rows, 1), F32)
        for acc, m, l in parts:
            e = jnp.exp(m - m_all)
            num = num + e * acc
            den = den + e * l
        out = num * (1.0 / den)
        for t in range(t_new):
            own = jnp.where(hsel, out[t * N_HEADS:(t + 1) * N_HEADS, :], 0.0)
            o_ref[t, pl.ds(b, 1), :] = jnp.sum(own, axis=0, keepdims=True)


def _attn_sample(q, kvn, recent, far, biases):
    t_new, n, qw = q.shape
    step = SAMPLE_SEQ_STEP
    blk = lambda width: pl.BlockSpec((t_new, SAMPLE_ROWS, width),
                                     lambda i: (0, i // (SAMPLE_ROWS // step), 0))
    rows = lambda a: pl.BlockSpec((step,) + a.shape[1:], lambda i: (i, 0, 0))
    n_rows = t_new * N_HEADS
    return pl.pallas_call(
        _attn_sample_kernel,
        out_shape=jax.ShapeDtypeStruct((t_new, n, HEADS_W), F32),
        grid=(n // step,),
        in_specs=[blk(qw), blk(KV_W), rows(recent), rows(far)]
        + [_const_spec((n_rows, b.shape[1])) for b in biases],
        out_specs=blk(HEADS_W),
        compiler_params=_params(("arbitrary",), 40),
        name="attn_sample",
    )(q, kvn, recent, far, *biases)


def _attn_out_sample_kernel(x_ref, o_ref, wo_ref, g_ref, wup_ref, wdn_ref, out_ref):
    x = x_ref[...] + jnp.dot(o_ref[...].astype(BF16), wo_ref[...], preferred_element_type=F32)
    out_ref[...] = _mlp_residual(x, g_ref, wup_ref, wdn_ref)


def _attn_out_prompt_kernel(x_ref, o0_ref, l0_ref, o1_ref, l1_ref, o2_ref, l2_ref, ex_ref, wo_ref,
                            g_ref, wup_ref, wdn_ref, out_ref, slab_ref):
    rows = x_ref.shape[0]
    n_o = HEADS_W // LANES
    for gi, (o_ref, l_ref) in enumerate(((o1_ref, l1_ref), (o2_ref, l2_ref))):
        d = WINDOWS[gi + 1][1]
        for r in range(d):
            dst = pl.ds(r, rows // d, stride=d)
            for s in range(n_o):
                slab_ref[gi, s, dst, :] = o_ref[r, :, _lane_block(s)]
            slab_ref[gi, n_o, dst, :] = l_ref[r]
    outs = [o0_ref[0]] + [jnp.concatenate([slab_ref[gi, s] for s in range(n_o)], axis=1) for gi in range(2)]
    lses = [l0_ref[0], slab_ref[0, n_o], slab_ref[1, n_o]]
    mx = functools.reduce(jnp.maximum, lses)
    es = [jnp.exp(l - mx) for l in lses]
    inv = 1.0 / functools.reduce(jnp.add, es)
    o = jnp.zeros((rows, HEADS_W), F32)
    for e, og in zip(es, outs):
        w = e * inv
        hi = w.astype(BF16)
        lo = (w - hi.astype(F32)).astype(BF16)
        wide = (jnp.dot(hi, ex_ref[...], preferred_element_type=F32)
                + jnp.dot(lo, ex_ref[...], preferred_element_type=F32))
        o = o + wide * og
    x = x_ref[...] + jnp.dot(o.astype(BF16), wo_ref[...], preferred_element_type=F32)
    out_ref[...] = _mlp_residual(x, g_ref, wup_ref, wdn_ref)


def _mlp_weight_specs():
    return [_const_spec((HEADS_W, D_MODEL)), _const_spec((1, D_MODEL)),
            _const_spec((D_MODEL, D_FF)), _const_spec((D_FF, D_MODEL))]


def _attn_out_sample(x, o, wo, g, wup, wdn):
    m = x.shape[0]
    row = pl.BlockSpec((ROW_TILE, D_MODEL), lambda i: (i, 0))
    return pl.pallas_call(
        _attn_out_sample_kernel,
        out_shape=jax.ShapeDtypeStruct((m, D_MODEL), F32),
        grid=(m // ROW_TILE,),
        in_specs=[row, pl.BlockSpec((ROW_TILE, HEADS_W), lambda i: (i, 0))] + _mlp_weight_specs(),
        out_specs=row,
        compiler_params=_params(("arbitrary",), 48),
        name="attn_out_sample",
    )(x, o, wo, g, wup, wdn)


def _attn_out_prompt(x, parts, ex, wo, g, wup, wdn):
    n, t, _ = x.shape
    tile = ATT_OUT_TILE
    row = pl.BlockSpec((None, tile, D_MODEL), lambda b, i: (b, i, 0))
    part_specs = []
    for _, d in WINDOWS:
        for width in (HEADS_W, LANES):
            part_specs.append(pl.BlockSpec((None, d, tile // d, width), lambda b, i: (b, 0, i, 0)))
    flat = [a for pair in parts for a in pair]
    return pl.pallas_call(
        _attn_out_prompt_kernel,
        out_shape=jax.ShapeDtypeStruct((n, t, D_MODEL), F32),
        grid=(n, t // tile),
        in_specs=[row] + part_specs + [_const_spec((LANES, HEADS_W))] + _mlp_weight_specs(),
        out_specs=row,
        scratch_shapes=[pltpu.VMEM((2, HEADS_W // LANES + 1, tile, LANES), F32)],
        compiler_params=_params(("arbitrary", "arbitrary"), 48),
        name="attn_out_prompt",
    )(x, *flat, ex, wo, g, wup, wdn)


def _alibi_slopes():
    n = N_GROUPS * N_HEADS
    s = 2.0 ** (-8.0 * jnp.arange(1, n + 1, dtype=F32) / n)
    return s.reshape(N_GROUPS, N_HEADS)


def _prompt_bias(slopes, grp):
    d = WINDOWS[grp][1]
    iq = jnp.arange(BAND, dtype=jnp.int32)[:, None]
    ik = jnp.arange(2 * BAND, dtype=jnp.int32)[None, :]
    steps = iq + BAND - ik
    valid = (steps >= 0) & (steps <= BAND)
    offs = (steps * d).astype(F32)
    return jnp.where(valid[None], -slopes[grp][:, None, None] * offs[None], NEG_INF)


def _sample_bias(slopes, t_new, past):
    d1, d2 = WINDOWS[1][1], WINDOWS[2][1]
    t = jnp.arange(t_new, dtype=jnp.int32)[:, None]

    def table(slope, dist, valid):
        b = jnp.where(valid[:, None, :], -slope[None, :, None] * dist.astype(F32)[:, None, :], NEG_INF)
        return b.reshape(t_new * N_HEADS, -1)

    u = jnp.arange(BAND, dtype=jnp.int32)[None, :]
    dist_n = t - u
    ok_n = (u < t_new) & (dist_n >= 0)
    c = jnp.arange(BAND, dtype=jnp.int32)[None, :]
    dist_c = BAND + t - c
    b0 = table(slopes[0], jnp.concatenate([dist_c, dist_n], axis=1),
               jnp.concatenate([dist_c <= WINDOWS[0][0], ok_n], axis=1))
    c = jnp.arange(d1 * BAND, dtype=jnp.int32)[None, :]
    dist_c = d1 * BAND + t - c
    ok_c = (dist_c % d1 == 0) & (dist_c <= WINDOWS[1][0])
    b1 = table(slopes[1], jnp.concatenate([dist_c, dist_n], axis=1),
               jnp.concatenate([ok_c, ok_n & (dist_n % d1 == 0)], axis=1))
    per = past // d2
    col = jnp.arange(t_new * per, dtype=jnp.int32)[None, :]
    pos = (col % per) * d2 + col // per
    dist_c = past + t - pos
    ok_c = (dist_c % d2 == 0) & (dist_c <= WINDOWS[2][0])
    b2 = table(slopes[2], dist_c, ok_c)
    b2n = table(slopes[2], dist_n, ok_n & (dist_n % d2 == 0))
    return b0, b1, b2, b2n


def _head_mean_matrix():
    idx = jnp.arange(HEADS_W) // HEAD_DIM
    return jnp.where(idx[:, None] == idx[None, :], 1.0 / HEAD_DIM, 0.0).astype(BF16)


def _head_expand_matrix():
    lane = jnp.arange(LANES)[:, None]
    col = jnp.arange(HEADS_W)[None, :] // HEAD_DIM
    return jnp.where(lane == col, 1.0, 0.0).astype(BF16)


def kernel(x_prompt, x_sample, state_conv, cache_kv, norm_mix_g, norm_mlp_g, conv_w_pw1, conv_b_pw1,
           conv_w_dw, conv_b_dw, conv_ln_g, conv_ln_b, conv_w_pw2, conv_b_pw2, kv_norm_g, w_kv,
           k_norm_g, attn_w_q, q_norm_g, attn_w_o, mlp_w_up, mlp_w_down):
    n_p, t_p, _ = x_prompt.shape
    n_s, t_s, _ = x_sample.shape
    past = cache_kv.shape[1]
    scale = HEAD_DIM ** -0.5

    w1 = conv_w_pw1.astype(BF16)
    w2 = conv_w_pw2.astype(BF16)
    wkv = w_kv.astype(BF16)
    wq = attn_w_q.astype(BF16)
    wo = attn_w_o.astype(BF16)
    wup = mlp_w_up.astype(BF16)
    wdn = mlp_w_down.astype(BF16)
    row = lambda a: a.reshape(1, -1)
    e_mat = _head_mean_matrix()
    ex_mat = _head_expand_matrix()
    gk = row(jnp.tile(k_norm_g, N_HEADS))
    gq = [row(jnp.tile(q_norm_g[j], N_HEADS) * scale) for j in range(N_B_LAYERS)]
    slopes = _alibi_slopes()
    bias_p = [_prompt_bias(slopes, g) for g in range(N_GROUPS)]
    bias_s = _sample_bias(slopes, t_s, past)

    def conv_args(layer):
        return (row(norm_mix_g[layer]), w1[layer], row(conv_b_pw1[layer]), conv_w_dw[layer],
                row(conv_b_dw[layer]), row(conv_ln_g[layer]), row(conv_ln_b[layer]),
                w2[layer], row(conv_b_pw2[layer]))

    xp = x_prompt
    conv_p = []
    zero_hist = jnp.zeros((n_p, HIST_PAD, CONV_CH), F32)
    for layer in range(N_A_LAYERS):
        xp, st = _conv_prompt(xp, zero_hist, *conv_args(layer))
        conv_p.append(st[:, HIST_PAD - CONV_HIST:])
        xp = _mlp(xp.reshape(n_p * t_p, D_MODEL), row(norm_mlp_g[layer]), wup[layer],
                  wdn[layer]).reshape(n_p, t_p, D_MODEL)
    kv_p, *kvb_p = _kv_proj_prompt(xp, row(kv_norm_g), wkv, e_mat, gk)
    for j in range(N_B_LAYERS):
        layer = N_A_LAYERS + j
        qs = _q_proj_prompt(xp, row(norm_mix_g[layer]), wq[j], e_mat, gq[j])
        parts = [_attn_group(qs[g], kvb_p[g], bias_p[g], g) for g in range(N_GROUPS)]
        xp = _attn_out_prompt(xp, parts, ex_mat, wo[j], row(norm_mlp_g[layer]), wup[layer], wdn[layer])
    y_prompt = xp
    conv_prompt = jnp.stack(conv_p, axis=0)
    win = min(WINDOWS[-1][0], t_p)
    kv_prompt = kv_p.reshape(n_p, t_p, 2, N_HEADS, HEAD_DIM)[:, t_p - win:]

    xs = jnp.transpose(x_sample, (1, 0, 2))
    st_s = jnp.transpose(state_conv, (0, 2, 1, 3))
    conv_s = []
    for layer in range(N_A_LAYERS):
        xs, st = _conv_sample(xs, st_s[layer], *conv_args(layer))
        conv_s.append(jnp.transpose(st, (1, 0, 2)))
        xs = _mlp(xs.reshape(t_s * n_s, D_MODEL), row(norm_mlp_g[layer]), wup[layer],
                  wdn[layer]).reshape(t_s, n_s, D_MODEL)
    xs = xs.reshape(t_s * n_s, D_MODEL)
    kvn = _kv_proj_sample(xs, row(kv_norm_g), wkv, e_mat, gk).reshape(t_s, n_s, KV_W)
    recent, far = _compact_cache(cache_kv, t_s)
    for j in range(N_B_LAYERS):
        layer = N_A_LAYERS + j
        q = _q_proj_sample(xs, row(norm_mix_g[layer]), wq[j], e_mat, gq[j])
        o = _attn_sample(q.reshape(t_s, n_s, N_GROUPS * HEADS_W), kvn, recent, far, bias_s)
        xs = _attn_out_sample(xs, o.reshape(t_s * n_s, HEADS_W), wo[j], row(norm_mlp_g[layer]),
                              wup[layer], wdn[layer])
    y_sample = jnp.transpose(xs.reshape(t_s, n_s, D_MODEL), (1, 0, 2))
    conv_sample = jnp.stack(conv_s, axis=0)
    kv_sample = jnp.transpose(kvn, (1, 0, 2)).reshape(n_s, t_s, 2, N_HEADS, HEAD_DIM)

    return (y_prompt, y_sample, conv_prompt, conv_sample, kv_prompt, kv_sample)
```

```python
import functools

import jax
import jax.numpy as jnp
from jax import lax
from jax.experimental import pallas as pl
from jax.experimental.pallas import tpu as pltpu

F32 = jnp.float32
BF16 = jnp.bfloat16

D_MODEL = 1024
CONV_CH = 1024
CONV_WIDTH = 31
CONV_HIST = CONV_WIDTH - 1
HIST_PAD = 32
D_FF = 4096
N_HEADS = 8
HEAD_DIM = 64
HEADS_W = N_HEADS * HEAD_DIM
KV_W = 2 * HEADS_W
WINDOWS = ((128, 1), (512, 4), (2048, 16))
N_GROUPS = len(WINDOWS)
BAND = 128
N_A_LAYERS = 2
N_B_LAYERS = 2
NORM_EPS = 1e-6
NEG_INF = float("-inf")
LANES = 128
SUBLANES = 8

ROW_TILE = 512
ATT_OUT_TILE = 512
FF_CHUNK = 1024
ATT_QB = 512
SAMPLE_CONV_NB = 32
SAMPLE_ROWS = 8
SAMPLE_SEQ_STEP = 4
MXU_DIM = 256
MIB = 1024 * 1024

assert all(w // d == BAND for w, d in WINDOWS)


def _const_spec(shape):
    nd = len(shape)
    return pl.BlockSpec(shape, lambda *_: (0,) * nd, pipeline_mode=pl.Buffered(1))


def _params(semantics, vmem_mib):
    return pltpu.CompilerParams(dimension_semantics=semantics, vmem_limit_bytes=vmem_mib * MIB)


def _lane_block(s, width=LANES):
    return slice(s * width, (s + 1) * width)


def _rms(x, g):
    return x * lax.rsqrt(jnp.mean(x * x, axis=-1, keepdims=True) + NORM_EPS) * g


def _mlp_residual(x, g_ref, wup_ref, wdn_ref):
    h = _rms(x, g_ref[...]).astype(BF16)
    acc = x
    for c in range(D_FF // FF_CHUNK):
        sl = _lane_block(c, FF_CHUNK)
        z = jnp.maximum(jnp.dot(h, wup_ref[:, sl], preferred_element_type=F32), 0.0)
        acc = acc + jnp.dot((z * z).astype(BF16), wdn_ref[sl, :], preferred_element_type=F32)
    return acc


def _head_mean_sq(v, e_ref):
    return jnp.dot((v * v).astype(BF16), e_ref[...], preferred_element_type=F32)


def _dot_t(a, b):
    return lax.dot_general(a, b, (((1,), (1,)), ((), ())), preferred_element_type=F32)


def _deinterleave_store(val, slab_ref, out_ref, d):
    rows, width = val.shape
    for s in range(width // LANES):
        slab_ref[s] = val[:, _lane_block(s)]
    for s in range(width // LANES):
        for r in range(d):
            out_ref[r, :, _lane_block(s)] = slab_ref[s, pl.ds(r, rows // d, stride=d), :].astype(BF16)


def _mlp_kernel(x_ref, g_ref, wup_ref, wdn_ref, o_ref):
    o_ref[...] = _mlp_residual(x_ref[...], g_ref, wup_ref, wdn_ref)


def _mlp(x, g, wup, wdn):
    m = x.shape[0]
    row = pl.BlockSpec((ROW_TILE, D_MODEL), lambda i: (i, 0))
    return pl.pallas_call(
        _mlp_kernel,
        out_shape=jax.ShapeDtypeStruct((m, D_MODEL), F32),
        grid=(m // ROW_TILE,),
        in_specs=[row, _const_spec((1, D_MODEL)), _const_spec((D_MODEL, D_FF)),
                  _const_spec((D_FF, D_MODEL))],
        out_specs=row,
        compiler_params=_params(("arbitrary",), 48),
        name="mlp",
    )(x, g, wup, wdn)


def _glu(h, w1_ref, b1_ref):
    u = jnp.dot(h, w1_ref[...], preferred_element_type=F32) + b1_ref[...]
    return u[:, :CONV_CH] * jax.nn.sigmoid(u[:, CONV_CH:])


def _conv_tail(c, x, bdw_ref, lng_ref, lnb_ref, w2_ref, b2_ref):
    c = c + bdw_ref[...]
    mu = jnp.mean(c, axis=-1, keepdims=True)
    xc = c - mu
    y = xc * lax.rsqrt(jnp.mean(xc * xc, axis=-1, keepdims=True) + NORM_EPS)
    y = y * lng_ref[...] + lnb_ref[...]
    y = y * jax.nn.sigmoid(y)
    out = jnp.dot(y.astype(BF16), w2_ref[...], preferred_element_type=F32) + b2_ref[...]
    return x + out


def _conv_prompt_kernel(x_ref, st_ref, g_ref, w1_ref, b1_ref, wdw_ref, bdw_ref, lng_ref, lnb_ref,
                        w2_ref, b2_ref, o_ref, sto_ref, buf_ref):
    n_slabs = CONV_CH // LANES

    @pl.when(pl.program_id(1) == 0)
    def _():
        for s in range(n_slabs):
            buf_ref[s, 0:HIST_PAD, :] = st_ref[:, _lane_block(s)]

    x = x_ref[...]
    h = _rms(x, g_ref[...]).astype(BF16)
    u = _glu(h, w1_ref, b1_ref)
    for s in range(n_slabs):
        buf_ref[s, HIST_PAD:HIST_PAD + ROW_TILE, :] = u[:, _lane_block(s)]
    off = HIST_PAD - CONV_HIST
    cs = []
    for s in range(n_slabs):
        ls = _lane_block(s)
        c = buf_ref[s, off:off + ROW_TILE, :] * wdw_ref[0:1, ls]
        for k in range(1, CONV_WIDTH):
            c = c + buf_ref[s, off + k:off + k + ROW_TILE, :] * wdw_ref[k:k + 1, ls]
        cs.append(c)
    c = jnp.concatenate(cs, axis=1)
    o_ref[...] = _conv_tail(c, x, bdw_ref, lng_ref, lnb_ref, w2_ref, b2_ref)
    for s in range(n_slabs):
        last = buf_ref[s, ROW_TILE:ROW_TILE + HIST_PAD, :]
        sto_ref[:, _lane_block(s)] = last
        buf_ref[s, 0:HIST_PAD, :] = last


def _conv_prompt(x, st, g, w1, b1, wdw, bdw, lng, lnb, w2, b2):
    n, t, _ = x.shape
    row = pl.BlockSpec((None, ROW_TILE, D_MODEL), lambda b, i: (b, i, 0))
    st_spec = pl.BlockSpec((None, HIST_PAD, CONV_CH), lambda b, i: (b, 0, 0))
    return pl.pallas_call(
        _conv_prompt_kernel,
        out_shape=(jax.ShapeDtypeStruct((n, t, D_MODEL), F32),
                   jax.ShapeDtypeStruct((n, HIST_PAD, CONV_CH), F32)),
        grid=(n, t // ROW_TILE),
        in_specs=[row, st_spec, _const_spec((1, D_MODEL)),
                  _const_spec((D_MODEL, 2 * CONV_CH)), _const_spec((1, 2 * CONV_CH)),
                  _const_spec((CONV_WIDTH, CONV_CH)), _const_spec((1, CONV_CH)),
                  _const_spec((1, CONV_CH)), _const_spec((1, CONV_CH)),
                  _const_spec((CONV_CH, D_MODEL)), _const_spec((1, D_MODEL))],
        out_specs=(row, st_spec),
        scratch_shapes=[pltpu.VMEM((CONV_CH // LANES, HIST_PAD + ROW_TILE, LANES), F32)],
        compiler_params=_params(("arbitrary", "arbitrary"), 48),
        name="conv_prompt",
    )(x, st, g, w1, b1, wdw, bdw, lng, lnb, w2, b2)


def _conv_sample_kernel(x_ref, st_ref, g_ref, w1_ref, b1_ref, wdw_ref, bdw_ref, lng_ref, lnb_ref,
                        w2_ref, b2_ref, o_ref, sto_ref):
    t_new, nb, _ = x_ref.shape
    x = x_ref[...].reshape(t_new * nb, D_MODEL)
    h = _rms(x, g_ref[...]).astype(BF16)
    glu = _glu(h, w1_ref, b1_ref)
    new = [glu[t * nb:(t + 1) * nb, :] for t in range(t_new)]

    def full(j):
        return st_ref[j] if j < CONV_HIST else new[j - CONV_HIST]

    outs = []
    for t in range(t_new):
        c = full(t) * wdw_ref[0:1, :]
        for k in range(1, CONV_WIDTH):
            c = c + full(t + k) * wdw_ref[k:k + 1, :]
        outs.append(c)
    c = jnp.concatenate(outs, axis=0)
    o_ref[...] = _conv_tail(c, x, bdw_ref, lng_ref, lnb_ref, w2_ref, b2_ref).reshape(t_new, nb, D_MODEL)
    for j in range(CONV_HIST):
        sto_ref[j] = full(j + t_new)


def _conv_sample(x, st, g, w1, b1, wdw, bdw, lng, lnb, w2, b2):
    t_new, n, _ = x.shape
    nb = SAMPLE_CONV_NB
    x_spec = pl.BlockSpec((t_new, nb, D_MODEL), lambda i: (0, i, 0))
    st_spec = pl.BlockSpec((CONV_HIST, nb, CONV_CH), lambda i: (0, i, 0))
    return pl.pallas_call(
        _conv_sample_kernel,
        out_shape=(jax.ShapeDtypeStruct((t_new, n, D_MODEL), F32),
                   jax.ShapeDtypeStruct((CONV_HIST, n, CONV_CH), F32)),
        grid=(n // nb,),
        in_specs=[x_spec, st_spec, _const_spec((1, D_MODEL)),
                  _const_spec((D_MODEL, 2 * CONV_CH)), _const_spec((1, 2 * CONV_CH)),
                  _const_spec((CONV_WIDTH, CONV_CH)), _const_spec((1, CONV_CH)),
                  _const_spec((1, CONV_CH)), _const_spec((1, CONV_CH)),
                  _const_spec((CONV_CH, D_MODEL)), _const_spec((1, D_MODEL))],
        out_specs=(x_spec, st_spec),
        compiler_params=_params(("arbitrary",), 48),
        name="conv_sample",
    )(x, st, g, w1, b1, wdw, bdw, lng, lnb, w2, b2)


def _kv_rows(x_ref, g_ref, wkv_ref, e_ref, gk_ref):
    h = _rms(x_ref[...], g_ref[...]).astype(BF16)
    k = jnp.dot(h, wkv_ref[:, :HEADS_W], preferred_element_type=F32)
    v = jnp.dot(h, wkv_ref[:, HEADS_W:], preferred_element_type=F32)
    k = k * lax.rsqrt(_head_mean_sq(k, e_ref) + NORM_EPS) * gk_ref[...]
    return jnp.concatenate([k, v], axis=1)


def _kv_sample_kernel(x_ref, g_ref, wkv_ref, e_ref, gk_ref, kv_ref):
    kv_ref[...] = _kv_rows(x_ref, g_ref, wkv_ref, e_ref, gk_ref)


def _kv_prompt_kernel(x_ref, g_ref, wkv_ref, e_ref, gk_ref, kv_ref, kb0_ref, kb1_ref, kb2_ref, slab_ref):
    kv = _kv_rows(x_ref, g_ref, wkv_ref, e_ref, gk_ref)
    kv_ref[...] = kv
    kb0_ref[0] = kv.astype(BF16)
    _deinterleave_store(kv, slab_ref, kb1_ref, WINDOWS[1][1])
    _deinterleave_store(kv, slab_ref, kb2_ref, WINDOWS[2][1])


def _kv_weight_specs():
    return [_const_spec((1, D_MODEL)), _const_spec((D_MODEL, KV_W)),
            _const_spec((HEADS_W, HEADS_W)), _const_spec((1, HEADS_W))]


def _kv_proj_sample(x, g, wkv, e, gk):
    m = x.shape[0]
    return pl.pallas_call(
        _kv_sample_kernel,
        out_shape=jax.ShapeDtypeStruct((m, KV_W), F32),
        grid=(m // ROW_TILE,),
        in_specs=[pl.BlockSpec((ROW_TILE, D_MODEL), lambda i: (i, 0))] + _kv_weight_specs(),
        out_specs=pl.BlockSpec((ROW_TILE, KV_W), lambda i: (i, 0)),
        compiler_params=_params(("arbitrary",), 32),
        name="kv_proj_sample",
    )(x, g, wkv, e, gk)


def _dilated_shapes(n, t, width, dtype):
    sds, specs = [], []
    for _, d in WINDOWS:
        sds.append(jax.ShapeDtypeStruct((n, d, t // d, width), dtype))
        specs.append(pl.BlockSpec((None, d, ROW_TILE // d, width), lambda b, i: (b, 0, i, 0)))
    return sds, specs


def _kv_proj_prompt(x, g, wkv, e, gk, win):
    n, t, _ = x.shape
    sds, specs = _dilated_shapes(n, t, KV_W, BF16)
    skip = (t - win) // ROW_TILE
    assert win % ROW_TILE == 0 and t % ROW_TILE == 0
    f32_spec = pl.BlockSpec((None, ROW_TILE, KV_W), lambda b, i: (b, jnp.maximum(i - skip, 0), 0))
    return pl.pallas_call(
        _kv_prompt_kernel,
        out_shape=[jax.ShapeDtypeStruct((n, win, KV_W), F32)] + sds,
        grid=(n, t // ROW_TILE),
        in_specs=[pl.BlockSpec((None, ROW_TILE, D_MODEL), lambda b, i: (b, i, 0))] + _kv_weight_specs(),
        out_specs=[f32_spec] + specs,
        scratch_shapes=[pltpu.VMEM((KV_W // LANES, ROW_TILE, LANES), F32)],
        compiler_params=_params(("arbitrary", "arbitrary"), 40),
        name="kv_proj_prompt",
    )(x, g, wkv, e, gk)


def _q_group(h, wq_ref, e_ref, gq_ref, grp):
    q = jnp.dot(h, wq_ref[:, _lane_block(grp, HEADS_W)], preferred_element_type=F32)
    return q * lax.rsqrt(_head_mean_sq(q, e_ref) + NORM_EPS) * gq_ref[...]


def _q_sample_kernel(x_ref, g_ref, wq_ref, e_ref, gq_ref, q_ref):
    h = _rms(x_ref[...], g_ref[...]).astype(BF16)
    for grp in range(N_GROUPS):
        q = _q_group(h, wq_ref, e_ref, gq_ref, grp)
        q_ref[:, _lane_block(grp, HEADS_W)] = q.astype(BF16).astype(F32)


def _q_prompt_kernel(x_ref, g_ref, wq_ref, e_ref, gq_ref, q0_ref, q1_ref, q2_ref, slab_ref):
    h = _rms(x_ref[...], g_ref[...]).astype(BF16)
    q0_ref[0] = _q_group(h, wq_ref, e_ref, gq_ref, 0).astype(BF16)
    _deinterleave_store(_q_group(h, wq_ref, e_ref, gq_ref, 1), slab_ref, q1_ref, WINDOWS[1][1])
    _deinterleave_store(_q_group(h, wq_ref, e_ref, gq_ref, 2), slab_ref, q2_ref, WINDOWS[2][1])


def _q_weight_specs():
    return [_const_spec((1, D_MODEL)), _const_spec((D_MODEL, N_GROUPS * HEADS_W)),
            _const_spec((HEADS_W, HEADS_W)), _const_spec((1, HEADS_W))]


def _q_proj_sample(x, g, wq, e, gq):
    m = x.shape[0]
    qw = N_GROUPS * HEADS_W
    return pl.pallas_call(
        _q_sample_kernel,
        out_shape=jax.ShapeDtypeStruct((m, qw), F32),
        grid=(m // ROW_TILE,),
        in_specs=[pl.BlockSpec((ROW_TILE, D_MODEL), lambda i: (i, 0))] + _q_weight_specs(),
        out_specs=pl.BlockSpec((ROW_TILE, qw), lambda i: (i, 0)),
        compiler_params=_params(("arbitrary",), 32),
        name="q_proj_sample",
    )(x, g, wq, e, gq)


def _q_proj_prompt(x, g, wq, e, gq):
    n, t, _ = x.shape
    sds, specs = _dilated_shapes(n, t, HEADS_W, BF16)
    return pl.pallas_call(
        _q_prompt_kernel,
        out_shape=sds,
        grid=(n, t // ROW_TILE),
        in_specs=[pl.BlockSpec((None, ROW_TILE, D_MODEL), lambda b, i: (b, i, 0))] + _q_weight_specs(),
        out_specs=specs,
        scratch_shapes=[pltpu.VMEM((HEADS_W // LANES, ROW_TILE, LANES), F32)],
        compiler_params=_params(("arbitrary", "arbitrary"), 32),
        name="q_proj_prompt",
    )(x, g, wq, e, gq)


def _attn_group_kernel(q_ref, kc_ref, kp_ref, vc_ref, vp_ref, bias_ref, o_ref, lse_ref, kk_ref, vv_ref):
    kk_ref[0:BAND, :] = kp_ref[...]
    kk_ref[BAND:, :] = kc_ref[...]
    vv_ref[0:BAND, :] = vp_ref[...]
    vv_ref[BAND:, :] = vc_ref[...]
    first_block = pl.program_id(2) == 0
    key_col = lax.broadcasted_iota(jnp.int32, (2 * BAND, 2 * BAND), 1)
    lane = lax.broadcasted_iota(jnp.int32, (BAND, LANES), 1)
    left = lane < HEAD_DIM
    lane_row = lax.broadcasted_iota(jnp.int32, (1, LANES), 1)
    head_keep = [jnp.where(lane_row < HEAD_DIM, 1.0, 0.0).astype(BF16),
                 jnp.where(lane_row < HEAD_DIM, 0.0, 1.0).astype(BF16)]
    no_history = jnp.logical_and(first_block, key_col < BAND)

    for sb in range(q_ref.shape[0] // BAND):
        r0 = sb * BAND
        stats = jnp.zeros((BAND, LANES), F32)
        for pair in range(N_HEADS // 2):
            ls = _lane_block(pair)
            q2 = q_ref[r0:r0 + BAND, ls]
            k2 = kk_ref[r0:r0 + 2 * BAND, ls]
            v2 = vv_ref[r0:r0 + 2 * BAND, ls]
            qq = jnp.concatenate([q2 * head_keep[0], q2 * head_keep[1]], axis=0)
            s = _dot_t(qq, k2) + bias_ref[pair]
            if sb == 0:
                s = jnp.where(no_history, NEG_INF, s)
            m = jnp.max(s, axis=-1, keepdims=True)
            p = jnp.exp(s - m)
            l = jnp.sum(p, axis=-1, keepdims=True)
            a = jnp.dot(p.astype(BF16), v2, preferred_element_type=F32) * (1.0 / l)
            lse = m + jnp.log(l)
            o_ref[r0:r0 + BAND, ls] = jnp.where(left, a[:BAND], a[BAND:])
            stats = jnp.where(lane == 2 * pair, lse[:BAND], stats)
            stats = jnp.where(lane == 2 * pair + 1, lse[BAND:], stats)
        lse_ref[r0:r0 + BAND, :] = stats


def _attn_group(q, kvb, bias, grp):
    n, d, rows, _ = q.shape
    qb = ATT_QB
    per_blk = qb // BAND

    def prev_idx(i):
        return jnp.maximum(i * per_blk - 1, 0)

    def cur(col):
        return pl.BlockSpec((None, None, qb, HEADS_W), lambda b, r, i: (b, r, i, col))

    def prev(col):
        return pl.BlockSpec((None, None, BAND, HEADS_W), lambda b, r, i: (b, r, prev_idx(i), col))

    return pl.pallas_call(
        _attn_group_kernel,
        out_shape=(jax.ShapeDtypeStruct((n, d, rows, HEADS_W), F32),
                   jax.ShapeDtypeStruct((n, d, rows, LANES), F32)),
        grid=(n, d, rows // qb),
        in_specs=[cur(0), cur(0), prev(0), cur(1), prev(1),
                  _const_spec((N_HEADS // 2, 2 * BAND, 2 * BAND))],
        out_specs=(cur(0), pl.BlockSpec((None, None, qb, LANES), lambda b, r, i: (b, r, i, 0))),
        scratch_shapes=[pltpu.VMEM((BAND + qb, HEADS_W), BF16),
                        pltpu.VMEM((BAND + qb, HEADS_W), BF16)],
        compiler_params=_params(("arbitrary", "arbitrary", "arbitrary"), 32),
        name=f"attn_group{grp}",
    )(q, kvb, kvb, kvb, kvb, bias)


def _compact_cache_kernel(c_ref, eye_ref, sel_ref, recent_ref, far_ref, *, t_new):
    p = c_ref.shape[-1]
    d1, d2 = WINDOWS[1][1], WINDOWS[2][1]
    x = c_ref[...].reshape(KV_W, p).astype(BF16)
    lo = p - d1 * BAND
    for j in range(d1 * BAND // MXU_DIM):
        cols = x[:, lo + j * MXU_DIM:lo + (j + 1) * MXU_DIM]
        recent_ref[j * MXU_DIM:(j + 1) * MXU_DIM, :] = _dot_t(eye_ref[...], cols).astype(BF16)
    per = MXU_DIM // d2
    for c in range(p // MXU_DIM):
        picked = _dot_t(sel_ref[...], x[:, c * MXU_DIM:(c + 1) * MXU_DIM]).astype(BF16)
        for t in range(t_new):
            far_ref[t * BAND + c * per:t * BAND + (c + 1) * per, :] = picked[t * per:(t + 1) * per, :]


def _compact_cache(cache_kv, t_new):
    n, p = cache_kv.shape[:2]
    d1, d2 = WINDOWS[1][1], WINDOWS[2][1]
    per = MXU_DIM // d2
    assert p == WINDOWS[2][0] and p // d2 == BAND and t_new <= d1 and p % MXU_DIM == 0
    feat_major = jnp.transpose(cache_kv, (0, 2, 3, 4, 1))
    eye = jnp.eye(MXU_DIM, dtype=BF16)
    row = jnp.arange(t_new * per)[:, None]
    col = jnp.arange(MXU_DIM)[None, :]
    sel = jnp.where(col == (row % per) * d2 + row // per, 1.0, 0.0).astype(BF16)
    out_spec = lambda rows: pl.BlockSpec((None, rows, KV_W), lambda i: (i, 0, 0))
    return pl.pallas_call(
        functools.partial(_compact_cache_kernel, t_new=t_new),
        out_shape=(jax.ShapeDtypeStruct((n, d1 * BAND, KV_W), BF16),
                   jax.ShapeDtypeStruct((n, t_new * BAND, KV_W), BF16)),
        grid=(n,),
        in_specs=[pl.BlockSpec((None, 2, N_HEADS, HEAD_DIM, p), lambda i: (i, 0, 0, 0, 0)),
                  _const_spec((MXU_DIM, MXU_DIM)), _const_spec((t_new * per, MXU_DIM))],
        out_specs=(out_spec(d1 * BAND), out_spec(t_new * BAND)),
        compiler_params=_params(("arbitrary",), 48),
        name="compact_cache",
    )(feat_major, eye, sel)


def _attn_sample_kernel(q_ref, kvn_ref, recent_ref, far_ref, b0_ref, b1_ref, b2_ref, b2n_ref, o_ref):
    t_new = q_ref.shape[0]
    n_seq = SAMPLE_SEQ_STEP
    base = (pl.program_id(0) % (SAMPLE_ROWS // n_seq)) * n_seq
    n_rows = t_new * N_HEADS
    hsel = (lax.broadcasted_iota(jnp.int32, (N_HEADS, HEADS_W), 1) // HEAD_DIM
            == lax.broadcasted_iota(jnp.int32, (N_HEADS, HEADS_W), 0))
    row8 = lax.broadcasted_iota(jnp.int32, (SUBLANES, KV_W), 0)

    qs, kas, vas, kbs, vbs = [], [], [], [], []
    for j in range(n_seq):
        b = base + j
        qs.append([jnp.concatenate(
            [jnp.where(hsel, q_ref[t, pl.ds(b, 1), _lane_block(grp, HEADS_W)], 0.0) for t in range(t_new)],
            axis=0).astype(BF16) for grp in range(N_GROUPS)])
        new8 = jnp.zeros((SUBLANES, KV_W), F32)
        for t in range(t_new):
            new8 = jnp.where(row8 == t, kvn_ref[t, pl.ds(b, 1), :], new8)
        new = jnp.concatenate([new8, jnp.zeros((BAND - SUBLANES, KV_W), F32)], axis=0).astype(BF16)
        recent = recent_ref[j]
        kas.append(jnp.concatenate([recent[:, :HEADS_W], new[:, :HEADS_W]], axis=0))
        vas.append(jnp.concatenate([recent[:, HEADS_W:], new[:, HEADS_W:]], axis=0))
        kbs.append(far_ref[j, :, :HEADS_W])
        vbs.append(far_ref[j, :, HEADS_W:])
    n_a = kas[0].shape[0]

    def scores(grp, keys, lo, bias_ref):
        return jnp.concatenate([_dot_t(qs[j][grp], keys[j][lo:]) for j in range(n_seq)],
                               axis=0) + bias_ref[...]

    def softmax_pv(pieces):
        m = functools.reduce(jnp.maximum, [jnp.max(s, axis=-1, keepdims=True) for s, _, _ in pieces])
        l = jnp.zeros((n_seq * n_rows, 1), F32)
        acc = jnp.zeros((n_seq * n_rows, HEADS_W), F32)
        for s, vals, lo in pieces:
            p = jnp.exp(s - m)
            l = l + jnp.sum(p, axis=-1, keepdims=True)
            pb = p.astype(BF16)
            acc = acc + jnp.concatenate(
                [jnp.dot(pb[j * n_rows:(j + 1) * n_rows], vals[j][lo:], preferred_element_type=F32)
                 for j in range(n_seq)], axis=0)
        return acc, m, l

    lo0, lo_new = n_a - 2 * BAND, n_a - BAND
    parts = [
        softmax_pv([(scores(0, kas, lo0, b0_ref), vas, lo0)]),
        softmax_pv([(scores(1, kas, 0, b1_ref), vas, 0)]),
        softmax_pv([(scores(2, kbs, 0, b2_ref), vbs, 0), (scores(2, kas, lo_new, b2n_ref), vas, lo_new)]),
    ]
    m_all = functools.reduce(jnp.maximum, [m for _, m, _ in parts])
    num = jnp.zeros((n_seq * n_rows, HEADS_W), F32)
    den = jnp.zeros((n_seq * n_rows, 1), F32)
    for acc, m, l in parts:
        e = jnp.exp(m - m_all)
        num = num + e * acc
        den = den + e * l
    out = num * (1.0 / den)
    for j in range(n_seq):
        for t in range(t_new):
            r0 = j * n_rows + t * N_HEADS
            own = jnp.where(hsel, out[r0:r0 + N_HEADS, :], 0.0)
            o_ref[t, pl.ds(base + j, 1), :] = jnp.sum(own, axis=0, keepdims=True)


def _attn_sample(q, kvn, recent, far, biases):
    t_new, n, qw = q.shape
    step = SAMPLE_SEQ_STEP
    blk = lambda width: pl.BlockSpec((t_new, SAMPLE_ROWS, width),
                                     lambda i: (0, i // (SAMPLE_ROWS // step), 0))
    rows = lambda a: pl.BlockSpec((step,) + a.shape[1:], lambda i: (i, 0, 0))
    biases = [jnp.tile(b, (step, 1)) for b in biases]
    return pl.pallas_call(
        _attn_sample_kernel,
        out_shape=jax.ShapeDtypeStruct((t_new, n, HEADS_W), F32),
        grid=(n // step,),
        in_specs=[blk(qw), blk(KV_W), rows(recent), rows(far)] + [_const_spec(b.shape) for b in biases],
        out_specs=blk(HEADS_W),
        compiler_params=_params(("arbitrary",), 40),
        name="attn_sample",
    )(q, kvn, recent, far, *biases)


def _attn_out_sample_kernel(x_ref, o_ref, wo_ref, g_ref, wup_ref, wdn_ref, out_ref):
    x = x_ref[...] + jnp.dot(o_ref[...].astype(BF16), wo_ref[...], preferred_element_type=F32)
    out_ref[...] = _mlp_residual(x, g_ref, wup_ref, wdn_ref)


def _attn_out_prompt_kernel(x_ref, o0_ref, l0_ref, o1_ref, l1_ref, o2_ref, l2_ref, ex_ref, wo_ref,
                            g_ref, wup_ref, wdn_ref, out_ref, slab_ref):
    rows = x_ref.shape[0]
    n_o = HEADS_W // LANES
    for gi, (o_ref, l_ref) in enumerate(((o1_ref, l1_ref), (o2_ref, l2_ref))):
        d = WINDOWS[gi + 1][1]
        for r in range(d):
            dst = pl.ds(r, rows // d, stride=d)
            for s in range(n_o):
                slab_ref[gi, s, dst, :] = o_ref[r, :, _lane_block(s)]
            slab_ref[gi, n_o, dst, :] = l_ref[r]
    outs = [o0_ref[0]] + [jnp.concatenate([slab_ref[gi, s] for s in range(n_o)], axis=1) for gi in range(2)]
    lses = [l0_ref[0], slab_ref[0, n_o], slab_ref[1, n_o]]
    mx = functools.reduce(jnp.maximum, lses)
    es = [jnp.exp(l - mx) for l in lses]
    inv = 1.0 / functools.reduce(jnp.add, es)
    o = jnp.zeros((rows, HEADS_W), F32)
    for e, og in zip(es, outs):
        w = e * inv
        hi = w.astype(BF16)
        lo = (w - hi.astype(F32)).astype(BF16)
        wide = (jnp.dot(hi, ex_ref[...], preferred_element_type=F32)
                + jnp.dot(lo, ex_ref[...], preferred_element_type=F32))
        o = o + wide * og
    x = x_ref[...] + jnp.dot(o.astype(BF16), wo_ref[...], preferred_element_type=F32)
    out_ref[...] = _mlp_residual(x, g_ref, wup_ref, wdn_ref)


def _mlp_weight_specs():
    return [_const_spec((HEADS_W, D_MODEL)), _const_spec((1, D_MODEL)),
            _const_spec((D_MODEL, D_FF)), _const_spec((D_FF, D_MODEL))]


def _attn_out_sample(x, o, wo, g, wup, wdn):
    m = x.shape[0]
    row = pl.BlockSpec((ROW_TILE, D_MODEL), lambda i: (i, 0))
    return pl.pallas_call(
        _attn_out_sample_kernel,
        out_shape=jax.ShapeDtypeStruct((m, D_MODEL), F32),
        grid=(m // ROW_TILE,),
        in_specs=[row, pl.BlockSpec((ROW_TILE, HEADS_W), lambda i: (i, 0))] + _mlp_weight_specs(),
        out_specs=row,
        compiler_params=_params(("arbitrary",), 48),
        name="attn_out_sample",
    )(x, o, wo, g, wup, wdn)


def _attn_out_prompt(x, parts, ex, wo, g, wup, wdn):
    n, t, _ = x.shape
    tile = ATT_OUT_TILE
    row = pl.BlockSpec((None, tile, D_MODEL), lambda b, i: (b, i, 0))
    part_specs = []
    for _, d in WINDOWS:
        for width in (HEADS_W, LANES):
            part_specs.append(pl.BlockSpec((None, d, tile // d, width), lambda b, i: (b, 0, i, 0)))
    flat = [a for pair in parts for a in pair]
    return pl.pallas_call(
        _attn_out_prompt_kernel,
        out_shape=jax.ShapeDtypeStruct((n, t, D_MODEL), F32),
        grid=(n, t // tile),
        in_specs=[row] + part_specs + [_const_spec((LANES, HEADS_W))] + _mlp_weight_specs(),
        out_specs=row,
        scratch_shapes=[pltpu.VMEM((2, HEADS_W // LANES + 1, tile, LANES), F32)],
        compiler_params=_params(("arbitrary", "arbitrary"), 48),
        name="attn_out_prompt",
    )(x, *flat, ex, wo, g, wup, wdn)


def _alibi_slopes():
    n = N_GROUPS * N_HEADS
    s = 2.0 ** (-8.0 * jnp.arange(1, n + 1, dtype=F32) / n)
    return s.reshape(N_GROUPS, N_HEADS)


def _prompt_bias(slopes, grp):
    d = WINDOWS[grp][1]
    iq = jnp.arange(BAND, dtype=jnp.int32)[:, None]
    ik = jnp.arange(2 * BAND, dtype=jnp.int32)[None, :]
    steps = iq + BAND - ik
    valid = (steps >= 0) & (steps <= BAND)
    offs = (steps * d).astype(F32)
    per_head = jnp.where(valid[None], -slopes[grp][:, None, None] * offs[None], NEG_INF)
    return per_head.reshape(N_HEADS // 2, 2 * BAND, 2 * BAND)


def _sample_bias(slopes, t_new, past):
    d1, d2 = WINDOWS[1][1], WINDOWS[2][1]
    t = jnp.arange(t_new, dtype=jnp.int32)[:, None]

    def table(slope, dist, valid):
        b = jnp.where(valid[:, None, :], -slope[None, :, None] * dist.astype(F32)[:, None, :], NEG_INF)
        return b.reshape(t_new * N_HEADS, -1)

    u = jnp.arange(BAND, dtype=jnp.int32)[None, :]
    dist_n = t - u
    ok_n = (u < t_new) & (dist_n >= 0)
    c = jnp.arange(BAND, dtype=jnp.int32)[None, :]
    dist_c = BAND + t - c
    b0 = table(slopes[0], jnp.concatenate([dist_c, dist_n], axis=1),
               jnp.concatenate([dist_c <= WINDOWS[0][0], ok_n], axis=1))
    c = jnp.arange(d1 * BAND, dtype=jnp.int32)[None, :]
    dist_c = d1 * BAND + t - c
    ok_c = (dist_c % d1 == 0) & (dist_c <= WINDOWS[1][0])
    b1 = table(slopes[1], jnp.concatenate([dist_c, dist_n], axis=1),
               jnp.concatenate([ok_c, ok_n & (dist_n % d1 == 0)], axis=1))
    per = past // d2
    col = jnp.arange(t_new * per, dtype=jnp.int32)[None, :]
    pos = (col % per) * d2 + col // per
    dist_c = past + t - pos
    ok_c = (dist_c % d2 == 0) & (dist_c <= WINDOWS[2][0])
    b2 = table(slopes[2], dist_c, ok_c)
    b2n = table(slopes[2], dist_n, ok_n & (dist_n % d2 == 0))
    return b0, b1, b2, b2n


def _head_mean_matrix():
    idx = jnp.arange(HEADS_W) // HEAD_DIM
    return jnp.where(idx[:, None] == idx[None, :], 1.0 / HEAD_DIM, 0.0).astype(BF16)


def _head_expand_matrix():
    lane = jnp.arange(LANES)[:, None]
    col = jnp.arange(HEADS_W)[None, :] // HEAD_DIM
    return jnp.where(lane == col, 1.0, 0.0).astype(BF16)


def kernel(x_prompt, x_sample, state_conv, cache_kv, norm_mix_g, norm_mlp_g, conv_w_pw1, conv_b_pw1,
           conv_w_dw, conv_b_dw, conv_ln_g, conv_ln_b, conv_w_pw2, conv_b_pw2, kv_norm_g, w_kv,
           k_norm_g, attn_w_q, q_norm_g, attn_w_o, mlp_w_up, mlp_w_down):
    n_p, t_p, _ = x_prompt.shape
    n_s, t_s, _ = x_sample.shape
    past = cache_kv.shape[1]
    scale = HEAD_DIM ** -0.5

    w1 = conv_w_pw1.astype(BF16)
    w2 = conv_w_pw2.astype(BF16)
    wkv = w_kv.astype(BF16)
    wq = attn_w_q.astype(BF16)
    wo = attn_w_o.astype(BF16)
    wup = mlp_w_up.astype(BF16)
    wdn = mlp_w_down.astype(BF16)
    row = lambda a: a.reshape(1, -1)
    e_mat = _head_mean_matrix()
    ex_mat = _head_expand_matrix()
    gk = row(jnp.tile(k_norm_g, N_HEADS))
    gq = [row(jnp.tile(q_norm_g[j], N_HEADS) * scale) for j in range(N_B_LAYERS)]
    slopes = _alibi_slopes()
    bias_p = [_prompt_bias(slopes, g) for g in range(N_GROUPS)]
    bias_s = _sample_bias(slopes, t_s, past)

    def conv_args(layer):
        return (row(norm_mix_g[layer]), w1[layer], row(conv_b_pw1[layer]), conv_w_dw[layer],
                row(conv_b_dw[layer]), row(conv_ln_g[layer]), row(conv_ln_b[layer]),
                w2[layer], row(conv_b_pw2[layer]))

    xp = x_prompt
    conv_p = []
    zero_hist = jnp.zeros((n_p, HIST_PAD, CONV_CH), F32)
    for layer in range(N_A_LAYERS):
        xp, st = _conv_prompt(xp, zero_hist, *conv_args(layer))
        conv_p.append(st[:, HIST_PAD - CONV_HIST:])
        xp = _mlp(xp.reshape(n_p * t_p, D_MODEL), row(norm_mlp_g[layer]), wup[layer],
                  wdn[layer]).reshape(n_p, t_p, D_MODEL)
    win = min(WINDOWS[-1][0], t_p)
    kv_p, *kvb_p = _kv_proj_prompt(xp, row(kv_norm_g), wkv, e_mat, gk, win)
    for j in range(N_B_LAYERS):
        layer = N_A_LAYERS + j
        qs = _q_proj_prompt(xp, row(norm_mix_g[layer]), wq[j], e_mat, gq[j])
        parts = [_attn_group(qs[g], kvb_p[g], bias_p[g], g) for g in range(N_GROUPS)]
        xp = _attn_out_prompt(xp, parts, ex_mat, wo[j], row(norm_mlp_g[layer]), wup[layer], wdn[layer])
    y_prompt = xp
    conv_prompt = jnp.stack(conv_p, axis=0)
    kv_prompt = kv_p.reshape(n_p, win, 2, N_HEADS, HEAD_DIM)

    xs = jnp.transpose(x_sample, (1, 0, 2))
    st_s = jnp.transpose(state_conv, (0, 2, 1, 3))
    conv_s = []
    for layer in range(N_A_LAYERS):
        xs, st = _conv_sample(xs, st_s[layer], *conv_args(layer))
        conv_s.append(jnp.transpose(st, (1, 0, 2)))
        xs = _mlp(xs.reshape(t_s * n_s, D_MODEL), row(norm_mlp_g[layer]), wup[layer],
                  wdn[layer]).reshape(t_s, n_s, D_MODEL)
    xs = xs.reshape(t_s * n_s, D_MODEL)
    kvn = _kv_proj_sample(xs, row(kv_norm_g), wkv, e_mat, gk).reshape(t_s, n_s, KV_W)
    recent, far = _compact_cache(cache_kv, t_s)
    for j in range(N_B_LAYERS):
        layer = N_A_LAYERS + j
        q = _q_proj_sample(xs, row(norm_mix_g[layer]), wq[j], e_mat, gq[j])
        o = _attn_sample(q.reshape(t_s, n_s, N_GROUPS * HEADS_W), kvn, recent, far, bias_s)
        xs = _attn_out_sample(xs, o.reshape(t_s * n_s, HEADS_W), wo[j], row(norm_mlp_g[layer]),
                              wup[layer], wdn[layer])
    y_sample = jnp.transpose(xs.reshape(t_s, n_s, D_MODEL), (1, 0, 2))
    conv_sample = jnp.stack(conv_s, axis=0)
    kv_sample = jnp.transpose(kvn, (1, 0, 2)).reshape(n_s, t_s, 2, N_HEADS, HEAD_DIM)

    return (y_prompt, y_sample, conv_prompt, conv_sample, kv_prompt, kv_sample)
```

```python
import functools

import jax
import jax.numpy as jnp
from jax import lax
from jax.experimental import pallas as pl
from jax.experimental.pallas import tpu as pltpu

F32 = jnp.float32
BF16 = jnp.bfloat16

D_MODEL = 1024
CONV_CH = 1024
CONV_WIDTH = 31
CONV_HIST = CONV_WIDTH - 1
HIST_PAD = 32
D_FF = 4096
N_HEADS = 8
HEAD_DIM = 64
HEADS_W = N_HEADS * HEAD_DIM
KV_W = 2 * HEADS_W
WINDOWS = ((128, 1), (512, 4), (2048, 16))
N_GROUPS = len(WINDOWS)
BAND = 128
N_A_LAYERS = 2
N_B_LAYERS = 2
NORM_EPS = 1e-6
NEG_INF = float("-inf")
LANES = 128
SUBLANES = 8

ROW_TILE = 512
ATT_OUT_TILE = 512
FF_CHUNK = 1024
ATT_QB = 1024
SAMPLE_CONV_NB = 32
SAMPLE_ROWS = 8
SAMPLE_SEQ_STEP = 4
MXU_DIM = 256
MIB = 1024 * 1024

assert all(w // d == BAND for w, d in WINDOWS)


def _const_spec(shape):
    nd = len(shape)
    return pl.BlockSpec(shape, lambda *_: (0,) * nd, pipeline_mode=pl.Buffered(1))


def _layer_spec(stacked, layer):
    return pl.BlockSpec((None,) + stacked.shape[1:], lambda *_: (layer, 0, 0),
                        pipeline_mode=pl.Buffered(1))


def _params(semantics, vmem_mib):
    return pltpu.CompilerParams(dimension_semantics=semantics, vmem_limit_bytes=vmem_mib * MIB)


def _lane_block(s, width=LANES):
    return slice(s * width, (s + 1) * width)


def _rms(x, g):
    return x * lax.rsqrt(jnp.mean(x * x, axis=-1, keepdims=True) + NORM_EPS) * g


def _mlp_residual(x, g_ref, wup_ref, wdn_ref):
    h = _rms(x, g_ref[...]).astype(BF16)
    acc = x
    for c in range(D_FF // FF_CHUNK):
        sl = _lane_block(c, FF_CHUNK)
        z = jnp.maximum(jnp.dot(h, wup_ref[:, sl], preferred_element_type=F32), 0.0)
        acc = acc + jnp.dot((z * z).astype(BF16), wdn_ref[sl, :], preferred_element_type=F32)
    return acc


def _head_mean_sq(v, e_ref):
    return jnp.dot((v * v).astype(BF16), e_ref[...], preferred_element_type=F32)


def _dot_t(a, b):
    return lax.dot_general(a, b, (((1,), (1,)), ((), ())), preferred_element_type=F32)


def _deinterleave_store(val, slab_ref, out_ref, d):
    rows, width = val.shape
    for s in range(width // LANES):
        slab_ref[s] = val[:, _lane_block(s)]
    for s in range(width // LANES):
        for r in range(d):
            out_ref[r, :, _lane_block(s)] = slab_ref[s, pl.ds(r, rows // d, stride=d), :].astype(BF16)


def _mlp_kernel(x_ref, g_ref, wup_ref, wdn_ref, o_ref):
    o_ref[...] = _mlp_residual(x_ref[...], g_ref, wup_ref, wdn_ref)


def _mlp(x, layer, g, wup, wdn):
    m = x.shape[0]
    row = pl.BlockSpec((ROW_TILE, D_MODEL), lambda i: (i, 0))
    return pl.pallas_call(
        _mlp_kernel,
        out_shape=jax.ShapeDtypeStruct((m, D_MODEL), F32),
        grid=(m // ROW_TILE,),
        in_specs=[row] + [_layer_spec(a, layer) for a in (g, wup, wdn)],
        out_specs=row,
        compiler_params=_params(("arbitrary",), 48),
        name="mlp",
    )(x, g, wup, wdn)


def _glu(h, w1_ref, b1_ref):
    u = jnp.dot(h, w1_ref[...], preferred_element_type=F32) + b1_ref[...]
    return u[:, :CONV_CH] * jax.nn.sigmoid(u[:, CONV_CH:])


def _conv_tail(c, x, bdw_ref, lng_ref, lnb_ref, w2_ref, b2_ref):
    c = c + bdw_ref[...]
    mu = jnp.mean(c, axis=-1, keepdims=True)
    xc = c - mu
    y = xc * lax.rsqrt(jnp.mean(xc * xc, axis=-1, keepdims=True) + NORM_EPS)
    y = y * lng_ref[...] + lnb_ref[...]
    y = y * jax.nn.sigmoid(y)
    out = jnp.dot(y.astype(BF16), w2_ref[...], preferred_element_type=F32) + b2_ref[...]
    return x + out


def _conv_prompt_kernel(x_ref, st_ref, g_ref, w1_ref, b1_ref, wdw_ref, bdw_ref, lng_ref, lnb_ref,
                        w2_ref, b2_ref, o_ref, sto_ref, buf_ref):
    n_slabs = CONV_CH // LANES

    @pl.when(pl.program_id(1) == 0)
    def _():
        for s in range(n_slabs):
            buf_ref[s, 0:HIST_PAD, :] = st_ref[:, _lane_block(s)]

    x = x_ref[...]
    h = _rms(x, g_ref[...]).astype(BF16)
    u = _glu(h, w1_ref, b1_ref)
    for s in range(n_slabs):
        buf_ref[s, HIST_PAD:HIST_PAD + ROW_TILE, :] = u[:, _lane_block(s)]
    off = HIST_PAD - CONV_HIST
    cs = []
    for s in range(n_slabs):
        ls = _lane_block(s)
        c = buf_ref[s, off:off + ROW_TILE, :] * wdw_ref[0:1, ls]
        for k in range(1, CONV_WIDTH):
            c = c + buf_ref[s, off + k:off + k + ROW_TILE, :] * wdw_ref[k:k + 1, ls]
        cs.append(c)
    c = jnp.concatenate(cs, axis=1)
    o_ref[...] = _conv_tail(c, x, bdw_ref, lng_ref, lnb_ref, w2_ref, b2_ref)
    for s in range(n_slabs):
        last = buf_ref[s, ROW_TILE:ROW_TILE + HIST_PAD, :]
        sto_ref[:, _lane_block(s)] = last
        buf_ref[s, 0:HIST_PAD, :] = last


def _conv_prompt(x, st, layer, weights):
    n, t, _ = x.shape
    row = pl.BlockSpec((None, ROW_TILE, D_MODEL), lambda b, i: (b, i, 0))
    st_spec = pl.BlockSpec((None, HIST_PAD, CONV_CH), lambda b, i: (b, 0, 0))
    return pl.pallas_call(
        _conv_prompt_kernel,
        out_shape=(jax.ShapeDtypeStruct((n, t, D_MODEL), F32),
                   jax.ShapeDtypeStruct((n, HIST_PAD, CONV_CH), F32)),
        grid=(n, t // ROW_TILE),
        in_specs=[row, st_spec] + [_layer_spec(a, layer) for a in weights],
        out_specs=(row, st_spec),
        scratch_shapes=[pltpu.VMEM((CONV_CH // LANES, HIST_PAD + ROW_TILE, LANES), F32)],
        compiler_params=_params(("arbitrary", "arbitrary"), 48),
        name="conv_prompt",
    )(x, st, *weights)


def _conv_sample_kernel(x_ref, st_ref, g_ref, w1_ref, b1_ref, wdw_ref, bdw_ref, lng_ref, lnb_ref,
                        w2_ref, b2_ref, o_ref, sto_ref):
    t_new, nb, _ = x_ref.shape
    x = x_ref[...].reshape(t_new * nb, D_MODEL)
    h = _rms(x, g_ref[...]).astype(BF16)
    glu = _glu(h, w1_ref, b1_ref)
    new = [glu[t * nb:(t + 1) * nb, :] for t in range(t_new)]

    def full(j):
        return st_ref[j] if j < CONV_HIST else new[j - CONV_HIST]

    outs = []
    for t in range(t_new):
        c = full(t) * wdw_ref[0:1, :]
        for k in range(1, CONV_WIDTH):
            c = c + full(t + k) * wdw_ref[k:k + 1, :]
        outs.append(c)
    c = jnp.concatenate(outs, axis=0)
    o_ref[...] = _conv_tail(c, x, bdw_ref, lng_ref, lnb_ref, w2_ref, b2_ref).reshape(t_new, nb, D_MODEL)
    for j in range(CONV_HIST):
        sto_ref[j] = full(j + t_new)


def _conv_sample_chained_kernel(*refs):
    _conv_sample_kernel(*refs[:11], *refs[12:])


def _conv_sample(x, st_all, layer, weights, st_out=None):
    t_new, n, _ = x.shape
    nb = SAMPLE_CONV_NB
    x_spec = pl.BlockSpec((t_new, nb, D_MODEL), lambda i: (0, i, 0))
    st_spec = pl.BlockSpec((None, CONV_HIST, nb, CONV_CH), lambda i: (layer, 0, i, 0))
    in_specs = [x_spec, st_spec] + [_layer_spec(a, layer) for a in weights]
    args = [x, st_all, *weights]
    if st_out is None:
        body, aliases = _conv_sample_kernel, {}
    else:
        body, aliases = _conv_sample_chained_kernel, {len(args): 1}
        in_specs.append(pl.BlockSpec(memory_space=pl.ANY))
        args.append(st_out)
    return pl.pallas_call(
        body,
        out_shape=(jax.ShapeDtypeStruct((t_new, n, D_MODEL), F32),
                   jax.ShapeDtypeStruct(st_all.shape, F32)),
        grid=(n // nb,),
        in_specs=in_specs,
        out_specs=(x_spec, st_spec),
        input_output_aliases=aliases,
        compiler_params=_params(("arbitrary",), 48),
        name="conv_sample",
    )(*args)


def _kv_rows(x_ref, g_ref, wkv_ref, e_ref, gk_ref):
    h = _rms(x_ref[...], g_ref[...]).astype(BF16)
    k = jnp.dot(h, wkv_ref[:, :HEADS_W], preferred_element_type=F32)
    v = jnp.dot(h, wkv_ref[:, HEADS_W:], preferred_element_type=F32)
    k = k * lax.rsqrt(_head_mean_sq(k, e_ref) + NORM_EPS) * gk_ref[...]
    return jnp.concatenate([k, v], axis=1)


def _kv_sample_kernel(x_ref, g_ref, wkv_ref, e_ref, gk_ref, kv_ref):
    kv_ref[...] = _kv_rows(x_ref, g_ref, wkv_ref, e_ref, gk_ref)


def _kv_prompt_kernel(x_ref, g_ref, wkv_ref, e_ref, gk_ref, kv_ref, kb0_ref, kb1_ref, kb2_ref, slab_ref):
    kv = _kv_rows(x_ref, g_ref, wkv_ref, e_ref, gk_ref)
    kv_ref[...] = kv
    kb0_ref[0] = kv.astype(BF16)
    _deinterleave_store(kv, slab_ref, kb1_ref, WINDOWS[1][1])
    _deinterleave_store(kv, slab_ref, kb2_ref, WINDOWS[2][1])


def _proj_weight_specs(layer, g, w, e, gh):
    return [_layer_spec(g, layer), _layer_spec(w, layer), _const_spec(e.shape), _layer_spec(gh, layer)]


def _kv_proj_sample(x, g, wkv, e, gk):
    m = x.shape[0]
    return pl.pallas_call(
        _kv_sample_kernel,
        out_shape=jax.ShapeDtypeStruct((m, KV_W), F32),
        grid=(m // ROW_TILE,),
        in_specs=[pl.BlockSpec((ROW_TILE, D_MODEL), lambda i: (i, 0))] + _proj_weight_specs(0, g, wkv, e, gk),
        out_specs=pl.BlockSpec((ROW_TILE, KV_W), lambda i: (i, 0)),
        compiler_params=_params(("arbitrary",), 32),
        name="kv_proj_sample",
    )(x, g, wkv, e, gk)


def _dilated_shapes(n, t, width, dtype):
    sds, specs = [], []
    for _, d in WINDOWS:
        sds.append(jax.ShapeDtypeStruct((n, d, t // d, width), dtype))
        specs.append(pl.BlockSpec((None, d, ROW_TILE // d, width), lambda b, i: (b, 0, i, 0)))
    return sds, specs


def _kv_proj_prompt(x, g, wkv, e, gk, win):
    n, t, _ = x.shape
    sds, specs = _dilated_shapes(n, t, KV_W, BF16)
    skip = (t - win) // ROW_TILE
    assert win % ROW_TILE == 0 and t % ROW_TILE == 0
    f32_spec = pl.BlockSpec((None, ROW_TILE, KV_W), lambda b, i: (b, jnp.maximum(i - skip, 0), 0))
    return pl.pallas_call(
        _kv_prompt_kernel,
        out_shape=[jax.ShapeDtypeStruct((n, win, KV_W), F32)] + sds,
        grid=(n, t // ROW_TILE),
        in_specs=[pl.BlockSpec((None, ROW_TILE, D_MODEL), lambda b, i: (b, i, 0))]
        + _proj_weight_specs(0, g, wkv, e, gk),
        out_specs=[f32_spec] + specs,
        scratch_shapes=[pltpu.VMEM((KV_W // LANES, ROW_TILE, LANES), F32)],
        compiler_params=_params(("arbitrary", "arbitrary"), 40),
        name="kv_proj_prompt",
    )(x, g, wkv, e, gk)


def _q_group(h, wq_ref, e_ref, gq_ref, grp):
    q = jnp.dot(h, wq_ref[:, _lane_block(grp, HEADS_W)], preferred_element_type=F32)
    return q * lax.rsqrt(_head_mean_sq(q, e_ref) + NORM_EPS) * gq_ref[...]


def _q_sample_kernel(x_ref, g_ref, wq_ref, e_ref, gq_ref, q_ref):
    h = _rms(x_ref[...], g_ref[...]).astype(BF16)
    for grp in range(N_GROUPS):
        q = _q_group(h, wq_ref, e_ref, gq_ref, grp)
        q_ref[:, _lane_block(grp, HEADS_W)] = q.astype(BF16).astype(F32)


def _q_prompt_kernel(x_ref, g_ref, wq_ref, e_ref, gq_ref, q0_ref, q1_ref, q2_ref, slab_ref):
    h = _rms(x_ref[...], g_ref[...]).astype(BF16)
    q0_ref[0] = _q_group(h, wq_ref, e_ref, gq_ref, 0).astype(BF16)
    _deinterleave_store(_q_group(h, wq_ref, e_ref, gq_ref, 1), slab_ref, q1_ref, WINDOWS[1][1])
    _deinterleave_store(_q_group(h, wq_ref, e_ref, gq_ref, 2), slab_ref, q2_ref, WINDOWS[2][1])


def _q_proj_sample(x, layer, j, g, wq, e, gq):
    m = x.shape[0]
    qw = N_GROUPS * HEADS_W
    specs = [_layer_spec(g, layer), _layer_spec(wq, j), _const_spec(e.shape), _layer_spec(gq, j)]
    return pl.pallas_call(
        _q_sample_kernel,
        out_shape=jax.ShapeDtypeStruct((m, qw), F32),
        grid=(m // ROW_TILE,),
        in_specs=[pl.BlockSpec((ROW_TILE, D_MODEL), lambda i: (i, 0))] + specs,
        out_specs=pl.BlockSpec((ROW_TILE, qw), lambda i: (i, 0)),
        compiler_params=_params(("arbitrary",), 32),
        name="q_proj_sample",
    )(x, g, wq, e, gq)


def _q_proj_prompt(x, layer, j, g, wq, e, gq):
    n, t, _ = x.shape
    sds, specs = _dilated_shapes(n, t, HEADS_W, BF16)
    w_specs = [_layer_spec(g, layer), _layer_spec(wq, j), _const_spec(e.shape), _layer_spec(gq, j)]
    return pl.pallas_call(
        _q_prompt_kernel,
        out_shape=sds,
        grid=(n, t // ROW_TILE),
        in_specs=[pl.BlockSpec((None, ROW_TILE, D_MODEL), lambda b, i: (b, i, 0))] + w_specs,
        out_specs=specs,
        scratch_shapes=[pltpu.VMEM((HEADS_W // LANES, ROW_TILE, LANES), F32)],
        compiler_params=_params(("arbitrary", "arbitrary"), 32),
        name="q_proj_prompt",
    )(x, g, wq, e, gq)


def _attn_group_kernel(q_ref, kc_ref, kp_ref, vc_ref, vp_ref, bias_ref, acc_ref, m_ref, l_ref,
                       kk_ref, vv_ref):
    kk_ref[0:BAND, :] = kp_ref[...]
    kk_ref[BAND:, :] = kc_ref[...]
    vv_ref[0:BAND, :] = vp_ref[...]
    vv_ref[BAND:, :] = vc_ref[...]
    first_block = pl.program_id(2) == 0
    key_col = lax.broadcasted_iota(jnp.int32, (2 * BAND, 2 * BAND), 1)
    lane = lax.broadcasted_iota(jnp.int32, (BAND, LANES), 1)
    left = lane < HEAD_DIM
    lane_row = lax.broadcasted_iota(jnp.int32, (1, LANES), 1)
    head_keep = [jnp.where(lane_row < HEAD_DIM, 1.0, 0.0).astype(BF16),
                 jnp.where(lane_row < HEAD_DIM, 0.0, 1.0).astype(BF16)]
    no_history = jnp.logical_and(first_block, key_col < BAND)

    for sb in range(q_ref.shape[0] // BAND):
        r0 = sb * BAND
        m_tile = jnp.zeros((BAND, LANES), F32)
        l_tile = jnp.ones((BAND, LANES), F32)
        for pair in range(N_HEADS // 2):
            ls = _lane_block(pair)
            q2 = q_ref[r0:r0 + BAND, ls]
            k2 = kk_ref[r0:r0 + 2 * BAND, ls]
            v2 = vv_ref[r0:r0 + 2 * BAND, ls]
            qq = jnp.concatenate([q2 * head_keep[0], q2 * head_keep[1]], axis=0)
            s = _dot_t(qq, k2) + bias_ref[pair]
            if sb == 0:
                s = jnp.where(no_history, NEG_INF, s)
            m = jnp.max(s, axis=-1, keepdims=True)
            p = jnp.exp(s - m)
            l = jnp.sum(p, axis=-1, keepdims=True)
            a = jnp.dot(p.astype(BF16), v2, preferred_element_type=F32)
            acc_ref[r0:r0 + BAND, ls] = jnp.where(left, a[:BAND], a[BAND:])
            for hh in range(2):
                rows = slice(hh * BAND, (hh + 1) * BAND)
                m_tile = jnp.where(lane == 2 * pair + hh, m[rows], m_tile)
                l_tile = jnp.where(lane == 2 * pair + hh, l[rows], l_tile)
        m_ref[r0:r0 + BAND, :] = m_tile
        l_ref[r0:r0 + BAND, :] = l_tile


def _attn_group(q, kvb, bias, grp):
    n, d, rows, _ = q.shape
    qb = min(ATT_QB, rows)
    per_blk = qb // BAND
    stat_sds = jax.ShapeDtypeStruct((n, d, rows, LANES), F32)
    stat_spec = pl.BlockSpec((None, None, qb, LANES), lambda b, r, i: (b, r, i, 0))

    def prev_idx(i):
        return jnp.maximum(i * per_blk - 1, 0)

    def cur(col):
        return pl.BlockSpec((None, None, qb, HEADS_W), lambda b, r, i: (b, r, i, col))

    def prev(col):
        return pl.BlockSpec((None, None, BAND, HEADS_W), lambda b, r, i: (b, r, prev_idx(i), col))

    return pl.pallas_call(
        _attn_group_kernel,
        out_shape=(jax.ShapeDtypeStruct((n, d, rows, HEADS_W), F32), stat_sds, stat_sds),
        grid=(n, d, rows // qb),
        in_specs=[cur(0), cur(0), prev(0), cur(1), prev(1),
                  _const_spec((N_HEADS // 2, 2 * BAND, 2 * BAND))],
        out_specs=(cur(0), stat_spec, stat_spec),
        scratch_shapes=[pltpu.VMEM((BAND + qb, HEADS_W), BF16),
                        pltpu.VMEM((BAND + qb, HEADS_W), BF16)],
        compiler_params=_params(("arbitrary", "arbitrary", "arbitrary"), 32),
        name=f"attn_group{grp}",
    )(q, kvb, kvb, kvb, kvb, bias)


def _compact_cache_kernel(c_ref, eye_ref, sel_ref, recent_ref, far_ref, *, t_new):
    p = c_ref.shape[-1]
    d1, d2 = WINDOWS[1][1], WINDOWS[2][1]
    x = c_ref[...].reshape(KV_W, p).astype(BF16)
    lo = p - d1 * BAND
    for j in range(d1 * BAND // MXU_DIM):
        cols = x[:, lo + j * MXU_DIM:lo + (j + 1) * MXU_DIM]
        recent_ref[j * MXU_DIM:(j + 1) * MXU_DIM, :] = _dot_t(eye_ref[...], cols).astype(BF16)
    per = MXU_DIM // d2
    for c in range(p // MXU_DIM):
        picked = _dot_t(sel_ref[...], x[:, c * MXU_DIM:(c + 1) * MXU_DIM]).astype(BF16)
        for t in range(t_new):
            far_ref[t * BAND + c * per:t * BAND + (c + 1) * per, :] = picked[t * per:(t + 1) * per, :]


def _compact_cache(cache_kv, t_new):
    n, p = cache_kv.shape[:2]
    d1, d2 = WINDOWS[1][1], WINDOWS[2][1]
    per = MXU_DIM // d2
    assert p == WINDOWS[2][0] and p // d2 == BAND and t_new <= d1 and p % MXU_DIM == 0
    feat_major = jnp.transpose(cache_kv, (0, 2, 3, 4, 1))
    eye = jnp.eye(MXU_DIM, dtype=BF16)
    row = jnp.arange(t_new * per)[:, None]
    col = jnp.arange(MXU_DIM)[None, :]
    sel = jnp.where(col == (row % per) * d2 + row // per, 1.0, 0.0).astype(BF16)
    out_spec = lambda rows: pl.BlockSpec((None, rows, KV_W), lambda i: (i, 0, 0))
    return pl.pallas_call(
        functools.partial(_compact_cache_kernel, t_new=t_new),
        out_shape=(jax.ShapeDtypeStruct((n, d1 * BAND, KV_W), BF16),
                   jax.ShapeDtypeStruct((n, t_new * BAND, KV_W), BF16)),
        grid=(n,),
        in_specs=[pl.BlockSpec((None, 2, N_HEADS, HEAD_DIM, p), lambda i: (i, 0, 0, 0, 0)),
                  _const_spec((MXU_DIM, MXU_DIM)), _const_spec((t_new * per, MXU_DIM))],
        out_specs=(out_spec(d1 * BAND), out_spec(t_new * BAND)),
        compiler_params=_params(("arbitrary",), 48),
        name="compact_cache",
    )(feat_major, eye, sel)


def _attn_sample_kernel(q_ref, kvn_ref, recent_ref, far_ref, b0_ref, b1_ref, b2_ref, b2n_ref, o_ref):
    t_new = q_ref.shape[0]
    n_seq = SAMPLE_SEQ_STEP
    base = (pl.program_id(0) % (SAMPLE_ROWS // n_seq)) * n_seq
    n_rows = t_new * N_HEADS
    hsel = (lax.broadcasted_iota(jnp.int32, (N_HEADS, HEADS_W), 1) // HEAD_DIM
            == lax.broadcasted_iota(jnp.int32, (N_HEADS, HEADS_W), 0))
    row8 = lax.broadcasted_iota(jnp.int32, (SUBLANES, KV_W), 0)

    qs, kas, vas, kbs, vbs = [], [], [], [], []
    for j in range(n_seq):
        b = base + j
        qs.append([jnp.concatenate(
            [jnp.where(hsel, q_ref[t, pl.ds(b, 1), _lane_block(grp, HEADS_W)], 0.0) for t in range(t_new)],
            axis=0).astype(BF16) for grp in range(N_GROUPS)])
        new8 = jnp.zeros((SUBLANES, KV_W), F32)
        for t in range(t_new):
            new8 = jnp.where(row8 == t, kvn_ref[t, pl.ds(b, 1), :], new8)
        new = jnp.concatenate([new8, jnp.zeros((BAND - SUBLANES, KV_W), F32)], axis=0).astype(BF16)
        recent = recent_ref[j]
        kas.append(jnp.concatenate([recent[:, :HEADS_W], new[:, :HEADS_W]], axis=0))
        vas.append(jnp.concatenate([recent[:, HEADS_W:], new[:, HEADS_W:]], axis=0))
        kbs.append(far_ref[j, :, :HEADS_W])
        vbs.append(far_ref[j, :, HEADS_W:])
    n_a = kas[0].shape[0]

    def scores(grp, keys, lo, bias_ref):
        return jnp.concatenate([_dot_t(qs[j][grp], keys[j][lo:]) for j in range(n_seq)],
                               axis=0) + bias_ref[...]

    def softmax_pv(pieces):
        m = functools.reduce(jnp.maximum, [jnp.max(s, axis=-1, keepdims=True) for s, _, _ in pieces])
        l = jnp.zeros((n_seq * n_rows, 1), F32)
        acc = jnp.zeros((n_seq * n_rows, HEADS_W), F32)
        for s, vals, lo in pieces:
            p = jnp.exp(s - m)
            l = l + jnp.sum(p, axis=-1, keepdims=True)
            pb = p.astype(BF16)
            acc = acc + jnp.concatenate(
                [jnp.dot(pb[j * n_rows:(j + 1) * n_rows], vals[j][lo:], preferred_element_type=F32)
                 for j in range(n_seq)], axis=0)
        return acc, m, l

    lo0, lo_new = n_a - 2 * BAND, n_a - BAND
    parts = [
        softmax_pv([(scores(0, kas, lo0, b0_ref), vas, lo0)]),
        softmax_pv([(scores(1, kas, 0, b1_ref), vas, 0)]),
        softmax_pv([(scores(2, kbs, 0, b2_ref), vbs, 0), (scores(2, kas, lo_new, b2n_ref), vas, lo_new)]),
    ]
    m_all = functools.reduce(jnp.maximum, [m for _, m, _ in parts])
    num = jnp.zeros((n_seq * n_rows, HEADS_W), F32)
    den = jnp.zeros((n_seq * n_rows, 1), F32)
    for acc, m, l in parts:
        e = jnp.exp(m - m_all)
        num = num + e * acc
        den = den + e * l
    out = num * (1.0 / den)
    for j in range(n_seq):
        for t in range(t_new):
            r0 = j * n_rows + t * N_HEADS
            own = jnp.where(hsel, out[r0:r0 + N_HEADS, :], 0.0)
            o_ref[t, pl.ds(base + j, 1), :] = jnp.sum(own, axis=0, keepdims=True)


def _attn_sample(q, kvn, recent, far, biases):
    t_new, n, qw = q.shape
    step = SAMPLE_SEQ_STEP
    blk = lambda width: pl.BlockSpec((t_new, SAMPLE_ROWS, width),
                                     lambda i: (0, i // (SAMPLE_ROWS // step), 0))
    rows = lambda a: pl.BlockSpec((step,) + a.shape[1:], lambda i: (i, 0, 0))
    biases = [jnp.tile(b, (step, 1)) for b in biases]
    return pl.pallas_call(
        _attn_sample_kernel,
        out_shape=jax.ShapeDtypeStruct((t_new, n, HEADS_W), F32),
        grid=(n // step,),
        in_specs=[blk(qw), blk(KV_W), rows(recent), rows(far)] + [_const_spec(b.shape) for b in biases],
        out_specs=blk(HEADS_W),
        compiler_params=_params(("arbitrary",), 40),
        name="attn_sample",
    )(q, kvn, recent, far, *biases)


def _attn_out_sample_kernel(x_ref, o_ref, wo_ref, g_ref, wup_ref, wdn_ref, out_ref):
    x = x_ref[...] + jnp.dot(o_ref[...].astype(BF16), wo_ref[...], preferred_element_type=F32)
    out_ref[...] = _mlp_residual(x, g_ref, wup_ref, wdn_ref)


def _attn_out_prompt_kernel(x_ref, a0_ref, m0_ref, l0_ref, a1_ref, m1_ref, l1_ref, a2_ref, m2_ref, l2_ref,
                            ex_ref, wo_ref, g_ref, wup_ref, wdn_ref, out_ref, slab_ref):
    rows = x_ref.shape[0]
    n_o = HEADS_W // LANES
    for gi, (a_ref, m_ref, l_ref) in enumerate(((a1_ref, m1_ref, l1_ref), (a2_ref, m2_ref, l2_ref))):
        d = WINDOWS[gi + 1][1]
        for r in range(d):
            dst = pl.ds(r, rows // d, stride=d)
            for s in range(n_o):
                slab_ref[gi, s, dst, :] = a_ref[r, :, _lane_block(s)]
            slab_ref[gi, n_o, dst, :] = m_ref[r]
            slab_ref[gi, n_o + 1, dst, :] = l_ref[r]
    accs = [a0_ref[0]] + [jnp.concatenate([slab_ref[gi, s] for s in range(n_o)], axis=1) for gi in range(2)]
    ms = [m0_ref[0], slab_ref[0, n_o], slab_ref[1, n_o]]
    ls = [l0_ref[0], slab_ref[0, n_o + 1], slab_ref[1, n_o + 1]]
    mx = functools.reduce(jnp.maximum, ms)
    es = [jnp.exp(m - mx) for m in ms]
    inv = 1.0 / functools.reduce(jnp.add, [e * l for e, l in zip(es, ls)])
    o = jnp.zeros((rows, HEADS_W), F32)
    for e, og in zip(es, accs):
        w = e * inv
        hi = w.astype(BF16)
        lo = (w - hi.astype(F32)).astype(BF16)
        wide = (jnp.dot(hi, ex_ref[...], preferred_element_type=F32)
                + jnp.dot(lo, ex_ref[...], preferred_element_type=F32))
        o = o + wide * og
    x = x_ref[...] + jnp.dot(o.astype(BF16), wo_ref[...], preferred_element_type=F32)
    out_ref[...] = _mlp_residual(x, g_ref, wup_ref, wdn_ref)


def _mlp_weight_specs(layer, j, wo, g, wup, wdn):
    return [_layer_spec(wo, j)] + [_layer_spec(a, layer) for a in (g, wup, wdn)]


def _attn_out_sample(x, o, layer, j, wo, g, wup, wdn):
    m = x.shape[0]
    row = pl.BlockSpec((ROW_TILE, D_MODEL), lambda i: (i, 0))
    return pl.pallas_call(
        _attn_out_sample_kernel,
        out_shape=jax.ShapeDtypeStruct((m, D_MODEL), F32),
        grid=(m // ROW_TILE,),
        in_specs=[row, pl.BlockSpec((ROW_TILE, HEADS_W), lambda i: (i, 0))]
        + _mlp_weight_specs(layer, j, wo, g, wup, wdn),
        out_specs=row,
        compiler_params=_params(("arbitrary",), 48),
        name="attn_out_sample",
    )(x, o, wo, g, wup, wdn)


def _attn_out_prompt(x, parts, ex, layer, j, wo, g, wup, wdn):
    n, t, _ = x.shape
    tile = ATT_OUT_TILE
    row = pl.BlockSpec((None, tile, D_MODEL), lambda b, i: (b, i, 0))
    part_specs = []
    for _, d in WINDOWS:
        for width in (HEADS_W, LANES, LANES):
            part_specs.append(pl.BlockSpec((None, d, tile // d, width), lambda b, i: (b, 0, i, 0)))
    flat = [a for part in parts for a in part]
    return pl.pallas_call(
        _attn_out_prompt_kernel,
        out_shape=jax.ShapeDtypeStruct((n, t, D_MODEL), F32),
        grid=(n, t // tile),
        in_specs=[row] + part_specs + [_const_spec((LANES, HEADS_W))]
        + _mlp_weight_specs(layer, j, wo, g, wup, wdn),
        out_specs=row,
        scratch_shapes=[pltpu.VMEM((2, HEADS_W // LANES + 2, tile, LANES), F32)],
        compiler_params=_params(("arbitrary", "arbitrary"), 48),
        name="attn_out_prompt",
    )(x, *flat, ex, wo, g, wup, wdn)


def _alibi_slopes():
    n = N_GROUPS * N_HEADS
    s = 2.0 ** (-8.0 * jnp.arange(1, n + 1, dtype=F32) / n)
    return s.reshape(N_GROUPS, N_HEADS)


def _prompt_bias(slopes, grp):
    d = WINDOWS[grp][1]
    iq = jnp.arange(BAND, dtype=jnp.int32)[:, None]
    ik = jnp.arange(2 * BAND, dtype=jnp.int32)[None, :]
    steps = iq + BAND - ik
    valid = (steps >= 0) & (steps <= BAND)
    offs = (steps * d).astype(F32)
    per_head = jnp.where(valid[None], -slopes[grp][:, None, None] * offs[None], NEG_INF)
    return per_head.reshape(N_HEADS // 2, 2 * BAND, 2 * BAND)


def _sample_bias(slopes, t_new, past):
    d1, d2 = WINDOWS[1][1], WINDOWS[2][1]
    t = jnp.arange(t_new, dtype=jnp.int32)[:, None]

    def table(slope, dist, valid):
        b = jnp.where(valid[:, None, :], -slope[None, :, None] * dist.astype(F32)[:, None, :], NEG_INF)
        return b.reshape(t_new * N_HEADS, -1)

    u = jnp.arange(BAND, dtype=jnp.int32)[None, :]
    dist_n = t - u
    ok_n = (u < t_new) & (dist_n >= 0)
    c = jnp.arange(BAND, dtype=jnp.int32)[None, :]
    dist_c = BAND + t - c
    b0 = table(slopes[0], jnp.concatenate([dist_c, dist_n], axis=1),
               jnp.concatenate([dist_c <= WINDOWS[0][0], ok_n], axis=1))
    c = jnp.arange(d1 * BAND, dtype=jnp.int32)[None, :]
    dist_c = d1 * BAND + t - c
    ok_c = (dist_c % d1 == 0) & (dist_c <= WINDOWS[1][0])
    b1 = table(slopes[1], jnp.concatenate([dist_c, dist_n], axis=1),
               jnp.concatenate([ok_c, ok_n & (dist_n % d1 == 0)], axis=1))
    per = past // d2
    col = jnp.arange(t_new * per, dtype=jnp.int32)[None, :]
    pos = (col % per) * d2 + col // per
    dist_c = past + t - pos
    ok_c = (dist_c % d2 == 0) & (dist_c <= WINDOWS[2][0])
    b2 = table(slopes[2], dist_c, ok_c)
    b2n = table(slopes[2], dist_n, ok_n & (dist_n % d2 == 0))
    return b0, b1, b2, b2n


def _head_mean_matrix():
    idx = jnp.arange(HEADS_W) // HEAD_DIM
    return jnp.where(idx[:, None] == idx[None, :], 1.0 / HEAD_DIM, 0.0).astype(BF16)


def _head_expand_matrix():
    lane = jnp.arange(LANES)[:, None]
    col = jnp.arange(HEADS_W)[None, :] // HEAD_DIM
    return jnp.where(lane == col, 1.0, 0.0).astype(BF16)


def kernel(x_prompt, x_sample, state_conv, cache_kv, norm_mix_g, norm_mlp_g, conv_w_pw1, conv_b_pw1,
           conv_w_dw, conv_b_dw, conv_ln_g, conv_ln_b, conv_w_pw2, conv_b_pw2, kv_norm_g, w_kv,
           k_norm_g, attn_w_q, q_norm_g, attn_w_o, mlp_w_up, mlp_w_down):
    n_p, t_p, _ = x_prompt.shape
    n_s, t_s, _ = x_sample.shape
    past = cache_kv.shape[1]
    scale = HEAD_DIM ** -0.5

    w1 = conv_w_pw1.astype(BF16)
    w2 = conv_w_pw2.astype(BF16)
    wkv = w_kv.astype(BF16)
    wq = attn_w_q.astype(BF16)
    wo = attn_w_o.astype(BF16)
    wup = mlp_w_up.astype(BF16)
    wdn = mlp_w_down.astype(BF16)
    vec = lambda a: a.reshape(-1, 1, a.shape[-1])
    g_mix, g_mlp, g_kv = vec(norm_mix_g), vec(norm_mlp_g), vec(kv_norm_g)
    e_mat = _head_mean_matrix()
    ex_mat = _head_expand_matrix()
    gk = vec(jnp.tile(k_norm_g, N_HEADS))
    gq = vec(jnp.tile(q_norm_g, (1, N_HEADS)) * scale)
    wkv = wkv[None]
    slopes = _alibi_slopes()
    bias_p = [_prompt_bias(slopes, g) for g in range(N_GROUPS)]
    bias_s = _sample_bias(slopes, t_s, past)
    conv_weights = (g_mix, w1, vec(conv_b_pw1), conv_w_dw, vec(conv_b_dw), vec(conv_ln_g),
                    vec(conv_ln_b), w2, vec(conv_b_pw2))

    xp = x_prompt
    conv_p = []
    zero_hist = jnp.zeros((n_p, HIST_PAD, CONV_CH), F32)
    for layer in range(N_A_LAYERS):
        xp, st = _conv_prompt(xp, zero_hist, layer, conv_weights)
        conv_p.append(st[:, HIST_PAD - CONV_HIST:])
        xp = _mlp(xp.reshape(n_p * t_p, D_MODEL), layer, g_mlp, wup, wdn).reshape(n_p, t_p, D_MODEL)
    win = min(WINDOWS[-1][0], t_p)
    kv_p, *kvb_p = _kv_proj_prompt(xp, g_kv, wkv, e_mat, gk, win)
    for j in range(N_B_LAYERS):
        layer = N_A_LAYERS + j
        qs = _q_proj_prompt(xp, layer, j, g_mix, wq, e_mat, gq)
        parts = [_attn_group(qs[g], kvb_p[g], bias_p[g], g) for g in range(N_GROUPS)]
        xp = _attn_out_prompt(xp, parts, ex_mat, layer, j, wo, g_mlp, wup, wdn)
    y_prompt = xp
    conv_prompt = jnp.stack(conv_p, axis=0)
    kv_prompt = kv_p.reshape(n_p, win, 2, N_HEADS, HEAD_DIM)

    xs = jnp.transpose(x_sample, (1, 0, 2))
    st_in = jnp.transpose(state_conv, (0, 2, 1, 3))
    st_out = None
    for layer in range(N_A_LAYERS):
        xs, st_out = _conv_sample(xs, st_in, layer, conv_weights, st_out)
        xs = _mlp(xs.reshape(t_s * n_s, D_MODEL), layer, g_mlp, wup, wdn).reshape(t_s, n_s, D_MODEL)
    xs = xs.reshape(t_s * n_s, D_MODEL)
    kvn = _kv_proj_sample(xs, g_kv, wkv, e_mat, gk).reshape(t_s, n_s, KV_W)
    recent, far = _compact_cache(cache_kv, t_s)
    for j in range(N_B_LAYERS):
        layer = N_A_LAYERS + j
        q = _q_proj_sample(xs, layer, j, g_mix, wq, e_mat, gq)
        o = _attn_sample(q.reshape(t_s, n_s, N_GROUPS * HEADS_W), kvn, recent, far, bias_s)
        xs = _attn_out_sample(xs, o.reshape(t_s * n_s, HEADS_W), layer, j, wo, g_mlp, wup, wdn)
    y_sample = jnp.transpose(xs.reshape(t_s, n_s, D_MODEL), (1, 0, 2))
    conv_sample = jnp.transpose(st_out, (0, 2, 1, 3))
    kv_sample = jnp.transpose(kvn, (1, 0, 2)).reshape(n_s, t_s, 2, N_HEADS, HEAD_DIM)

    return (y_prompt, y_sample, conv_prompt, conv_sample, kv_prompt, kv_sample)
```

```python
import functools

import jax
import jax.numpy as jnp
from jax import lax
from jax.experimental import pallas as pl
from jax.experimental.pallas import tpu as pltpu

F32 = jnp.float32
BF16 = jnp.bfloat16

D_MODEL = 1024
CONV_CH = 1024
CONV_WIDTH = 31
CONV_HIST = CONV_WIDTH - 1
HIST_PAD = 32
D_FF = 4096
N_HEADS = 8
HEAD_DIM = 64
HEADS_W = N_HEADS * HEAD_DIM
KV_W = 2 * HEADS_W
WINDOWS = ((128, 1), (512, 4), (2048, 16))
N_GROUPS = len(WINDOWS)
BAND = 128
N_A_LAYERS = 2
N_B_LAYERS = 2
NORM_EPS = 1e-6
NEG_INF = float("-inf")
LANES = 128
SUBLANES = 8

ROW_TILE = 512
ATT_OUT_TILE = 512
FF_CHUNK = 1024
ATT_QB = 1024
SAMPLE_CONV_NB = 32
SAMPLE_ROWS = 8
SAMPLE_SEQ_STEP = 4
MXU_DIM = 256
COMPACT_VMEM_MIB = 24
MIB = 1024 * 1024

assert all(w // d == BAND for w, d in WINDOWS)


def _const_spec(shape):
    nd = len(shape)
    return pl.BlockSpec(shape, lambda *_: (0,) * nd, pipeline_mode=pl.Buffered(1))


def _layer_spec(stacked, layer):
    return pl.BlockSpec((None,) + stacked.shape[1:], lambda *_: (layer, 0, 0),
                        pipeline_mode=pl.Buffered(1))


def _params(semantics, vmem_mib):
    return pltpu.CompilerParams(dimension_semantics=semantics, vmem_limit_bytes=vmem_mib * MIB)


def _lane_block(s, width=LANES):
    return slice(s * width, (s + 1) * width)


def _rms(x, g):
    return x * lax.rsqrt(jnp.mean(x * x, axis=-1, keepdims=True) + NORM_EPS) * g


def _mlp_residual(x, g_ref, wup_ref, wdn_ref):
    h = _rms(x, g_ref[...]).astype(BF16)
    acc = x
    for c in range(D_FF // FF_CHUNK):
        sl = _lane_block(c, FF_CHUNK)
        z = jnp.maximum(jnp.dot(h, wup_ref[:, sl], preferred_element_type=F32), 0.0)
        acc = acc + jnp.dot((z * z).astype(BF16), wdn_ref[sl, :], preferred_element_type=F32)
    return acc


def _head_mean_sq(v, e_ref):
    return jnp.dot((v * v).astype(BF16), e_ref[...], preferred_element_type=F32)


def _dot_t(a, b):
    return lax.dot_general(a, b, (((1,), (1,)), ((), ())), preferred_element_type=F32)


def _deinterleave_store(val, slab_ref, out_ref, d):
    rows, width = val.shape
    for s in range(width // LANES):
        slab_ref[s] = val[:, _lane_block(s)]
    for s in range(width // LANES):
        for r in range(d):
            out_ref[r, :, _lane_block(s)] = slab_ref[s, pl.ds(r, rows // d, stride=d), :].astype(BF16)


def _mlp_kernel(x_ref, g_ref, wup_ref, wdn_ref, o_ref):
    o_ref[...] = _mlp_residual(x_ref[...], g_ref, wup_ref, wdn_ref)


def _mlp(x, layer, g, wup, wdn, job=None):
    m = x.shape[0]
    row = pl.BlockSpec((ROW_TILE, D_MODEL), lambda i: (i, 0))
    return _call_with_compaction(
        _mlp_kernel, job, m // ROW_TILE, lambda i: i,
        out_shape=[jax.ShapeDtypeStruct((m, D_MODEL), F32)],
        grid=(m // ROW_TILE,),
        in_specs=[row] + [_layer_spec(a, layer) for a in (g, wup, wdn)],
        out_specs=[row],
        args=(x, g, wup, wdn),
        semantics=("arbitrary",), vmem_mib=34, name="mlp",
    )[0]


def _glu(h, w1_ref, b1_ref):
    u = jnp.dot(h, w1_ref[...], preferred_element_type=F32) + b1_ref[...]
    return u[:, :CONV_CH] * jax.nn.sigmoid(u[:, CONV_CH:])


def _conv_tail(c, x, bdw_ref, lng_ref, lnb_ref, w2_ref, b2_ref):
    c = c + bdw_ref[...]
    mu = jnp.mean(c, axis=-1, keepdims=True)
    xc = c - mu
    y = xc * lax.rsqrt(jnp.mean(xc * xc, axis=-1, keepdims=True) + NORM_EPS)
    y = y * lng_ref[...] + lnb_ref[...]
    y = y * jax.nn.sigmoid(y)
    out = jnp.dot(y.astype(BF16), w2_ref[...], preferred_element_type=F32) + b2_ref[...]
    return x + out


def _conv_prompt_kernel(x_ref, st_ref, g_ref, w1_ref, b1_ref, wdw_ref, bdw_ref, lng_ref, lnb_ref,
                        w2_ref, b2_ref, o_ref, sto_ref, buf_ref):
    n_slabs = CONV_CH // LANES

    @pl.when(pl.program_id(1) == 0)
    def _():
        for s in range(n_slabs):
            buf_ref[s, 0:HIST_PAD, :] = st_ref[:, _lane_block(s)]

    x = x_ref[...]
    h = _rms(x, g_ref[...]).astype(BF16)
    u = _glu(h, w1_ref, b1_ref)
    for s in range(n_slabs):
        buf_ref[s, HIST_PAD:HIST_PAD + ROW_TILE, :] = u[:, _lane_block(s)]
    off = HIST_PAD - CONV_HIST
    cs = []
    for s in range(n_slabs):
        ls = _lane_block(s)
        c = buf_ref[s, off:off + ROW_TILE, :] * wdw_ref[0:1, ls]
        for k in range(1, CONV_WIDTH):
            c = c + buf_ref[s, off + k:off + k + ROW_TILE, :] * wdw_ref[k:k + 1, ls]
        cs.append(c)
    c = jnp.concatenate(cs, axis=1)
    o_ref[...] = _conv_tail(c, x, bdw_ref, lng_ref, lnb_ref, w2_ref, b2_ref)
    for s in range(n_slabs):
        last = buf_ref[s, ROW_TILE:ROW_TILE + HIST_PAD, :]
        sto_ref[:, _lane_block(s)] = last
        buf_ref[s, 0:HIST_PAD, :] = last


def _conv_prompt(x, st, layer, weights, job=None):
    n, t, _ = x.shape
    tiles = t // ROW_TILE
    row = pl.BlockSpec((None, ROW_TILE, D_MODEL), lambda b, i: (b, i, 0))
    st_spec = pl.BlockSpec((None, HIST_PAD, CONV_CH), lambda b, i: (b, 0, 0))
    return _call_with_compaction(
        _conv_prompt_kernel, job, n * tiles, lambda b, i: b * tiles + i,
        out_shape=[jax.ShapeDtypeStruct((n, t, D_MODEL), F32),
                   jax.ShapeDtypeStruct((n, HIST_PAD, CONV_CH), F32)],
        grid=(n, tiles),
        in_specs=[row, st_spec] + [_layer_spec(a, layer) for a in weights],
        out_specs=[row, st_spec],
        scratch_shapes=[pltpu.VMEM((CONV_CH // LANES, HIST_PAD + ROW_TILE, LANES), F32)],
        args=(x, st, *weights),
        semantics=("arbitrary", "arbitrary"), vmem_mib=32, name="conv_prompt",
    )


def _conv_sample_kernel(x_ref, st_ref, g_ref, w1_ref, b1_ref, wdw_ref, bdw_ref, lng_ref, lnb_ref,
                        w2_ref, b2_ref, o_ref, sto_ref):
    t_new, nb, _ = x_ref.shape
    x = x_ref[...].reshape(t_new * nb, D_MODEL)
    h = _rms(x, g_ref[...]).astype(BF16)
    glu = _glu(h, w1_ref, b1_ref)
    new = [glu[t * nb:(t + 1) * nb, :] for t in range(t_new)]

    def full(j):
        return st_ref[j] if j < CONV_HIST else new[j - CONV_HIST]

    outs = []
    for t in range(t_new):
        c = full(t) * wdw_ref[0:1, :]
        for k in range(1, CONV_WIDTH):
            c = c + full(t + k) * wdw_ref[k:k + 1, :]
        outs.append(c)
    c = jnp.concatenate(outs, axis=0)
    o_ref[...] = _conv_tail(c, x, bdw_ref, lng_ref, lnb_ref, w2_ref, b2_ref).reshape(t_new, nb, D_MODEL)
    for j in range(CONV_HIST):
        sto_ref[j] = full(j + t_new)


def _conv_sample_chained_kernel(*refs):
    _conv_sample_kernel(*refs[:11], *refs[12:])


def _conv_sample(x, st_all, layer, weights, st_out=None):
    t_new, n, _ = x.shape
    nb = SAMPLE_CONV_NB
    x_spec = pl.BlockSpec((t_new, nb, D_MODEL), lambda i: (0, i, 0))
    st_spec = pl.BlockSpec((None, CONV_HIST, nb, CONV_CH), lambda i: (layer, 0, i, 0))
    in_specs = [x_spec, st_spec] + [_layer_spec(a, layer) for a in weights]
    args = [x, st_all, *weights]
    if st_out is None:
        body, aliases = _conv_sample_kernel, {}
    else:
        body, aliases = _conv_sample_chained_kernel, {len(args): 1}
        in_specs.append(pl.BlockSpec(memory_space=pl.ANY))
        args.append(st_out)
    return pl.pallas_call(
        body,
        out_shape=(jax.ShapeDtypeStruct((t_new, n, D_MODEL), F32),
                   jax.ShapeDtypeStruct(st_all.shape, F32)),
        grid=(n // nb,),
        in_specs=in_specs,
        out_specs=(x_spec, st_spec),
        input_output_aliases=aliases,
        compiler_params=_params(("arbitrary",), 48),
        name="conv_sample",
    )(*args)


def _kv_rows(x_ref, g_ref, wkv_ref, e_ref, gk_ref):
    h = _rms(x_ref[...], g_ref[...]).astype(BF16)
    k = jnp.dot(h, wkv_ref[:, :HEADS_W], preferred_element_type=F32)
    v = jnp.dot(h, wkv_ref[:, HEADS_W:], preferred_element_type=F32)
    k = k * lax.rsqrt(_head_mean_sq(k, e_ref) + NORM_EPS) * gk_ref[...]
    return jnp.concatenate([k, v], axis=1)


def _kv_sample_kernel(x_ref, g_ref, wkv_ref, e_ref, gk_ref, kv_ref):
    kv_ref[...] = _kv_rows(x_ref, g_ref, wkv_ref, e_ref, gk_ref)


def _kv_prompt_kernel(x_ref, g_ref, wkv_ref, e_ref, gk_ref, kv_ref, kb0_ref, kb1_ref, kb2_ref, slab_ref):
    kv = _kv_rows(x_ref, g_ref, wkv_ref, e_ref, gk_ref)
    kv_ref[...] = kv
    kb0_ref[0] = kv.astype(BF16)
    _deinterleave_store(kv, slab_ref, kb1_ref, WINDOWS[1][1])
    _deinterleave_store(kv, slab_ref, kb2_ref, WINDOWS[2][1])


def _proj_weight_specs(layer, g, w, e, gh):
    return [_layer_spec(g, layer), _layer_spec(w, layer), _const_spec(e.shape), _layer_spec(gh, layer)]


def _kv_proj_sample(x, g, wkv, e, gk):
    m = x.shape[0]
    return pl.pallas_call(
        _kv_sample_kernel,
        out_shape=jax.ShapeDtypeStruct((m, KV_W), F32),
        grid=(m // ROW_TILE,),
        in_specs=[pl.BlockSpec((ROW_TILE, D_MODEL), lambda i: (i, 0))] + _proj_weight_specs(0, g, wkv, e, gk),
        out_specs=pl.BlockSpec((ROW_TILE, KV_W), lambda i: (i, 0)),
        compiler_params=_params(("arbitrary",), 32),
        name="kv_proj_sample",
    )(x, g, wkv, e, gk)


def _dilated_shapes(n, t, width, dtype):
    sds, specs = [], []
    for _, d in WINDOWS:
        sds.append(jax.ShapeDtypeStruct((n, d, t // d, width), dtype))
        specs.append(pl.BlockSpec((None, d, ROW_TILE // d, width), lambda b, i: (b, 0, i, 0)))
    return sds, specs


def _kv_proj_prompt(x, g, wkv, e, gk, win):
    n, t, _ = x.shape
    sds, specs = _dilated_shapes(n, t, KV_W, BF16)
    skip = (t - win) // ROW_TILE
    assert win % ROW_TILE == 0 and t % ROW_TILE == 0
    f32_spec = pl.BlockSpec((None, ROW_TILE, KV_W), lambda b, i: (b, jnp.maximum(i - skip, 0), 0))
    return pl.pallas_call(
        _kv_prompt_kernel,
        out_shape=[jax.ShapeDtypeStruct((n, win, KV_W), F32)] + sds,
        grid=(n, t // ROW_TILE),
        in_specs=[pl.BlockSpec((None, ROW_TILE, D_MODEL), lambda b, i: (b, i, 0))]
        + _proj_weight_specs(0, g, wkv, e, gk),
        out_specs=[f32_spec] + specs,
        scratch_shapes=[pltpu.VMEM((KV_W // LANES, ROW_TILE, LANES), F32)],
        compiler_params=_params(("arbitrary", "arbitrary"), 40),
        name="kv_proj_prompt",
    )(x, g, wkv, e, gk)


def _q_group(h, wq_ref, e_ref, gq_ref, grp):
    q = jnp.dot(h, wq_ref[:, _lane_block(grp, HEADS_W)], preferred_element_type=F32)
    return q * lax.rsqrt(_head_mean_sq(q, e_ref) + NORM_EPS) * gq_ref[...]


def _q_sample_kernel(x_ref, g_ref, wq_ref, e_ref, gq_ref, q_ref):
    h = _rms(x_ref[...], g_ref[...]).astype(BF16)
    for grp in range(N_GROUPS):
        q = _q_group(h, wq_ref, e_ref, gq_ref, grp)
        q_ref[:, _lane_block(grp, HEADS_W)] = q.astype(BF16).astype(F32)


def _q_prompt_kernel(x_ref, g_ref, wq_ref, e_ref, gq_ref, q0_ref, q1_ref, q2_ref, slab_ref):
    h = _rms(x_ref[...], g_ref[...]).astype(BF16)
    q0_ref[0] = _q_group(h, wq_ref, e_ref, gq_ref, 0).astype(BF16)
    _deinterleave_store(_q_group(h, wq_ref, e_ref, gq_ref, 1), slab_ref, q1_ref, WINDOWS[1][1])
    _deinterleave_store(_q_group(h, wq_ref, e_ref, gq_ref, 2), slab_ref, q2_ref, WINDOWS[2][1])


def _q_proj_sample(x, layer, j, g, wq, e, gq):
    m = x.shape[0]
    qw = N_GROUPS * HEADS_W
    specs = [_layer_spec(g, layer), _layer_spec(wq, j), _const_spec(e.shape), _layer_spec(gq, j)]
    return pl.pallas_call(
        _q_sample_kernel,
        out_shape=jax.ShapeDtypeStruct((m, qw), F32),
        grid=(m // ROW_TILE,),
        in_specs=[pl.BlockSpec((ROW_TILE, D_MODEL), lambda i: (i, 0))] + specs,
        out_specs=pl.BlockSpec((ROW_TILE, qw), lambda i: (i, 0)),
        compiler_params=_params(("arbitrary",), 32),
        name="q_proj_sample",
    )(x, g, wq, e, gq)


def _q_proj_prompt(x, layer, j, g, wq, e, gq):
    n, t, _ = x.shape
    sds, specs = _dilated_shapes(n, t, HEADS_W, BF16)
    w_specs = [_layer_spec(g, layer), _layer_spec(wq, j), _const_spec(e.shape), _layer_spec(gq, j)]
    return pl.pallas_call(
        _q_prompt_kernel,
        out_shape=sds,
        grid=(n, t // ROW_TILE),
        in_specs=[pl.BlockSpec((None, ROW_TILE, D_MODEL), lambda b, i: (b, i, 0))] + w_specs,
        out_specs=specs,
        scratch_shapes=[pltpu.VMEM((HEADS_W // LANES, ROW_TILE, LANES), F32)],
        compiler_params=_params(("arbitrary", "arbitrary"), 32),
        name="q_proj_prompt",
    )(x, g, wq, e, gq)


def _attn_group_kernel(q_ref, kc_ref, kp_ref, vc_ref, vp_ref, bias_ref, acc_ref, m_ref, l_ref,
                       kk_ref, vv_ref):
    kk_ref[0:BAND, :] = kp_ref[...]
    kk_ref[BAND:, :] = kc_ref[...]
    vv_ref[0:BAND, :] = vp_ref[...]
    vv_ref[BAND:, :] = vc_ref[...]
    first_block = pl.program_id(2) == 0
    key_col = lax.broadcasted_iota(jnp.int32, (2 * BAND, 2 * BAND), 1)
    lane = lax.broadcasted_iota(jnp.int32, (BAND, LANES), 1)
    left = lane < HEAD_DIM
    lane_row = lax.broadcasted_iota(jnp.int32, (1, LANES), 1)
    head_keep = [jnp.where(lane_row < HEAD_DIM, 1.0, 0.0).astype(BF16),
                 jnp.where(lane_row < HEAD_DIM, 0.0, 1.0).astype(BF16)]
    no_history = jnp.logical_and(first_block, key_col < BAND)

    for sb in range(q_ref.shape[0] // BAND):
        r0 = sb * BAND
        m_tile = jnp.zeros((BAND, LANES), F32)
        l_tile = jnp.ones((BAND, LANES), F32)
        for pair in range(N_HEADS // 2):
            ls = _lane_block(pair)
            q2 = q_ref[r0:r0 + BAND, ls]
            k2 = kk_ref[r0:r0 + 2 * BAND, ls]
            v2 = vv_ref[r0:r0 + 2 * BAND, ls]
            qq = jnp.concatenate([q2 * head_keep[0], q2 * head_keep[1]], axis=0)
            s = _dot_t(qq, k2) + bias_ref[pair]
            if sb == 0:
                s = jnp.where(no_history, NEG_INF, s)
            m = jnp.max(s, axis=-1, keepdims=True)
            p = jnp.exp(s - m)
            l = jnp.sum(p, axis=-1, keepdims=True)
            a = jnp.dot(p.astype(BF16), v2, preferred_element_type=F32)
            acc_ref[r0:r0 + BAND, ls] = jnp.where(left, a[:BAND], a[BAND:])
            for hh in range(2):
                rows = slice(hh * BAND, (hh + 1) * BAND)
                m_tile = jnp.where(lane == 2 * pair + hh, m[rows], m_tile)
                l_tile = jnp.where(lane == 2 * pair + hh, l[rows], l_tile)
        m_ref[r0:r0 + BAND, :] = m_tile
        l_ref[r0:r0 + BAND, :] = l_tile


def _attn_group(q, kvb, bias, grp):
    n, d, rows, _ = q.shape
    qb = min(ATT_QB, rows)
    per_blk = qb // BAND
    stat_sds = jax.ShapeDtypeStruct((n, d, rows, LANES), F32)
    stat_spec = pl.BlockSpec((None, None, qb, LANES), lambda b, r, i: (b, r, i, 0))

    def prev_idx(i):
        return jnp.maximum(i * per_blk - 1, 0)

    def cur(col):
        return pl.BlockSpec((None, None, qb, HEADS_W), lambda b, r, i: (b, r, i, col))

    def prev(col):
        return pl.BlockSpec((None, None, BAND, HEADS_W), lambda b, r, i: (b, r, prev_idx(i), col))

    return pl.pallas_call(
        _attn_group_kernel,
        out_shape=(jax.ShapeDtypeStruct((n, d, rows, HEADS_W), F32), stat_sds, stat_sds),
        grid=(n, d, rows // qb),
        in_specs=[cur(0), cur(0), prev(0), cur(1), prev(1),
                  _const_spec((N_HEADS // 2, 2 * BAND, 2 * BAND))],
        out_specs=(cur(0), stat_spec, stat_spec),
        scratch_shapes=[pltpu.VMEM((BAND + qb, HEADS_W), BF16),
                        pltpu.VMEM((BAND + qb, HEADS_W), BF16)],
        compiler_params=_params(("arbitrary", "arbitrary", "arbitrary"), 32),
        name=f"attn_group{grp}",
    )(q, kvb, kvb, kvb, kvb, bias)


def _compact_cache_kernel(c_ref, eye_ref, sel_ref, recent_ref, far_ref, *, t_new):
    p = c_ref.shape[-1]
    d1, d2 = WINDOWS[1][1], WINDOWS[2][1]
    x = c_ref[...].reshape(KV_W, p).astype(BF16)
    lo = p - d1 * BAND
    for j in range(d1 * BAND // MXU_DIM):
        cols = x[:, lo + j * MXU_DIM:lo + (j + 1) * MXU_DIM]
        recent_ref[j * MXU_DIM:(j + 1) * MXU_DIM, :] = _dot_t(eye_ref[...], cols).astype(BF16)
    per = MXU_DIM // d2
    for c in range(p // MXU_DIM):
        picked = _dot_t(sel_ref[...], x[:, c * MXU_DIM:(c + 1) * MXU_DIM]).astype(BF16)
        for t in range(t_new):
            far_ref[t * BAND + c * per:t * BAND + (c + 1) * per, :] = picked[t * per:(t + 1) * per, :]


def _compaction_job(cache_kv, t_new):
    n, p = cache_kv.shape[:2]
    d1, d2 = WINDOWS[1][1], WINDOWS[2][1]
    per = MXU_DIM // d2
    assert p == WINDOWS[2][0] and p // d2 == BAND and t_new <= d1 and p % MXU_DIM == 0
    feat_major = jnp.transpose(cache_kv, (0, 2, 3, 4, 1))
    eye = jnp.eye(MXU_DIM, dtype=BF16)
    row = jnp.arange(t_new * per)[:, None]
    col = jnp.arange(MXU_DIM)[None, :]
    sel = jnp.where(col == (row % per) * d2 + row // per, 1.0, 0.0).astype(BF16)
    return dict(inputs=(feat_major, eye, sel), t_new=t_new, n=n, next=0, outs=None)


def _fused_body(*refs, body, n_in, n_extra_in, n_out, t_new):
    c_ref, eye_ref, sel_ref = refs[n_in:n_in + 3]
    rest = refs[n_in + n_extra_in:]
    recent_ref, far_ref = rest[n_out:n_out + 2]
    body(*refs[:n_in], *rest[:n_out], *rest[n_out + 2:])
    _compact_cache_kernel(c_ref, eye_ref, sel_ref, recent_ref, far_ref, t_new=t_new)


def _call_with_compaction(body, job, steps, step_of, *, grid, in_specs, out_specs, out_shape, args,
                          semantics, vmem_mib, name, scratch_shapes=()):
    if job is None:
        return pl.pallas_call(body, out_shape=out_shape, grid=grid, in_specs=in_specs, out_specs=out_specs,
                              scratch_shapes=scratch_shapes, compiler_params=_params(semantics, vmem_mib),
                              name=name)(*args)
    base, n, t_new = job["next"], job["n"], job["t_new"]
    assert base + steps <= n
    job["next"] = base + steps
    feat, eye, sel = job["inputs"]
    seq = lambda *idx: base + step_of(*idx)
    extra_specs = [pl.BlockSpec((None,) + feat.shape[1:], lambda *idx: (seq(*idx), 0, 0, 0, 0)),
                   _const_spec(eye.shape), _const_spec(sel.shape)]
    extra_args = [feat, eye, sel]
    n_out = len(out_shape)
    aliases = {}
    for k, prev in enumerate(job["outs"] or ()):
        aliases[len(args) + len(extra_args)] = n_out + k
        extra_specs.append(pl.BlockSpec(memory_space=pl.ANY))
        extra_args.append(prev)
    rows = (WINDOWS[1][1] * BAND, t_new * BAND)
    res = pl.pallas_call(
        functools.partial(_fused_body, body=body, n_in=len(args), n_extra_in=len(extra_args),
                          n_out=n_out, t_new=t_new),
        out_shape=list(out_shape) + [jax.ShapeDtypeStruct((n, r, KV_W), BF16) for r in rows],
        grid=grid,
        in_specs=list(in_specs) + extra_specs,
        out_specs=list(out_specs) + [pl.BlockSpec((None, r, KV_W), lambda *idx: (seq(*idx), 0, 0)) for r in rows],
        scratch_shapes=scratch_shapes,
        input_output_aliases=aliases,
        compiler_params=_params(semantics, vmem_mib + COMPACT_VMEM_MIB),
        name=name,
    )(*args, *extra_args)
    job["outs"] = res[n_out:]
    return res[:n_out]


def _attn_sample_kernel(q_ref, kvn_ref, recent_ref, far_ref, b0_ref, b1_ref, b2_ref, b2n_ref, o_ref):
    t_new = q_ref.shape[0]
    n_seq = SAMPLE_SEQ_STEP
    base = (pl.program_id(0) % (SAMPLE_ROWS // n_seq)) * n_seq
    n_rows = t_new * N_HEADS
    hsel = (lax.broadcasted_iota(jnp.int32, (N_HEADS, HEADS_W), 1) // HEAD_DIM
            == lax.broadcasted_iota(jnp.int32, (N_HEADS, HEADS_W), 0))
    row8 = lax.broadcasted_iota(jnp.int32, (SUBLANES, KV_W), 0)

    qs, kas, vas, kbs, vbs = [], [], [], [], []
    for j in range(n_seq):
        b = base + j
        qs.append([jnp.concatenate(
            [jnp.where(hsel, q_ref[t, pl.ds(b, 1), _lane_block(grp, HEADS_W)], 0.0) for t in range(t_new)],
            axis=0).astype(BF16) for grp in range(N_GROUPS)])
        new8 = jnp.zeros((SUBLANES, KV_W), F32)
        for t in range(t_new):
            new8 = jnp.where(row8 == t, kvn_ref[t, pl.ds(b, 1), :], new8)
        new = jnp.concatenate([new8, jnp.zeros((BAND - SUBLANES, KV_W), F32)], axis=0).astype(BF16)
        recent = recent_ref[j]
        kas.append(jnp.concatenate([recent[:, :HEADS_W], new[:, :HEADS_W]], axis=0))
        vas.append(jnp.concatenate([recent[:, HEADS_W:], new[:, HEADS_W:]], axis=0))
        kbs.append(far_ref[j, :, :HEADS_W])
        vbs.append(far_ref[j, :, HEADS_W:])
    n_a = kas[0].shape[0]

    def scores(grp, keys, lo, bias_ref):
        return jnp.concatenate([_dot_t(qs[j][grp], keys[j][lo:]) for j in range(n_seq)],
                               axis=0) + bias_ref[...]

    def softmax_pv(pieces):
        m = functools.reduce(jnp.maximum, [jnp.max(s, axis=-1, keepdims=True) for s, _, _ in pieces])
        l = jnp.zeros((n_seq * n_rows, 1), F32)
        acc = jnp.zeros((n_seq * n_rows, HEADS_W), F32)
        for s, vals, lo in pieces:
            p = jnp.exp(s - m)
            l = l + jnp.sum(p, axis=-1, keepdims=True)
            pb = p.astype(BF16)
            acc = acc + jnp.concatenate(
                [jnp.dot(pb[j * n_rows:(j + 1) * n_rows], vals[j][lo:], preferred_element_type=F32)
                 for j in range(n_seq)], axis=0)
        return acc, m, l

    lo0, lo_new = n_a - 2 * BAND, n_a - BAND
    parts = [
        softmax_pv([(scores(0, kas, lo0, b0_ref), vas, lo0)]),
        softmax_pv([(scores(1, kas, 0, b1_ref), vas, 0)]),
        softmax_pv([(scores(2, kbs, 0, b2_ref), vbs, 0), (scores(2, kas, lo_new, b2n_ref), vas, lo_new)]),
    ]
    m_all = functools.reduce(jnp.maximum, [m for _, m, _ in parts])
    num = jnp.zeros((n_seq * n_rows, HEADS_W), F32)
    den = jnp.zeros((n_seq * n_rows, 1), F32)
    for acc, m, l in parts:
        e = jnp.exp(m - m_all)
        num = num + e * acc
        den = den + e * l
    out = num * (1.0 / den)
    for j in range(n_seq):
        for t in range(t_new):
            r0 = j * n_rows + t * N_HEADS
            own = jnp.where(hsel, out[r0:r0 + N_HEADS, :], 0.0)
            o_ref[t, pl.ds(base + j, 1), :] = jnp.sum(own, axis=0, keepdims=True)


def _attn_sample(q, kvn, recent, far, biases):
    t_new, n, qw = q.shape
    step = SAMPLE_SEQ_STEP
    blk = lambda width: pl.BlockSpec((t_new, SAMPLE_ROWS, width),
                                     lambda i: (0, i // (SAMPLE_ROWS // step), 0))
    rows = lambda a: pl.BlockSpec((step,) + a.shape[1:], lambda i: (i, 0, 0))
    biases = [jnp.tile(b, (step, 1)) for b in biases]
    return pl.pallas_call(
        _attn_sample_kernel,
        out_shape=jax.ShapeDtypeStruct((t_new, n, HEADS_W), F32),
        grid=(n // step,),
        in_specs=[blk(qw), blk(KV_W), rows(recent), rows(far)] + [_const_spec(b.shape) for b in biases],
        out_specs=blk(HEADS_W),
        compiler_params=_params(("arbitrary",), 40),
        name="attn_sample",
    )(q, kvn, recent, far, *biases)


def _attn_out_sample_kernel(x_ref, o_ref, wo_ref, g_ref, wup_ref, wdn_ref, out_ref):
    x = x_ref[...] + jnp.dot(o_ref[...].astype(BF16), wo_ref[...], preferred_element_type=F32)
    out_ref[...] = _mlp_residual(x, g_ref, wup_ref, wdn_ref)


def _attn_out_prompt_kernel(x_ref, a0_ref, m0_ref, l0_ref, a1_ref, m1_ref, l1_ref, a2_ref, m2_ref, l2_ref,
                            ex_ref, wo_ref, g_ref, wup_ref, wdn_ref, out_ref, slab_ref):
    rows = x_ref.shape[0]
    n_o = HEADS_W // LANES
    for gi, (a_ref, m_ref, l_ref) in enumerate(((a1_ref, m1_ref, l1_ref), (a2_ref, m2_ref, l2_ref))):
        d = WINDOWS[gi + 1][1]
        for r in range(d):
            dst = pl.ds(r, rows // d, stride=d)
            for s in range(n_o):
                slab_ref[gi, s, dst, :] = a_ref[r, :, _lane_block(s)]
            slab_ref[gi, n_o, dst, :] = m_ref[r]
            slab_ref[gi, n_o + 1, dst, :] = l_ref[r]
    accs = [a0_ref[0]] + [jnp.concatenate([slab_ref[gi, s] for s in range(n_o)], axis=1) for gi in range(2)]
    ms = [m0_ref[0], slab_ref[0, n_o], slab_ref[1, n_o]]
    ls = [l0_ref[0], slab_ref[0, n_o + 1], slab_ref[1, n_o + 1]]
    mx = functools.reduce(jnp.maximum, ms)
    es = [jnp.exp(m - mx) for m in ms]
    inv = 1.0 / functools.reduce(jnp.add, [e * l for e, l in zip(es, ls)])
    o = jnp.zeros((rows, HEADS_W), F32)
    for e, og in zip(es, accs):
        w = e * inv
        hi = w.astype(BF16)
        lo = (w - hi.astype(F32)).astype(BF16)
        wide = (jnp.dot(hi, ex_ref[...], preferred_element_type=F32)
                + jnp.dot(lo, ex_ref[...], preferred_element_type=F32))
        o = o + wide * og
    x = x_ref[...] + jnp.dot(o.astype(BF16), wo_ref[...], preferred_element_type=F32)
    out_ref[...] = _mlp_residual(x, g_ref, wup_ref, wdn_ref)


def _mlp_weight_specs(layer, j, wo, g, wup, wdn):
    return [_layer_spec(wo, j)] + [_layer_spec(a, layer) for a in (g, wup, wdn)]


def _attn_out_sample(x, o, layer, j, wo, g, wup, wdn):
    m = x.shape[0]
    row = pl.BlockSpec((ROW_TILE, D_MODEL), lambda i: (i, 0))
    return pl.pallas_call(
        _attn_out_sample_kernel,
        out_shape=jax.ShapeDtypeStruct((m, D_MODEL), F32),
        grid=(m // ROW_TILE,),
        in_specs=[row, pl.BlockSpec((ROW_TILE, HEADS_W), lambda i: (i, 0))]
        + _mlp_weight_specs(layer, j, wo, g, wup, wdn),
        out_specs=row,
        compiler_params=_params(("arbitrary",), 48),
        name="attn_out_sample",
    )(x, o, wo, g, wup, wdn)


def _attn_out_prompt(x, parts, ex, layer, j, wo, g, wup, wdn):
    n, t, _ = x.shape
    tile = ATT_OUT_TILE
    row = pl.BlockSpec((None, tile, D_MODEL), lambda b, i: (b, i, 0))
    part_specs = []
    for _, d in WINDOWS:
        for width in (HEADS_W, LANES, LANES):
            part_specs.append(pl.BlockSpec((None, d, tile // d, width), lambda b, i: (b, 0, i, 0)))
    flat = [a for part in parts for a in part]
    return pl.pallas_call(
        _attn_out_prompt_kernel,
        out_shape=jax.ShapeDtypeStruct((n, t, D_MODEL), F32),
        grid=(n, t // tile),
        in_specs=[row] + part_specs + [_const_spec((LANES, HEADS_W))]
        + _mlp_weight_specs(layer, j, wo, g, wup, wdn),
        out_specs=row,
        scratch_shapes=[pltpu.VMEM((2, HEADS_W // LANES + 2, tile, LANES), F32)],
        compiler_params=_params(("arbitrary", "arbitrary"), 48),
        name="attn_out_prompt",
    )(x, *flat, ex, wo, g, wup, wdn)


def _alibi_slopes():
    n = N_GROUPS * N_HEADS
    s = 2.0 ** (-8.0 * jnp.arange(1, n + 1, dtype=F32) / n)
    return s.reshape(N_GROUPS, N_HEADS)


def _prompt_bias(slopes, grp):
    d = WINDOWS[grp][1]
    iq = jnp.arange(BAND, dtype=jnp.int32)[:, None]
    ik = jnp.arange(2 * BAND, dtype=jnp.int32)[None, :]
    steps = iq + BAND - ik
    valid = (steps >= 0) & (steps <= BAND)
    offs = (steps * d).astype(F32)
    per_head = jnp.where(valid[None], -slopes[grp][:, None, None] * offs[None], NEG_INF)
    return per_head.reshape(N_HEADS // 2, 2 * BAND, 2 * BAND)


def _sample_bias(slopes, t_new, past):
    d1, d2 = WINDOWS[1][1], WINDOWS[2][1]
    t = jnp.arange(t_new, dtype=jnp.int32)[:, None]

    def table(slope, dist, valid):
        b = jnp.where(valid[:, None, :], -slope[None, :, None] * dist.astype(F32)[:, None, :], NEG_INF)
        return b.reshape(t_new * N_HEADS, -1)

    u = jnp.arange(BAND, dtype=jnp.int32)[None, :]
    dist_n = t - u
    ok_n = (u < t_new) & (dist_n >= 0)
    c = jnp.arange(BAND, dtype=jnp.int32)[None, :]
    dist_c = BAND + t - c
    b0 = table(slopes[0], jnp.concatenate([dist_c, dist_n], axis=1),
               jnp.concatenate([dist_c <= WINDOWS[0][0], ok_n], axis=1))
    c = jnp.arange(d1 * BAND, dtype=jnp.int32)[None, :]
    dist_c = d1 * BAND + t - c
    ok_c = (dist_c % d1 == 0) & (dist_c <= WINDOWS[1][0])
    b1 = table(slopes[1], jnp.concatenate([dist_c, dist_n], axis=1),
               jnp.concatenate([ok_c, ok_n & (dist_n % d1 == 0)], axis=1))
    per = past // d2
    col = jnp.arange(t_new * per, dtype=jnp.int32)[None, :]
    pos = (col % per) * d2 + col // per
    dist_c = past + t - pos
    ok_c = (dist_c % d2 == 0) & (dist_c <= WINDOWS[2][0])
    b2 = table(slopes[2], dist_c, ok_c)
    b2n = table(slopes[2], dist_n, ok_n & (dist_n % d2 == 0))
    return b0, b1, b2, b2n


def _head_mean_matrix():
    idx = jnp.arange(HEADS_W) // HEAD_DIM
    return jnp.where(idx[:, None] == idx[None, :], 1.0 / HEAD_DIM, 0.0).astype(BF16)


def _head_expand_matrix():
    lane = jnp.arange(LANES)[:, None]
    col = jnp.arange(HEADS_W)[None, :] // HEAD_DIM
    return jnp.where(lane == col, 1.0, 0.0).astype(BF16)


def kernel(x_prompt, x_sample, state_conv, cache_kv, norm_mix_g, norm_mlp_g, conv_w_pw1, conv_b_pw1,
           conv_w_dw, conv_b_dw, conv_ln_g, conv_ln_b, conv_w_pw2, conv_b_pw2, kv_norm_g, w_kv,
           k_norm_g, attn_w_q, q_norm_g, attn_w_o, mlp_w_up, mlp_w_down):
    n_p, t_p, _ = x_prompt.shape
    n_s, t_s, _ = x_sample.shape
    past = cache_kv.shape[1]
    scale = HEAD_DIM ** -0.5

    w1 = conv_w_pw1.astype(BF16)
    w2 = conv_w_pw2.astype(BF16)
    wkv = w_kv.astype(BF16)
    wq = attn_w_q.astype(BF16)
    wo = attn_w_o.astype(BF16)
    wup = mlp_w_up.astype(BF16)
    wdn = mlp_w_down.astype(BF16)
    vec = lambda a: a.reshape(-1, 1, a.shape[-1])
    g_mix, g_mlp, g_kv = vec(norm_mix_g), vec(norm_mlp_g), vec(kv_norm_g)
    e_mat = _head_mean_matrix()
    ex_mat = _head_expand_matrix()
    gk = vec(jnp.tile(k_norm_g, N_HEADS))
    gq = vec(jnp.tile(q_norm_g, (1, N_HEADS)) * scale)
    wkv = wkv[None]
    slopes = _alibi_slopes()
    bias_p = [_prompt_bias(slopes, g) for g in range(N_GROUPS)]
    bias_s = _sample_bias(slopes, t_s, past)
    conv_weights = (g_mix, w1, vec(conv_b_pw1), conv_w_dw, vec(conv_b_dw), vec(conv_ln_g),
                    vec(conv_ln_b), w2, vec(conv_b_pw2))

    job = _compaction_job(cache_kv, t_s)
    xp = x_prompt
    conv_p = []
    zero_hist = jnp.zeros((n_p, HIST_PAD, CONV_CH), F32)
    for layer in range(N_A_LAYERS):
        xp, st = _conv_prompt(xp, zero_hist, layer, conv_weights, job)
        conv_p.append(st[:, HIST_PAD - CONV_HIST:])
        xp = _mlp(xp.reshape(n_p * t_p, D_MODEL), layer, g_mlp, wup, wdn, job).reshape(n_p, t_p, D_MODEL)
    assert job["next"] == job["n"]
    recent, far = job["outs"]
    win = min(WINDOWS[-1][0], t_p)
    kv_p, *kvb_p = _kv_proj_prompt(xp, g_kv, wkv, e_mat, gk, win)
    for j in range(N_B_LAYERS):
        layer = N_A_LAYERS + j
        qs = _q_proj_prompt(xp, layer, j, g_mix, wq, e_mat, gq)
        parts = [_attn_group(qs[g], kvb_p[g], bias_p[g], g) for g in range(N_GROUPS)]
        xp = _attn_out_prompt(xp, parts, ex_mat, layer, j, wo, g_mlp, wup, wdn)
    y_prompt = xp
    conv_prompt = jnp.stack(conv_p, axis=0)
    kv_prompt = kv_p.reshape(n_p, win, 2, N_HEADS, HEAD_DIM)

    xs = jnp.transpose(x_sample, (1, 0, 2))
    st_in = jnp.transpose(state_conv, (0, 2, 1, 3))
    st_out = None
    for layer in range(N_A_LAYERS):
        xs, st_out = _conv_sample(xs, st_in, layer, conv_weights, st_out)
        xs = _mlp(xs.reshape(t_s * n_s, D_MODEL), layer, g_mlp, wup, wdn).reshape(t_s, n_s, D_MODEL)
    xs = xs.reshape(t_s * n_s, D_MODEL)
    kvn = _kv_proj_sample(xs, g_kv, wkv, e_mat, gk).reshape(t_s, n_s, KV_W)
    for j in range(N_B_LAYERS):
        layer = N_A_LAYERS + j
        q = _q_proj_sample(xs, layer, j, g_mix, wq, e_mat, gq)
        o = _attn_sample(q.reshape(t_s, n_s, N_GROUPS * HEADS_W), kvn, recent, far, bias_s)
        xs = _attn_out_sample(xs, o.reshape(t_s * n_s, HEADS_W), layer, j, wo, g_mlp, wup, wdn)
    y_sample = jnp.transpose(xs.reshape(t_s, n_s, D_MODEL), (1, 0, 2))
    conv_sample = jnp.transpose(st_out, (0, 2, 1, 3))
    kv_sample = jnp.transpose(kvn, (1, 0, 2)).reshape(n_s, t_s, 2, N_HEADS, HEAD_DIM)

    return (y_prompt, y_sample, conv_prompt, conv_sample, kv_prompt, kv_sample)
```

```python
import functools

import jax
import jax.numpy as jnp
from jax import lax
from jax.experimental import pallas as pl
from jax.experimental.pallas import tpu as pltpu

F32 = jnp.float32
BF16 = jnp.bfloat16

D_MODEL = 1024
CONV_CH = 1024
CONV_WIDTH = 31
CONV_HIST = CONV_WIDTH - 1
HIST_PAD = 32
D_FF = 4096
N_HEADS = 8
HEAD_DIM = 64
HEADS_W = N_HEADS * HEAD_DIM
KV_W = 2 * HEADS_W
WINDOWS = ((128, 1), (512, 4), (2048, 16))
N_GROUPS = len(WINDOWS)
BAND = 128
N_A_LAYERS = 2
N_B_LAYERS = 2
NORM_EPS = 1e-6
NEG_INF = float("-inf")
LOG2_E = 1.4426950408889634
LANES = 128
SUBLANES = 8

ROW_TILE = 512
ATT_OUT_TILE = 512
FF_CHUNK = 1024
ATT_QB = 1024
SAMPLE_CONV_NB = 32
SAMPLE_ROWS = 8
SAMPLE_SEQ_STEP = 4
MXU_DIM = 256
COMPACT_VMEM_MIB = 24
MIB = 1024 * 1024

assert all(w // d == BAND for w, d in WINDOWS)


def _const_spec(shape):
    nd = len(shape)
    return pl.BlockSpec(shape, lambda *_: (0,) * nd, pipeline_mode=pl.Buffered(1))


def _layer_spec(stacked, layer):
    return pl.BlockSpec((None,) + stacked.shape[1:], lambda *_: (layer, 0, 0),
                        pipeline_mode=pl.Buffered(1))


def _params(semantics, vmem_mib):
    return pltpu.CompilerParams(dimension_semantics=semantics, vmem_limit_bytes=vmem_mib * MIB)


def _lane_block(s, width=LANES):
    return slice(s * width, (s + 1) * width)


def _rms(x, g):
    return x * lax.rsqrt(jnp.mean(x * x, axis=-1, keepdims=True) + NORM_EPS) * g


def _mlp_residual(x, g_ref, wup_ref, wdn_ref):
    h = _rms(x, g_ref[...]).astype(BF16)
    acc = x
    for c in range(D_FF // FF_CHUNK):
        sl = _lane_block(c, FF_CHUNK)
        z = jnp.maximum(jnp.dot(h, wup_ref[:, sl], preferred_element_type=F32), 0.0)
        acc = acc + jnp.dot((z * z).astype(BF16), wdn_ref[sl, :], preferred_element_type=F32)
    return acc


def _head_mean_sq(v, e_ref):
    return jnp.dot((v * v).astype(BF16), e_ref[...], preferred_element_type=F32)


def _dot_t(a, b):
    return lax.dot_general(a, b, (((1,), (1,)), ((), ())), preferred_element_type=F32)


def _deinterleave_store(val, slab_ref, out_ref, d):
    rows, width = val.shape
    for s in range(width // LANES):
        slab_ref[s] = val[:, _lane_block(s)]
    for s in range(width // LANES):
        for r in range(d):
            out_ref[r, :, _lane_block(s)] = slab_ref[s, pl.ds(r, rows // d, stride=d), :].astype(BF16)


def _mlp_kernel(x_ref, g_ref, wup_ref, wdn_ref, o_ref):
    o_ref[...] = _mlp_residual(x_ref[...], g_ref, wup_ref, wdn_ref)


def _mlp(x, layer, g, wup, wdn, job=None):
    m = x.shape[0]
    row = pl.BlockSpec((ROW_TILE, D_MODEL), lambda i: (i, 0))
    return _call_with_compaction(
        _mlp_kernel, job, m // ROW_TILE, lambda i: i,
        out_shape=[jax.ShapeDtypeStruct((m, D_MODEL), F32)],
        grid=(m // ROW_TILE,),
        in_specs=[row] + [_layer_spec(a, layer) for a in (g, wup, wdn)],
        out_specs=[row],
        args=(x, g, wup, wdn),
        semantics=("arbitrary",), vmem_mib=34, name="mlp",
    )[0]


def _glu(h, w1_ref, b1_ref):
    u = jnp.dot(h, w1_ref[...], preferred_element_type=F32) + b1_ref[...]
    return u[:, :CONV_CH] * jax.nn.sigmoid(u[:, CONV_CH:])


def _conv_tail(c, x, bdw_ref, lng_ref, lnb_ref, w2_ref, b2_ref):
    c = c + bdw_ref[...]
    mu = jnp.mean(c, axis=-1, keepdims=True)
    xc = c - mu
    y = xc * lax.rsqrt(jnp.mean(xc * xc, axis=-1, keepdims=True) + NORM_EPS)
    y = y * lng_ref[...] + lnb_ref[...]
    y = y * jax.nn.sigmoid(y)
    out = jnp.dot(y.astype(BF16), w2_ref[...], preferred_element_type=F32) + b2_ref[...]
    return x + out


def _conv_prompt_kernel(x_ref, st_ref, g_ref, w1_ref, b1_ref, wdw_ref, bdw_ref, lng_ref, lnb_ref,
                        w2_ref, b2_ref, o_ref, sto_ref, buf_ref):
    n_slabs = CONV_CH // LANES

    @pl.when(pl.program_id(1) == 0)
    def _():
        for s in range(n_slabs):
            buf_ref[s, 0:HIST_PAD, :] = st_ref[:, _lane_block(s)]

    x = x_ref[...]
    h = _rms(x, g_ref[...]).astype(BF16)
    u = _glu(h, w1_ref, b1_ref)
    for s in range(n_slabs):
        buf_ref[s, HIST_PAD:HIST_PAD + ROW_TILE, :] = u[:, _lane_block(s)]
    off = HIST_PAD - CONV_HIST
    cs = []
    for s in range(n_slabs):
        ls = _lane_block(s)
        c = buf_ref[s, off:off + ROW_TILE, :] * wdw_ref[0:1, ls]
        for k in range(1, CONV_WIDTH):
            c = c + buf_ref[s, off + k:off + k + ROW_TILE, :] * wdw_ref[k:k + 1, ls]
        cs.append(c)
    c = jnp.concatenate(cs, axis=1)
    o_ref[...] = _conv_tail(c, x, bdw_ref, lng_ref, lnb_ref, w2_ref, b2_ref)
    for s in range(n_slabs):
        last = buf_ref[s, ROW_TILE:ROW_TILE + HIST_PAD, :]
        sto_ref[:, _lane_block(s)] = last
        buf_ref[s, 0:HIST_PAD, :] = last


def _conv_prompt(x, st, layer, weights, job=None):
    n, t, _ = x.shape
    tiles = t // ROW_TILE
    row = pl.BlockSpec((None, ROW_TILE, D_MODEL), lambda b, i: (b, i, 0))
    st_spec = pl.BlockSpec((None, HIST_PAD, CONV_CH), lambda b, i: (b, 0, 0))
    return _call_with_compaction(
        _conv_prompt_kernel, job, n * tiles, lambda b, i: b * tiles + i,
        out_shape=[jax.ShapeDtypeStruct((n, t, D_MODEL), F32),
                   jax.ShapeDtypeStruct((n, HIST_PAD, CONV_CH), F32)],
        grid=(n, tiles),
        in_specs=[row, st_spec] + [_layer_spec(a, layer) for a in weights],
        out_specs=[row, st_spec],
        scratch_shapes=[pltpu.VMEM((CONV_CH // LANES, HIST_PAD + ROW_TILE, LANES), F32)],
        args=(x, st, *weights),
        semantics=("arbitrary", "arbitrary"), vmem_mib=32, name="conv_prompt",
    )


def _conv_sample_kernel(x_ref, st_ref, g_ref, w1_ref, b1_ref, wdw_ref, bdw_ref, lng_ref, lnb_ref,
                        w2_ref, b2_ref, o_ref, sto_ref):
    t_new, nb, _ = x_ref.shape
    x = x_ref[...].reshape(t_new * nb, D_MODEL)
    h = _rms(x, g_ref[...]).astype(BF16)
    glu = _glu(h, w1_ref, b1_ref)
    new = [glu[t * nb:(t + 1) * nb, :] for t in range(t_new)]

    def full(j):
        return st_ref[j] if j < CONV_HIST else new[j - CONV_HIST]

    outs = []
    for t in range(t_new):
        c = full(t) * wdw_ref[0:1, :]
        for k in range(1, CONV_WIDTH):
            c = c + full(t + k) * wdw_ref[k:k + 1, :]
        outs.append(c)
    c = jnp.concatenate(outs, axis=0)
    o_ref[...] = _conv_tail(c, x, bdw_ref, lng_ref, lnb_ref, w2_ref, b2_ref).reshape(t_new, nb, D_MODEL)
    for j in range(CONV_HIST):
        sto_ref[j] = full(j + t_new)


def _conv_sample_chained_kernel(*refs):
    _conv_sample_kernel(*refs[:11], *refs[12:])


def _conv_sample(x, st_all, layer, weights, st_out=None):
    t_new, n, _ = x.shape
    nb = SAMPLE_CONV_NB
    x_spec = pl.BlockSpec((t_new, nb, D_MODEL), lambda i: (0, i, 0))
    st_spec = pl.BlockSpec((None, CONV_HIST, nb, CONV_CH), lambda i: (layer, 0, i, 0))
    in_specs = [x_spec, st_spec] + [_layer_spec(a, layer) for a in weights]
    args = [x, st_all, *weights]
    if st_out is None:
        body, aliases = _conv_sample_kernel, {}
    else:
        body, aliases = _conv_sample_chained_kernel, {len(args): 1}
        in_specs.append(pl.BlockSpec(memory_space=pl.ANY))
        args.append(st_out)
    return pl.pallas_call(
        body,
        out_shape=(jax.ShapeDtypeStruct((t_new, n, D_MODEL), F32),
                   jax.ShapeDtypeStruct(st_all.shape, F32)),
        grid=(n // nb,),
        in_specs=in_specs,
        out_specs=(x_spec, st_spec),
        input_output_aliases=aliases,
        compiler_params=_params(("arbitrary",), 48),
        name="conv_sample",
    )(*args)


def _kv_rows(x_ref, g_ref, wkv_ref, e_ref, gk_ref):
    h = _rms(x_ref[...], g_ref[...]).astype(BF16)
    k = jnp.dot(h, wkv_ref[:, :HEADS_W], preferred_element_type=F32)
    v = jnp.dot(h, wkv_ref[:, HEADS_W:], preferred_element_type=F32)
    k = k * lax.rsqrt(_head_mean_sq(k, e_ref) + NORM_EPS) * gk_ref[...]
    return jnp.concatenate([k, v], axis=1)


def _kv_sample_kernel(x_ref, g_ref, wkv_ref, e_ref, gk_ref, kv_ref):
    kv_ref[...] = _kv_rows(x_ref, g_ref, wkv_ref, e_ref, gk_ref)


def _kvq_prompt_kernel(x_ref, g_ref, wkv_ref, e_ref, gk_ref, gm_ref, wq_ref, gq_ref,
                       kv_ref, kb0_ref, kb1_ref, kb2_ref, q0_ref, q1_ref, q2_ref, slab_ref):
    kv = _kv_rows(x_ref, g_ref, wkv_ref, e_ref, gk_ref)
    kv_ref[...] = kv
    kb0_ref[0] = kv.astype(BF16)
    _deinterleave_store(kv, slab_ref, kb1_ref, WINDOWS[1][1])
    _deinterleave_store(kv, slab_ref, kb2_ref, WINDOWS[2][1])
    h = _rms(x_ref[...], gm_ref[...]).astype(BF16)
    _q_prompt_store(h, wq_ref, e_ref, gq_ref, q0_ref, q1_ref, q2_ref, slab_ref)


def _proj_weight_specs(layer, g, w, e, gh):
    return [_layer_spec(g, layer), _layer_spec(w, layer), _const_spec(e.shape), _layer_spec(gh, layer)]


def _kv_proj_sample(x, g, wkv, e, gk):
    m = x.shape[0]
    return pl.pallas_call(
        _kv_sample_kernel,
        out_shape=jax.ShapeDtypeStruct((m, KV_W), F32),
        grid=(m // ROW_TILE,),
        in_specs=[pl.BlockSpec((ROW_TILE, D_MODEL), lambda i: (i, 0))] + _proj_weight_specs(0, g, wkv, e, gk),
        out_specs=pl.BlockSpec((ROW_TILE, KV_W), lambda i: (i, 0)),
        compiler_params=_params(("arbitrary",), 32),
        name="kv_proj_sample",
    )(x, g, wkv, e, gk)


def _dilated_shapes(n, t, width, dtype):
    sds, specs = [], []
    for _, d in WINDOWS:
        sds.append(jax.ShapeDtypeStruct((n, d, t // d, width), dtype))
        specs.append(pl.BlockSpec((None, d, ROW_TILE // d, width), lambda b, i: (b, 0, i, 0)))
    return sds, specs


def _kvq_proj_prompt(x, g, wkv, e, gk, win, layer, g_mix, wq, gq):
    n, t, _ = x.shape
    kv_sds, kv_specs = _dilated_shapes(n, t, KV_W, BF16)
    q_sds, q_specs = _dilated_shapes(n, t, HEADS_W, BF16)
    skip = (t - win) // ROW_TILE
    assert win % ROW_TILE == 0 and t % ROW_TILE == 0
    f32_spec = pl.BlockSpec((None, ROW_TILE, KV_W), lambda b, i: (b, jnp.maximum(i - skip, 0), 0))
    res = pl.pallas_call(
        _kvq_prompt_kernel,
        out_shape=[jax.ShapeDtypeStruct((n, win, KV_W), F32)] + kv_sds + q_sds,
        grid=(n, t // ROW_TILE),
        in_specs=[pl.BlockSpec((None, ROW_TILE, D_MODEL), lambda b, i: (b, i, 0))]
        + _proj_weight_specs(0, g, wkv, e, gk)
        + [_layer_spec(g_mix, layer), _layer_spec(wq, 0), _layer_spec(gq, 0)],
        out_specs=[f32_spec] + kv_specs + q_specs,
        scratch_shapes=[pltpu.VMEM((KV_W // LANES, ROW_TILE, LANES), F32)],
        compiler_params=_params(("arbitrary", "arbitrary"), 48),
        name="kvq_proj_prompt",
    )(x, g, wkv, e, gk, g_mix, wq, gq)
    return res[0], res[1:4], res[4:7]


def _q_group(h, wq_ref, e_ref, gq_ref, grp):
    q = jnp.dot(h, wq_ref[:, _lane_block(grp, HEADS_W)], preferred_element_type=F32)
    return q * lax.rsqrt(_head_mean_sq(q, e_ref) + NORM_EPS) * gq_ref[...]


def _q_sample_kernel(x_ref, g_ref, wq_ref, e_ref, gq_ref, q_ref):
    h = _rms(x_ref[...], g_ref[...]).astype(BF16)
    for grp in range(N_GROUPS):
        q = _q_group(h, wq_ref, e_ref, gq_ref, grp)
        q_ref[:, _lane_block(grp, HEADS_W)] = q.astype(BF16).astype(F32)


def _q_prompt_store(h, wq_ref, e_ref, gq_ref, q0_ref, q1_ref, q2_ref, slab_ref):
    q0_ref[0] = _q_group(h, wq_ref, e_ref, gq_ref, 0).astype(BF16)
    _deinterleave_store(_q_group(h, wq_ref, e_ref, gq_ref, 1), slab_ref, q1_ref, WINDOWS[1][1])
    _deinterleave_store(_q_group(h, wq_ref, e_ref, gq_ref, 2), slab_ref, q2_ref, WINDOWS[2][1])


def _q_proj_sample(x, layer, j, g, wq, e, gq):
    m = x.shape[0]
    qw = N_GROUPS * HEADS_W
    specs = [_layer_spec(g, layer), _layer_spec(wq, j), _const_spec(e.shape), _layer_spec(gq, j)]
    return pl.pallas_call(
        _q_sample_kernel,
        out_shape=jax.ShapeDtypeStruct((m, qw), F32),
        grid=(m // ROW_TILE,),
        in_specs=[pl.BlockSpec((ROW_TILE, D_MODEL), lambda i: (i, 0))] + specs,
        out_specs=pl.BlockSpec((ROW_TILE, qw), lambda i: (i, 0)),
        compiler_params=_params(("arbitrary",), 32),
        name="q_proj_sample",
    )(x, g, wq, e, gq)


def _attn_group_kernel(q_ref, kc_ref, kp_ref, vc_ref, vp_ref, bias_ref, acc_ref, m_ref, l_ref,
                       kk_ref, vv_ref):
    kk_ref[0:BAND, :] = kp_ref[...]
    kk_ref[BAND:, :] = kc_ref[...]
    vv_ref[0:BAND, :] = vp_ref[...]
    vv_ref[BAND:, :] = vc_ref[...]
    first_block = pl.program_id(2) == 0
    key_col = lax.broadcasted_iota(jnp.int32, (2 * BAND, 2 * BAND), 1)
    lane = lax.broadcasted_iota(jnp.int32, (BAND, LANES), 1)
    left = lane < HEAD_DIM
    lane_row = lax.broadcasted_iota(jnp.int32, (1, LANES), 1)
    head_keep = [jnp.where(lane_row < HEAD_DIM, 1.0, 0.0).astype(BF16),
                 jnp.where(lane_row < HEAD_DIM, 0.0, 1.0).astype(BF16)]
    no_history = jnp.logical_and(first_block, key_col < BAND)

    for sb in range(q_ref.shape[0] // BAND):
        r0 = sb * BAND
        m_tile = jnp.zeros((BAND, LANES), F32)
        l_tile = jnp.ones((BAND, LANES), F32)
        for pair in range(N_HEADS // 2):
            ls = _lane_block(pair)
            q2 = q_ref[r0:r0 + BAND, ls]
            k2 = kk_ref[r0:r0 + 2 * BAND, ls]
            v2 = vv_ref[r0:r0 + 2 * BAND, ls]
            qq = jnp.concatenate([q2 * head_keep[0], q2 * head_keep[1]], axis=0)
            s = _dot_t(qq, k2) + bias_ref[pair]
            if sb == 0:
                s = jnp.where(no_history, NEG_INF, s)
            m = jnp.max(s, axis=-1, keepdims=True)
            p = jnp.exp2(s - m)
            l = jnp.sum(p, axis=-1, keepdims=True)
            a = jnp.dot(p.astype(BF16), v2, preferred_element_type=F32)
            acc_ref[r0:r0 + BAND, ls] = jnp.where(left, a[:BAND], a[BAND:])
            for hh in range(2):
                rows = slice(hh * BAND, (hh + 1) * BAND)
                m_tile = jnp.where(lane == 2 * pair + hh, m[rows], m_tile)
                l_tile = jnp.where(lane == 2 * pair + hh, l[rows], l_tile)
        m_ref[r0:r0 + BAND, :] = m_tile
        l_ref[r0:r0 + BAND, :] = l_tile


def _attn_group(q, kvb, bias, grp):
    n, d, rows, _ = q.shape
    qb = min(ATT_QB, rows)
    per_blk = qb // BAND
    stat_sds = jax.ShapeDtypeStruct((n, d, rows, LANES), F32)
    stat_spec = pl.BlockSpec((None, None, qb, LANES), lambda b, r, i: (b, r, i, 0))

    def prev_idx(i):
        return jnp.maximum(i * per_blk - 1, 0)

    def cur(col):
        return pl.BlockSpec((None, None, qb, HEADS_W), lambda b, r, i: (b, r, i, col))

    def prev(col):
        return pl.BlockSpec((None, None, BAND, HEADS_W), lambda b, r, i: (b, r, prev_idx(i), col))

    return pl.pallas_call(
        _attn_group_kernel,
        out_shape=(jax.ShapeDtypeStruct((n, d, rows, HEADS_W), F32), stat_sds, stat_sds),
        grid=(n, d, rows // qb),
        in_specs=[cur(0), cur(0), prev(0), cur(1), prev(1),
                  _const_spec((N_HEADS // 2, 2 * BAND, 2 * BAND))],
        out_specs=(cur(0), stat_spec, stat_spec),
        scratch_shapes=[pltpu.VMEM((BAND + qb, HEADS_W), BF16),
                        pltpu.VMEM((BAND + qb, HEADS_W), BF16)],
        compiler_params=_params(("arbitrary", "arbitrary", "arbitrary"), 32),
        name=f"attn_group{grp}",
    )(q, kvb, kvb, kvb, kvb, bias)


def _compact_cache_kernel(c_ref, eye_ref, sel_ref, recent_ref, far_ref, *, t_new):
    p = c_ref.shape[-1]
    d1, d2 = WINDOWS[1][1], WINDOWS[2][1]
    x = c_ref[...].reshape(KV_W, p).astype(BF16)
    lo = p - d1 * BAND
    for j in range(d1 * BAND // MXU_DIM):
        cols = x[:, lo + j * MXU_DIM:lo + (j + 1) * MXU_DIM]
        recent_ref[j * MXU_DIM:(j + 1) * MXU_DIM, :] = _dot_t(eye_ref[...], cols).astype(BF16)
    per = MXU_DIM // d2
    for c in range(p // MXU_DIM):
        picked = _dot_t(sel_ref[...], x[:, c * MXU_DIM:(c + 1) * MXU_DIM]).astype(BF16)
        for t in range(t_new):
            far_ref[t * BAND + c * per:t * BAND + (c + 1) * per, :] = picked[t * per:(t + 1) * per, :]


def _compaction_job(cache_kv, t_new):
    n, p = cache_kv.shape[:2]
    d1, d2 = WINDOWS[1][1], WINDOWS[2][1]
    per = MXU_DIM // d2
    assert p == WINDOWS[2][0] and p // d2 == BAND and t_new <= d1 and p % MXU_DIM == 0
    feat_major = jnp.transpose(cache_kv, (0, 2, 3, 4, 1))
    eye = jnp.eye(MXU_DIM, dtype=BF16)
    row = jnp.arange(t_new * per)[:, None]
    col = jnp.arange(MXU_DIM)[None, :]
    sel = jnp.where(col == (row % per) * d2 + row // per, 1.0, 0.0).astype(BF16)
    return dict(inputs=(feat_major, eye, sel), t_new=t_new, n=n, next=0, outs=None)


def _fused_body(*refs, body, n_in, n_extra_in, n_out, t_new):
    c_ref, eye_ref, sel_ref = refs[n_in:n_in + 3]
    rest = refs[n_in + n_extra_in:]
    recent_ref, far_ref = rest[n_out:n_out + 2]
    body(*refs[:n_in], *rest[:n_out], *rest[n_out + 2:])
    _compact_cache_kernel(c_ref, eye_ref, sel_ref, recent_ref, far_ref, t_new=t_new)


def _call_with_compaction(body, job, steps, step_of, *, grid, in_specs, out_specs, out_shape, args,
                          semantics, vmem_mib, name, scratch_shapes=()):
    if job is None:
        return pl.pallas_call(body, out_shape=out_shape, grid=grid, in_specs=in_specs, out_specs=out_specs,
                              scratch_shapes=scratch_shapes, compiler_params=_params(semantics, vmem_mib),
                              name=name)(*args)
    base, n, t_new = job["next"], job["n"], job["t_new"]
    assert base + steps <= n
    job["next"] = base + steps
    feat, eye, sel = job["inputs"]
    seq = lambda *idx: base + step_of(*idx)
    extra_specs = [pl.BlockSpec((None,) + feat.shape[1:], lambda *idx: (seq(*idx), 0, 0, 0, 0)),
                   _const_spec(eye.shape), _const_spec(sel.shape)]
    extra_args = [feat, eye, sel]
    n_out = len(out_shape)
    aliases = {}
    for k, prev in enumerate(job["outs"] or ()):
        aliases[len(args) + len(extra_args)] = n_out + k
        extra_specs.append(pl.BlockSpec(memory_space=pl.ANY))
        extra_args.append(prev)
    rows = (WINDOWS[1][1] * BAND, t_new * BAND)
    res = pl.pallas_call(
        functools.partial(_fused_body, body=body, n_in=len(args), n_extra_in=len(extra_args),
                          n_out=n_out, t_new=t_new),
        out_shape=list(out_shape) + [jax.ShapeDtypeStruct((n, r, KV_W), BF16) for r in rows],
        grid=grid,
        in_specs=list(in_specs) + extra_specs,
        out_specs=list(out_specs) + [pl.BlockSpec((None, r, KV_W), lambda *idx: (seq(*idx), 0, 0)) for r in rows],
        scratch_shapes=scratch_shapes,
        input_output_aliases=aliases,
        compiler_params=_params(semantics, vmem_mib + COMPACT_VMEM_MIB),
        name=name,
    )(*args, *extra_args)
    job["outs"] = res[n_out:]
    return res[:n_out]


def _attn_sample_kernel(q_ref, kvn_ref, recent_ref, far_ref, b0_ref, b1_ref, b2_ref, b2n_ref, o_ref):
    t_new = q_ref.shape[0]
    n_seq = SAMPLE_SEQ_STEP
    base = (pl.program_id(0) % (SAMPLE_ROWS // n_seq)) * n_seq
    n_rows = t_new * N_HEADS
    hsel = (lax.broadcasted_iota(jnp.int32, (N_HEADS, HEADS_W), 1) // HEAD_DIM
            == lax.broadcasted_iota(jnp.int32, (N_HEADS, HEADS_W), 0))
    row8 = lax.broadcasted_iota(jnp.int32, (SUBLANES, KV_W), 0)

    qs, kas, vas, kbs, vbs = [], [], [], [], []
    for j in range(n_seq):
        b = base + j
        qs.append([jnp.concatenate(
            [jnp.where(hsel, q_ref[t, pl.ds(b, 1), _lane_block(grp, HEADS_W)], 0.0) for t in range(t_new)],
            axis=0).astype(BF16) for grp in range(N_GROUPS)])
        new8 = jnp.zeros((SUBLANES, KV_W), F32)
        for t in range(t_new):
            new8 = jnp.where(row8 == t, kvn_ref[t, pl.ds(b, 1), :], new8)
        new = jnp.concatenate([new8, jnp.zeros((BAND - SUBLANES, KV_W), F32)], axis=0).astype(BF16)
        recent = recent_ref[j]
        kas.append(jnp.concatenate([recent[:, :HEADS_W], new[:, :HEADS_W]], axis=0))
        vas.append(jnp.concatenate([recent[:, HEADS_W:], new[:, HEADS_W:]], axis=0))
        kbs.append(far_ref[j, :, :HEADS_W])
        vbs.append(far_ref[j, :, HEADS_W:])
    n_a = kas[0].shape[0]

    def scores(grp, keys, lo, bias_ref):
        return jnp.concatenate([_dot_t(qs[j][grp], keys[j][lo:]) for j in range(n_seq)],
                               axis=0) + bias_ref[...]

    def softmax_pv(pieces):
        m = functools.reduce(jnp.maximum, [jnp.max(s, axis=-1, keepdims=True) for s, _, _ in pieces])
        l = jnp.zeros((n_seq * n_rows, 1), F32)
        acc = jnp.zeros((n_seq * n_rows, HEADS_W), F32)
        for s, vals, lo in pieces:
            p = jnp.exp2(s - m)
            l = l + jnp.sum(p, axis=-1, keepdims=True)
            pb = p.astype(BF16)
            acc = acc + jnp.concatenate(
                [jnp.dot(pb[j * n_rows:(j + 1) * n_rows], vals[j][lo:], preferred_element_type=F32)
                 for j in range(n_seq)], axis=0)
        return acc, m, l

    lo0, lo_new = n_a - 2 * BAND, n_a - BAND
    parts = [
        softmax_pv([(scores(0, kas, lo0, b0_ref), vas, lo0)]),
        softmax_pv([(scores(1, kas, 0, b1_ref), vas, 0)]),
        softmax_pv([(scores(2, kbs, 0, b2_ref), vbs, 0), (scores(2, kas, lo_new, b2n_ref), vas, lo_new)]),
    ]
    m_all = functools.reduce(jnp.maximum, [m for _, m, _ in parts])
    num = jnp.zeros((n_seq * n_rows, HEADS_W), F32)
    den = jnp.zeros((n_seq * n_rows, 1), F32)
    for acc, m, l in parts:
        e = jnp.exp2(m - m_all)
        num = num + e * acc
        den = den + e * l
    out = num * (1.0 / den)
    for j in range(n_seq):
        for t in range(t_new):
            r0 = j * n_rows + t * N_HEADS
            own = jnp.where(hsel, out[r0:r0 + N_HEADS, :], 0.0)
            o_ref[t, pl.ds(base + j, 1), :] = jnp.sum(own, axis=0, keepdims=True)


def _attn_sample(q, kvn, recent, far, biases):
    t_new, n, qw = q.shape
    step = SAMPLE_SEQ_STEP
    blk = lambda width: pl.BlockSpec((t_new, SAMPLE_ROWS, width),
                                     lambda i: (0, i // (SAMPLE_ROWS // step), 0))
    rows = lambda a: pl.BlockSpec((step,) + a.shape[1:], lambda i: (i, 0, 0))
    biases = [jnp.tile(b, (step, 1)) for b in biases]
    return pl.pallas_call(
        _attn_sample_kernel,
        out_shape=jax.ShapeDtypeStruct((t_new, n, HEADS_W), F32),
        grid=(n // step,),
        in_specs=[blk(qw), blk(KV_W), rows(recent), rows(far)] + [_const_spec(b.shape) for b in biases],
        out_specs=blk(HEADS_W),
        compiler_params=_params(("arbitrary",), 40),
        name="attn_sample",
    )(q, kvn, recent, far, *biases)


def _attn_out_sample_kernel(x_ref, o_ref, wo_ref, g_ref, wup_ref, wdn_ref, out_ref):
    x = x_ref[...] + jnp.dot(o_ref[...].astype(BF16), wo_ref[...], preferred_element_type=F32)
    out_ref[...] = _mlp_residual(x, g_ref, wup_ref, wdn_ref)


def _attn_out_prompt_kernel(x_ref, a0_ref, m0_ref, l0_ref, a1_ref, m1_ref, l1_ref, a2_ref, m2_ref, l2_ref,
                            ex_ref, wo_ref, g_ref, wup_ref, wdn_ref, out_ref, slab_ref):
    rows = x_ref.shape[0]
    n_o = HEADS_W // LANES
    for gi, (a_ref, m_ref, l_ref) in enumerate(((a1_ref, m1_ref, l1_ref), (a2_ref, m2_ref, l2_ref))):
        d = WINDOWS[gi + 1][1]
        for r in range(d):
            dst = pl.ds(r, rows // d, stride=d)
            for s in range(n_o):
                slab_ref[gi, s, dst, :] = a_ref[r, :, _lane_block(s)]
            slab_ref[gi, n_o, dst, :] = m_ref[r]
            slab_ref[gi, n_o + 1, dst, :] = l_ref[r]
    accs = [a0_ref[0]] + [jnp.concatenate([slab_ref[gi, s] for s in range(n_o)], axis=1) for gi in range(2)]
    ms = [m0_ref[0], slab_ref[0, n_o], slab_ref[1, n_o]]
    ls = [l0_ref[0], slab_ref[0, n_o + 1], slab_ref[1, n_o + 1]]
    mx = functools.reduce(jnp.maximum, ms)
    es = [jnp.exp2(m - mx) for m in ms]
    inv = 1.0 / functools.reduce(jnp.add, [e * l for e, l in zip(es, ls)])
    o = jnp.zeros((rows, HEADS_W), F32)
    for e, og in zip(es, accs):
        w = e * inv
        hi = w.astype(BF16)
        lo = (w - hi.astype(F32)).astype(BF16)
        wide = (jnp.dot(hi, ex_ref[...], preferred_element_type=F32)
                + jnp.dot(lo, ex_ref[...], preferred_element_type=F32))
        o = o + wide * og
    x = x_ref[...] + jnp.dot(o.astype(BF16), wo_ref[...], preferred_element_type=F32)
    out_ref[...] = _mlp_residual(x, g_ref, wup_ref, wdn_ref)


def _attn_out_q_prompt_kernel(*refs):
    n_in = 15
    gm_ref, wq_ref, e_ref, gq_ref = refs[n_in:n_in + 4]
    out_ref, q0_ref, q1_ref, q2_ref, slab_ref, qslab_ref = refs[n_in + 4:]
    _attn_out_prompt_kernel(*refs[:n_in], out_ref, slab_ref)
    h = _rms(out_ref[...], gm_ref[...]).astype(BF16)
    _q_prompt_store(h, wq_ref, e_ref, gq_ref, q0_ref, q1_ref, q2_ref, qslab_ref)


def _mlp_weight_specs(layer, j, wo, g, wup, wdn):
    return [_layer_spec(wo, j)] + [_layer_spec(a, layer) for a in (g, wup, wdn)]


def _attn_out_sample(x, o, layer, j, wo, g, wup, wdn):
    m = x.shape[0]
    row = pl.BlockSpec((ROW_TILE, D_MODEL), lambda i: (i, 0))
    return pl.pallas_call(
        _attn_out_sample_kernel,
        out_shape=jax.ShapeDtypeStruct((m, D_MODEL), F32),
        grid=(m // ROW_TILE,),
        in_specs=[row, pl.BlockSpec((ROW_TILE, HEADS_W), lambda i: (i, 0))]
        + _mlp_weight_specs(layer, j, wo, g, wup, wdn),
        out_specs=row,
        compiler_params=_params(("arbitrary",), 48),
        name="attn_out_sample",
    )(x, o, wo, g, wup, wdn)


def _attn_out_prompt(x, parts, ex, layer, j, wo, g, wup, wdn, next_q=None):
    n, t, _ = x.shape
    tile = ATT_OUT_TILE
    assert tile == ROW_TILE
    row = pl.BlockSpec((None, tile, D_MODEL), lambda b, i: (b, i, 0))
    part_specs = []
    for _, d in WINDOWS:
        for width in (HEADS_W, LANES, LANES):
            part_specs.append(pl.BlockSpec((None, d, tile // d, width), lambda b, i: (b, 0, i, 0)))
    flat = [a for part in parts for a in part]
    in_specs = ([row] + part_specs + [_const_spec((LANES, HEADS_W))]
                + _mlp_weight_specs(layer, j, wo, g, wup, wdn))
    args = [x, *flat, ex, wo, g, wup, wdn]
    out_shape, out_specs = [jax.ShapeDtypeStruct((n, t, D_MODEL), F32)], [row]
    scratch = [pltpu.VMEM((2, HEADS_W // LANES + 2, tile, LANES), F32)]
    body = _attn_out_prompt_kernel
    if next_q is not None:
        g_mix, wq, e, gq = next_q
        in_specs += [_layer_spec(g_mix, layer + 1), _layer_spec(wq, j + 1), _const_spec(e.shape),
                     _layer_spec(gq, j + 1)]
        args += [g_mix, wq, e, gq]
        q_sds, q_specs = _dilated_shapes(n, t, HEADS_W, BF16)
        out_shape, out_specs = out_shape + q_sds, out_specs + q_specs
        scratch.append(pltpu.VMEM((HEADS_W // LANES, tile, LANES), F32))
        body = _attn_out_q_prompt_kernel
    res = pl.pallas_call(
        body,
        out_shape=out_shape,
        grid=(n, t // tile),
        in_specs=in_specs,
        out_specs=out_specs,
        scratch_shapes=scratch,
        compiler_params=_params(("arbitrary", "arbitrary"), 56),
        name="attn_out_prompt",
    )(*args)
    return res[0], res[1:]


def _alibi_slopes():
    n = N_GROUPS * N_HEADS
    s = 2.0 ** (-8.0 * jnp.arange(1, n + 1, dtype=F32) / n)
    return s.reshape(N_GROUPS, N_HEADS)


def _prompt_bias(slopes, grp):
    d = WINDOWS[grp][1]
    iq = jnp.arange(BAND, dtype=jnp.int32)[:, None]
    ik = jnp.arange(2 * BAND, dtype=jnp.int32)[None, :]
    steps = iq + BAND - ik
    valid = (steps >= 0) & (steps <= BAND)
    offs = (steps * d).astype(F32)
    per_head = jnp.where(valid[None], -slopes[grp][:, None, None] * offs[None] * LOG2_E, NEG_INF)
    return per_head.reshape(N_HEADS // 2, 2 * BAND, 2 * BAND)


def _sample_bias(slopes, t_new, past):
    d1, d2 = WINDOWS[1][1], WINDOWS[2][1]
    t = jnp.arange(t_new, dtype=jnp.int32)[:, None]

    def table(slope, dist, valid):
        b = jnp.where(valid[:, None, :], -slope[None, :, None] * dist.astype(F32)[:, None, :] * LOG2_E,
                      NEG_INF)
        return b.reshape(t_new * N_HEADS, -1)

    u = jnp.arange(BAND, dtype=jnp.int32)[None, :]
    dist_n = t - u
    ok_n = (u < t_new) & (dist_n >= 0)
    c = jnp.arange(BAND, dtype=jnp.int32)[None, :]
    dist_c = BAND + t - c
    b0 = table(slopes[0], jnp.concatenate([dist_c, dist_n], axis=1),
               jnp.concatenate([dist_c <= WINDOWS[0][0], ok_n], axis=1))
    c = jnp.arange(d1 * BAND, dtype=jnp.int32)[None, :]
    dist_c = d1 * BAND + t - c
    ok_c = (dist_c % d1 == 0) & (dist_c <= WINDOWS[1][0])
    b1 = table(slopes[1], jnp.concatenate([dist_c, dist_n], axis=1),
               jnp.concatenate([ok_c, ok_n & (dist_n % d1 == 0)], axis=1))
    per = past // d2
    col = jnp.arange(t_new * per, dtype=jnp.int32)[None, :]
    pos = (col % per) * d2 + col // per
    dist_c = past + t - pos
    ok_c = (dist_c % d2 == 0) & (dist_c <= WINDOWS[2][0])
    b2 = table(slopes[2], dist_c, ok_c)
    b2n = table(slopes[2], dist_n, ok_n & (dist_n % d2 == 0))
    return b0, b1, b2, b2n


def _head_mean_matrix():
    idx = jnp.arange(HEADS_W) // HEAD_DIM
    return jnp.where(idx[:, None] == idx[None, :], 1.0 / HEAD_DIM, 0.0).astype(BF16)


def _head_expand_matrix():
    lane = jnp.arange(LANES)[:, None]
    col = jnp.arange(HEADS_W)[None, :] // HEAD_DIM
    return jnp.where(lane == col, 1.0, 0.0).astype(BF16)


def kernel(x_prompt, x_sample, state_conv, cache_kv, norm_mix_g, norm_mlp_g, conv_w_pw1, conv_b_pw1,
           conv_w_dw, conv_b_dw, conv_ln_g, conv_ln_b, conv_w_pw2, conv_b_pw2, kv_norm_g, w_kv,
           k_norm_g, attn_w_q, q_norm_g, attn_w_o, mlp_w_up, mlp_w_down):
    n_p, t_p, _ = x_prompt.shape
    n_s, t_s, _ = x_sample.shape
    past = cache_kv.shape[1]
    scale = HEAD_DIM ** -0.5 * LOG2_E

    w1 = conv_w_pw1.astype(BF16)
    w2 = conv_w_pw2.astype(BF16)
    wkv = w_kv.astype(BF16)
    wq = attn_w_q.astype(BF16)
    wo = attn_w_o.astype(BF16)
    wup = mlp_w_up.astype(BF16)
    wdn = mlp_w_down.astype(BF16)
    vec = lambda a: a.reshape(-1, 1, a.shape[-1])
    g_mix, g_mlp, g_kv = vec(norm_mix_g), vec(norm_mlp_g), vec(kv_norm_g)
    e_mat = _head_mean_matrix()
    ex_mat = _head_expand_matrix()
    gk = vec(jnp.tile(k_norm_g, N_HEADS))
    gq = vec(jnp.tile(q_norm_g, (1, N_HEADS)) * scale)
    wkv = wkv[None]
    slopes = _alibi_slopes()
    bias_p = [_prompt_bias(slopes, g) for g in range(N_GROUPS)]
    bias_s = _sample_bias(slopes, t_s, past)
    conv_weights = (g_mix, w1, vec(conv_b_pw1), conv_w_dw, vec(conv_b_dw), vec(conv_ln_g),
                    vec(conv_ln_b), w2, vec(conv_b_pw2))

    job = _compaction_job(cache_kv, t_s)
    xp = x_prompt
    conv_p = []
    zero_hist = jnp.zeros((n_p, HIST_PAD, CONV_CH), F32)
    for layer in range(N_A_LAYERS):
        xp, st = _conv_prompt(xp, zero_hist, layer, conv_weights, job)
        conv_p.append(st[:, HIST_PAD - CONV_HIST:])
        xp = _mlp(xp.reshape(n_p * t_p, D_MODEL), layer, g_mlp, wup, wdn, job).reshape(n_p, t_p, D_MODEL)
    assert job["next"] == job["n"]
    recent, far = job["outs"]
    win = min(WINDOWS[-1][0], t_p)
    kv_p, kvb_p, qs = _kvq_proj_prompt(xp, g_kv, wkv, e_mat, gk, win, N_A_LAYERS, g_mix, wq, gq)
    for j in range(N_B_LAYERS):
        layer = N_A_LAYERS + j
        parts = [_attn_group(qs[g], kvb_p[g], bias_p[g], g) for g in range(N_GROUPS)]
        next_q = (g_mix, wq, e_mat, gq) if j + 1 < N_B_LAYERS else None
        xp, qs = _attn_out_prompt(xp, parts, ex_mat, layer, j, wo, g_mlp, wup, wdn, next_q)
    y_prompt = xp
    conv_prompt = jnp.stack(conv_p, axis=0)
    kv_prompt = kv_p.reshape(n_p, win, 2, N_HEADS, HEAD_DIM)

    xs = jnp.transpose(x_sample, (1, 0, 2))
    st_in = jnp.transpose(state_conv, (0, 2, 1, 3))
    st_out = None
    for layer in range(N_A_LAYERS):
        xs, st_out = _conv_sample(xs, st_in, layer, conv_weights, st_out)
        xs = _mlp(xs.reshape(t_s * n_s, D_MODEL), layer, g_mlp, wup, wdn).reshape(t_s, n_s, D_MODEL)
    xs = xs.reshape(t_s * n_s, D_MODEL)
    kvn = _kv_proj_sample(xs, g_kv, wkv, e_mat, gk).reshape(t_s, n_s, KV_W)
    for j in range(N_B_LAYERS):
        layer = N_A_LAYERS + j
        q = _q_proj_sample(xs, layer, j, g_mix, wq, e_mat, gq)
        o = _attn_sample(q.reshape(t_s, n_s, N_GROUPS * HEADS_W), kvn, recent, far, bias_s)
        xs = _attn_out_sample(xs, o.reshape(t_s * n_s, HEADS_W), layer, j, wo, g_mlp, wup, wdn)
    y_sample = jnp.transpose(xs.reshape(t_s, n_s, D_MODEL), (1, 0, 2))
    conv_sample = jnp.transpose(st_out, (0, 2, 1, 3))
    kv_sample = jnp.transpose(kvn, (1, 0, 2)).reshape(n_s, t_s, 2, N_HEADS, HEAD_DIM)

    return (y_prompt, y_sample, conv_prompt, conv_sample, kv_prompt, kv_sample)
```

```python
import functools

import jax
import jax.numpy as jnp
from jax import lax
from jax.experimental import pallas as pl
from jax.experimental.pallas import tpu as pltpu

F32 = jnp.float32
BF16 = jnp.bfloat16

D_MODEL = 1024
CONV_CH = 1024
CONV_WIDTH = 31
CONV_HIST = CONV_WIDTH - 1
HIST_PAD = 32
D_FF = 4096
N_HEADS = 8
HEAD_DIM = 64
HEADS_W = N_HEADS * HEAD_DIM
KV_W = 2 * HEADS_W
WINDOWS = ((128, 1), (512, 4), (2048, 16))
N_GROUPS = len(WINDOWS)
BAND = 128
N_A_LAYERS = 2
N_B_LAYERS = 2
NORM_EPS = 1e-6
NEG_INF = float("-inf")
LOG2_E = 1.4426950408889634
LANES = 128
SUBLANES = 8

ROW_TILE = 512
ATT_OUT_TILE = 512
FF_CHUNK = 1024
ATT_QB = 1024
SAMPLE_CONV_NB = 32
SAMPLE_ROWS = 8
SAMPLE_SEQ_STEP = 4
MXU_DIM = 256
COMPACT_VMEM_MIB = 24
MIB = 1024 * 1024

assert all(w // d == BAND for w, d in WINDOWS)


def _const_spec(shape):
    nd = len(shape)
    return pl.BlockSpec(shape, lambda *_: (0,) * nd, pipeline_mode=pl.Buffered(1))


def _layer_spec(stacked, layer):
    return pl.BlockSpec((None,) + stacked.shape[1:], lambda *_: (layer, 0, 0),
                        pipeline_mode=pl.Buffered(1))


def _params(semantics, vmem_mib):
    return pltpu.CompilerParams(dimension_semantics=semantics, vmem_limit_bytes=vmem_mib * MIB)


def _lane_block(s, width=LANES):
    return slice(s * width, (s + 1) * width)


def _rms(x, g):
    return x * lax.rsqrt(jnp.mean(x * x, axis=-1, keepdims=True) + NORM_EPS) * g


def _mlp_residual(x, g_ref, wup_ref, wdn_ref):
    h = _rms(x, g_ref[...]).astype(BF16)
    acc = x
    for c in range(D_FF // FF_CHUNK):
        sl = _lane_block(c, FF_CHUNK)
        z = jnp.maximum(jnp.dot(h, wup_ref[:, sl], preferred_element_type=F32), 0.0)
        acc = acc + jnp.dot((z * z).astype(BF16), wdn_ref[sl, :], preferred_element_type=F32)
    return acc


def _head_mean_sq(v, e_ref):
    return jnp.dot((v * v).astype(BF16), e_ref[...], preferred_element_type=F32)


def _dot_t(a, b):
    return lax.dot_general(a, b, (((1,), (1,)), ((), ())), preferred_element_type=F32)


def _deinterleave_store(val, slab_ref, out_ref, d):
    rows, width = val.shape
    for s in range(width // LANES):
        slab_ref[s] = val[:, _lane_block(s)]
    for s in range(width // LANES):
        for r in range(d):
            out_ref[r, :, _lane_block(s)] = slab_ref[s, pl.ds(r, rows // d, stride=d), :].astype(BF16)


def _mlp_kernel(x_ref, g_ref, wup_ref, wdn_ref, o_ref):
    o_ref[...] = _mlp_residual(x_ref[...], g_ref, wup_ref, wdn_ref)


def _mlp(x, layer, g, wup, wdn, job=None):
    m = x.shape[0]
    row = pl.BlockSpec((ROW_TILE, D_MODEL), lambda i: (i, 0))
    return _call_with_compaction(
        _mlp_kernel, job, m // ROW_TILE, lambda i: i,
        out_shape=[jax.ShapeDtypeStruct((m, D_MODEL), F32)],
        grid=(m // ROW_TILE,),
        in_specs=[row] + [_layer_spec(a, layer) for a in (g, wup, wdn)],
        out_specs=[row],
        args=(x, g, wup, wdn),
        semantics=("arbitrary",), vmem_mib=34, name="mlp",
    )[0]


def _glu(h, w1_ref, b1_ref):
    u = jnp.dot(h, w1_ref[...], preferred_element_type=F32) + b1_ref[...]
    return u[:, :CONV_CH] * jax.nn.sigmoid(u[:, CONV_CH:])


def _conv_tail(c, x, bdw_ref, lng_ref, lnb_ref, w2_ref, b2_ref):
    c = c + bdw_ref[...]
    mu = jnp.mean(c, axis=-1, keepdims=True)
    xc = c - mu
    y = xc * lax.rsqrt(jnp.mean(xc * xc, axis=-1, keepdims=True) + NORM_EPS)
    y = y * lng_ref[...] + lnb_ref[...]
    y = y * jax.nn.sigmoid(y)
    out = jnp.dot(y.astype(BF16), w2_ref[...], preferred_element_type=F32) + b2_ref[...]
    return x + out


def _conv_prompt_kernel(x_ref, st_ref, g_ref, w1_ref, b1_ref, wdw_ref, bdw_ref, lng_ref, lnb_ref,
                        w2_ref, b2_ref, o_ref, sto_ref, buf_ref):
    n_slabs = CONV_CH // LANES

    @pl.when(pl.program_id(1) == 0)
    def _():
        for s in range(n_slabs):
            buf_ref[s, 0:HIST_PAD, :] = st_ref[:, _lane_block(s)]

    x = x_ref[...]
    h = _rms(x, g_ref[...]).astype(BF16)
    u = _glu(h, w1_ref, b1_ref)
    for s in range(n_slabs):
        buf_ref[s, HIST_PAD:HIST_PAD + ROW_TILE, :] = u[:, _lane_block(s)]
    off = HIST_PAD - CONV_HIST
    cs = []
    for s in range(n_slabs):
        ls = _lane_block(s)
        c = buf_ref[s, off:off + ROW_TILE, :] * wdw_ref[0:1, ls]
        for k in range(1, CONV_WIDTH):
            c = c + buf_ref[s, off + k:off + k + ROW_TILE, :] * wdw_ref[k:k + 1, ls]
        cs.append(c)
    c = jnp.concatenate(cs, axis=1)
    o_ref[...] = _conv_tail(c, x, bdw_ref, lng_ref, lnb_ref, w2_ref, b2_ref)
    for s in range(n_slabs):
        last = buf_ref[s, ROW_TILE:ROW_TILE + HIST_PAD, :]
        sto_ref[:, _lane_block(s)] = last
        buf_ref[s, 0:HIST_PAD, :] = last


def _conv_prompt(x, st, layer, weights, job=None):
    n, t, _ = x.shape
    tiles = t // ROW_TILE
    row = pl.BlockSpec((None, ROW_TILE, D_MODEL), lambda b, i: (b, i, 0))
    st_spec = pl.BlockSpec((None, HIST_PAD, CONV_CH), lambda b, i: (b, 0, 0))
    return _call_with_compaction(
        _conv_prompt_kernel, job, n * tiles, lambda b, i: b * tiles + i,
        out_shape=[jax.ShapeDtypeStruct((n, t, D_MODEL), F32),
                   jax.ShapeDtypeStruct((n, HIST_PAD, CONV_CH), F32)],
        grid=(n, tiles),
        in_specs=[row, st_spec] + [_layer_spec(a, layer) for a in weights],
        out_specs=[row, st_spec],
        scratch_shapes=[pltpu.VMEM((CONV_CH // LANES, HIST_PAD + ROW_TILE, LANES), F32)],
        args=(x, st, *weights),
        semantics=("arbitrary", "arbitrary"), vmem_mib=32, name="conv_prompt",
    )


def _conv_sample_kernel(x_ref, st_ref, g_ref, w1_ref, b1_ref, wdw_ref, bdw_ref, lng_ref, lnb_ref,
                        w2_ref, b2_ref, *rest):
    *prev_refs, o_ref, sto_ref = rest
    t_new, nb, _ = x_ref.shape
    x = x_ref[...].reshape(t_new * nb, D_MODEL)
    h = _rms(x, g_ref[...]).astype(BF16)
    glu = _glu(h, w1_ref, b1_ref)
    new = [glu[t * nb:(t + 1) * nb, :] for t in range(t_new)]

    def full(j):
        return st_ref[j] if j < CONV_HIST else new[j - CONV_HIST]

    outs = []
    for t in range(t_new):
        c = full(t) * wdw_ref[0:1, :]
        for k in range(1, CONV_WIDTH):
            c = c + full(t + k) * wdw_ref[k:k + 1, :]
        outs.append(c)
    c = jnp.concatenate(outs, axis=0)
    o_ref[...] = _conv_tail(c, x, bdw_ref, lng_ref, lnb_ref, w2_ref, b2_ref).reshape(t_new, nb, D_MODEL)
    if prev_refs:
        for l, prev_ref in enumerate(prev_refs):
            sto_ref[l] = prev_ref[...]
        for j in range(CONV_HIST):
            sto_ref[len(prev_refs), j] = full(j + t_new)
    else:
        for j in range(CONV_HIST):
            sto_ref[j] = full(j + t_new)


def _conv_sample(x, st_all, layer, weights, prev_states=()):
    t_new, n, _ = x.shape
    nb = SAMPLE_CONV_NB
    x_spec = pl.BlockSpec((t_new, nb, D_MODEL), lambda i: (0, i, 0))
    st_spec = pl.BlockSpec((None, CONV_HIST, nb, CONV_CH), lambda i: (layer, 0, i, 0))
    one_state = pl.BlockSpec((CONV_HIST, nb, CONV_CH), lambda i: (0, i, 0))
    if prev_states:
        n_stack = len(prev_states) + 1
        st_sds = jax.ShapeDtypeStruct((n_stack, CONV_HIST, n, CONV_CH), F32)
        sto_spec = pl.BlockSpec((n_stack, CONV_HIST, nb, CONV_CH), lambda i: (0, 0, i, 0))
    else:
        st_sds = jax.ShapeDtypeStruct((CONV_HIST, n, CONV_CH), F32)
        sto_spec = one_state
    return pl.pallas_call(
        _conv_sample_kernel,
        out_shape=(jax.ShapeDtypeStruct((t_new, n, D_MODEL), F32), st_sds),
        grid=(n // nb,),
        in_specs=[x_spec, st_spec] + [_layer_spec(a, layer) for a in weights] + [one_state] * len(prev_states),
        out_specs=(x_spec, sto_spec),
        compiler_params=_params(("arbitrary",), 48),
        name="conv_sample",
    )(x, st_all, *weights, *prev_states)


def _kv_rows(x_ref, g_ref, wkv_ref, e_ref, gk_ref):
    h = _rms(x_ref[...], g_ref[...]).astype(BF16)
    k = jnp.dot(h, wkv_ref[:, :HEADS_W], preferred_element_type=F32)
    v = jnp.dot(h, wkv_ref[:, HEADS_W:], preferred_element_type=F32)
    k = k * lax.rsqrt(_head_mean_sq(k, e_ref) + NORM_EPS) * gk_ref[...]
    return jnp.concatenate([k, v], axis=1)


def _kv_sample_kernel(x_ref, g_ref, wkv_ref, e_ref, gk_ref, kv_ref):
    kv_ref[...] = _kv_rows(x_ref, g_ref, wkv_ref, e_ref, gk_ref)


def _kvq_prompt_kernel(x_ref, g_ref, wkv_ref, e_ref, gk_ref, gm_ref, wq_ref, gq_ref,
                       kv_ref, kb0_ref, kb1_ref, kb2_ref, q0_ref, q1_ref, q2_ref, slab_ref):
    kv = _kv_rows(x_ref, g_ref, wkv_ref, e_ref, gk_ref)
    kv_ref[...] = kv
    kb0_ref[0] = kv.astype(BF16)
    _deinterleave_store(kv, slab_ref, kb1_ref, WINDOWS[1][1])
    _deinterleave_store(kv, slab_ref, kb2_ref, WINDOWS[2][1])
    h = _rms(x_ref[...], gm_ref[...]).astype(BF16)
    _q_prompt_store(h, wq_ref, e_ref, gq_ref, q0_ref, q1_ref, q2_ref, slab_ref)


def _proj_weight_specs(layer, g, w, e, gh):
    return [_layer_spec(g, layer), _layer_spec(w, layer), _const_spec(e.shape), _layer_spec(gh, layer)]


def _kv_proj_sample(x, g, wkv, e, gk):
    m = x.shape[0]
    return pl.pallas_call(
        _kv_sample_kernel,
        out_shape=jax.ShapeDtypeStruct((m, KV_W), F32),
        grid=(m // ROW_TILE,),
        in_specs=[pl.BlockSpec((ROW_TILE, D_MODEL), lambda i: (i, 0))] + _proj_weight_specs(0, g, wkv, e, gk),
        out_specs=pl.BlockSpec((ROW_TILE, KV_W), lambda i: (i, 0)),
        compiler_params=_params(("arbitrary",), 32),
        name="kv_proj_sample",
    )(x, g, wkv, e, gk)


def _dilated_shapes(n, t, width, dtype):
    sds, specs = [], []
    for _, d in WINDOWS:
        sds.append(jax.ShapeDtypeStruct((n, d, t // d, width), dtype))
        specs.append(pl.BlockSpec((None, d, ROW_TILE // d, width), lambda b, i: (b, 0, i, 0)))
    return sds, specs


def _kvq_proj_prompt(x, g, wkv, e, gk, win, layer, g_mix, wq, gq):
    n, t, _ = x.shape
    kv_sds, kv_specs = _dilated_shapes(n, t, KV_W, BF16)
    q_sds, q_specs = _dilated_shapes(n, t, HEADS_W, BF16)
    skip = (t - win) // ROW_TILE
    assert win % ROW_TILE == 0 and t % ROW_TILE == 0
    f32_spec = pl.BlockSpec((None, ROW_TILE, KV_W), lambda b, i: (b, jnp.maximum(i - skip, 0), 0))
    res = pl.pallas_call(
        _kvq_prompt_kernel,
        out_shape=[jax.ShapeDtypeStruct((n, win, KV_W), F32)] + kv_sds + q_sds,
        grid=(n, t // ROW_TILE),
        in_specs=[pl.BlockSpec((None, ROW_TILE, D_MODEL), lambda b, i: (b, i, 0))]
        + _proj_weight_specs(0, g, wkv, e, gk)
        + [_layer_spec(g_mix, layer), _layer_spec(wq, 0), _layer_spec(gq, 0)],
        out_specs=[f32_spec] + kv_specs + q_specs,
        scratch_shapes=[pltpu.VMEM((KV_W // LANES, ROW_TILE, LANES), F32)],
        compiler_params=_params(("arbitrary", "arbitrary"), 48),
        name="kvq_proj_prompt",
    )(x, g, wkv, e, gk, g_mix, wq, gq)
    return res[0], res[1:4], res[4:7]


def _q_group(h, wq_ref, e_ref, gq_ref, grp):
    q = jnp.dot(h, wq_ref[:, _lane_block(grp, HEADS_W)], preferred_element_type=F32)
    return q * lax.rsqrt(_head_mean_sq(q, e_ref) + NORM_EPS) * gq_ref[...]


def _q_sample_kernel(x_ref, g_ref, wq_ref, e_ref, gq_ref, q_ref):
    h = _rms(x_ref[...], g_ref[...]).astype(BF16)
    for grp in range(N_GROUPS):
        q = _q_group(h, wq_ref, e_ref, gq_ref, grp)
        q_ref[:, _lane_block(grp, HEADS_W)] = q.astype(BF16).astype(F32)


def _q_prompt_store(h, wq_ref, e_ref, gq_ref, q0_ref, q1_ref, q2_ref, slab_ref):
    q0_ref[0] = _q_group(h, wq_ref, e_ref, gq_ref, 0).astype(BF16)
    _deinterleave_store(_q_group(h, wq_ref, e_ref, gq_ref, 1), slab_ref, q1_ref, WINDOWS[1][1])
    _deinterleave_store(_q_group(h, wq_ref, e_ref, gq_ref, 2), slab_ref, q2_ref, WINDOWS[2][1])


def _q_proj_sample(x, layer, j, g, wq, e, gq):
    m = x.shape[0]
    qw = N_GROUPS * HEADS_W
    specs = [_layer_spec(g, layer), _layer_spec(wq, j), _const_spec(e.shape), _layer_spec(gq, j)]
    return pl.pallas_call(
        _q_sample_kernel,
        out_shape=jax.ShapeDtypeStruct((m, qw), F32),
        grid=(m // ROW_TILE,),
        in_specs=[pl.BlockSpec((ROW_TILE, D_MODEL), lambda i: (i, 0))] + specs,
        out_specs=pl.BlockSpec((ROW_TILE, qw), lambda i: (i, 0)),
        compiler_params=_params(("arbitrary",), 32),
        name="q_proj_sample",
    )(x, g, wq, e, gq)


def _attn_group_kernel(q_ref, kc_ref, kp_ref, vc_ref, vp_ref, bias_ref, acc_ref, m_ref, l_ref,
                       kk_ref, vv_ref):
    kk_ref[0:BAND, :] = kp_ref[...]
    kk_ref[BAND:, :] = kc_ref[...]
    vv_ref[0:BAND, :] = vp_ref[...]
    vv_ref[BAND:, :] = vc_ref[...]
    first_block = pl.program_id(2) == 0
    key_col = lax.broadcasted_iota(jnp.int32, (2 * BAND, 2 * BAND), 1)
    lane = lax.broadcasted_iota(jnp.int32, (BAND, LANES), 1)
    left = lane < HEAD_DIM
    lane_row = lax.broadcasted_iota(jnp.int32, (1, LANES), 1)
    head_keep = [jnp.where(lane_row < HEAD_DIM, 1.0, 0.0).astype(BF16),
                 jnp.where(lane_row < HEAD_DIM, 0.0, 1.0).astype(BF16)]
    no_history = jnp.logical_and(first_block, key_col < BAND)

    for sb in range(q_ref.shape[0] // BAND):
        r0 = sb * BAND
        m_tile = jnp.zeros((BAND, LANES), F32)
        l_tile = jnp.ones((BAND, LANES), F32)
        for pair in range(N_HEADS // 2):
            ls = _lane_block(pair)
            q2 = q_ref[r0:r0 + BAND, ls]
            k2 = kk_ref[r0:r0 + 2 * BAND, ls]
            v2 = vv_ref[r0:r0 + 2 * BAND, ls]
            qq = jnp.concatenate([q2 * head_keep[0], q2 * head_keep[1]], axis=0)
            s = _dot_t(qq, k2) + bias_ref[pair]
            if sb == 0:
                s = jnp.where(no_history, NEG_INF, s)
            m = jnp.max(s, axis=-1, keepdims=True)
            p = jnp.exp2(s - m)
            l = jnp.sum(p, axis=-1, keepdims=True)
            a = jnp.dot(p.astype(BF16), v2, preferred_element_type=F32)
            acc_ref[r0:r0 + BAND, ls] = jnp.where(left, a[:BAND], a[BAND:])
            for hh in range(2):
                rows = slice(hh * BAND, (hh + 1) * BAND)
                m_tile = jnp.where(lane == 2 * pair + hh, m[rows], m_tile)
                l_tile = jnp.where(lane == 2 * pair + hh, l[rows], l_tile)
        m_ref[r0:r0 + BAND, :] = m_tile
        l_ref[r0:r0 + BAND, :] = l_tile


def _attn_group(q, kvb, bias, grp):
    n, d, rows, _ = q.shape
    qb = min(ATT_QB, rows)
    per_blk = qb // BAND
    stat_sds = jax.ShapeDtypeStruct((n, d, rows, LANES), F32)
    stat_spec = pl.BlockSpec((None, None, qb, LANES), lambda b, r, i: (b, r, i, 0))

    def prev_idx(i):
        return jnp.maximum(i * per_blk - 1, 0)

    def cur(col):
        return pl.BlockSpec((None, None, qb, HEADS_W), lambda b, r, i: (b, r, i, col))

    def prev(col):
        return pl.BlockSpec((None, None, BAND, HEADS_W), lambda b, r, i: (b, r, prev_idx(i), col))

    return pl.pallas_call(
        _attn_group_kernel,
        out_shape=(jax.ShapeDtypeStruct((n, d, rows, HEADS_W), F32), stat_sds, stat_sds),
        grid=(n, d, rows // qb),
        in_specs=[cur(0), cur(0), prev(0), cur(1), prev(1),
                  _const_spec((N_HEADS // 2, 2 * BAND, 2 * BAND))],
        out_specs=(cur(0), stat_spec, stat_spec),
        scratch_shapes=[pltpu.VMEM((BAND + qb, HEADS_W), BF16),
                        pltpu.VMEM((BAND + qb, HEADS_W), BF16)],
        compiler_params=_params(("arbitrary", "arbitrary", "arbitrary"), 32),
        name=f"attn_group{grp}",
    )(q, kvb, kvb, kvb, kvb, bias)


def _compact_cache_kernel(c_ref, eye_ref, sel_ref, recent_ref, far_ref, *, t_new):
    p = c_ref.shape[-1]
    d1, d2 = WINDOWS[1][1], WINDOWS[2][1]
    x = c_ref[...].reshape(KV_W, p).astype(BF16)
    lo = p - d1 * BAND
    for j in range(d1 * BAND // MXU_DIM):
        cols = x[:, lo + j * MXU_DIM:lo + (j + 1) * MXU_DIM]
        recent_ref[j * MXU_DIM:(j + 1) * MXU_DIM, :] = _dot_t(eye_ref[...], cols).astype(BF16)
    per = MXU_DIM // d2
    for c in range(p // MXU_DIM):
        picked = _dot_t(sel_ref[...], x[:, c * MXU_DIM:(c + 1) * MXU_DIM]).astype(BF16)
        for t in range(t_new):
            far_ref[t * BAND + c * per:t * BAND + (c + 1) * per, :] = picked[t * per:(t + 1) * per, :]


def _compaction_job(cache_kv, t_new):
    n, p = cache_kv.shape[:2]
    d1, d2 = WINDOWS[1][1], WINDOWS[2][1]
    per = MXU_DIM // d2
    assert p == WINDOWS[2][0] and p // d2 == BAND and t_new <= d1 and p % MXU_DIM == 0
    feat_major = jnp.transpose(cache_kv, (0, 2, 3, 4, 1))
    eye = jnp.eye(MXU_DIM, dtype=BF16)
    row = jnp.arange(t_new * per)[:, None]
    col = jnp.arange(MXU_DIM)[None, :]
    sel = jnp.where(col == (row % per) * d2 + row // per, 1.0, 0.0).astype(BF16)
    return dict(inputs=(feat_major, eye, sel), t_new=t_new, n=n, next=0, chunks=[])


def _fused_body(*refs, body, n_in, n_out, t_new):
    c_ref, eye_ref, sel_ref = refs[n_in:n_in + 3]
    rest = refs[n_in + 3:]
    recent_ref, far_ref = rest[n_out:n_out + 2]
    body(*refs[:n_in], *rest[:n_out], *rest[n_out + 2:])
    _compact_cache_kernel(c_ref, eye_ref, sel_ref, recent_ref, far_ref, t_new=t_new)


def _call_with_compaction(body, job, steps, step_of, *, grid, in_specs, out_specs, out_shape, args,
                          semantics, vmem_mib, name, scratch_shapes=()):
    if job is None:
        return pl.pallas_call(body, out_shape=out_shape, grid=grid, in_specs=in_specs, out_specs=out_specs,
                              scratch_shapes=scratch_shapes, compiler_params=_params(semantics, vmem_mib),
                              name=name)(*args)
    base, t_new = job["next"], job["t_new"]
    assert base + steps <= job["n"]
    job["next"] = base + steps
    feat, eye, sel = job["inputs"]
    n_out = len(out_shape)
    rows = (WINDOWS[1][1] * BAND, t_new * BAND)
    res = pl.pallas_call(
        functools.partial(_fused_body, body=body, n_in=len(args), n_out=n_out, t_new=t_new),
        out_shape=list(out_shape) + [jax.ShapeDtypeStruct((steps, r, KV_W), BF16) for r in rows],
        grid=grid,
        in_specs=list(in_specs)
        + [pl.BlockSpec((None,) + feat.shape[1:], lambda *idx: (base + step_of(*idx), 0, 0, 0, 0)),
           _const_spec(eye.shape), _const_spec(sel.shape)],
        out_specs=list(out_specs)
        + [pl.BlockSpec((None, r, KV_W), lambda *idx: (step_of(*idx), 0, 0)) for r in rows],
        scratch_shapes=scratch_shapes,
        compiler_params=_params(semantics, vmem_mib + COMPACT_VMEM_MIB),
        name=name,
    )(*args, feat, eye, sel)
    job["chunks"].append((base, *res[n_out:]))
    return res[:n_out]


def _attn_sample_kernel(q_ref, kvn_ref, recent_ref, far_ref, b0_ref, b1_ref, b2_ref, b2n_ref, o_ref):
    t_new = q_ref.shape[0]
    n_seq = SAMPLE_SEQ_STEP
    base = (pl.program_id(0) % (SAMPLE_ROWS // n_seq)) * n_seq
    n_rows = t_new * N_HEADS
    hsel = (lax.broadcasted_iota(jnp.int32, (N_HEADS, HEADS_W), 1) // HEAD_DIM
            == lax.broadcasted_iota(jnp.int32, (N_HEADS, HEADS_W), 0))
    row8 = lax.broadcasted_iota(jnp.int32, (SUBLANES, KV_W), 0)

    qs, kas, vas, kbs, vbs = [], [], [], [], []
    for j in range(n_seq):
        b = base + j
        qs.append([jnp.concatenate(
            [jnp.where(hsel, q_ref[t, pl.ds(b, 1), _lane_block(grp, HEADS_W)], 0.0) for t in range(t_new)],
            axis=0).astype(BF16) for grp in range(N_GROUPS)])
        new8 = jnp.zeros((SUBLANES, KV_W), F32)
        for t in range(t_new):
            new8 = jnp.where(row8 == t, kvn_ref[t, pl.ds(b, 1), :], new8)
        new = jnp.concatenate([new8, jnp.zeros((BAND - SUBLANES, KV_W), F32)], axis=0).astype(BF16)
        recent = recent_ref[j]
        kas.append(jnp.concatenate([recent[:, :HEADS_W], new[:, :HEADS_W]], axis=0))
        vas.append(jnp.concatenate([recent[:, HEADS_W:], new[:, HEADS_W:]], axis=0))
        kbs.append(far_ref[j, :, :HEADS_W])
        vbs.append(far_ref[j, :, HEADS_W:])
    n_a = kas[0].shape[0]

    def scores(grp, keys, lo, bias_ref):
        return jnp.concatenate([_dot_t(qs[j][grp], keys[j][lo:]) for j in range(n_seq)],
                               axis=0) + bias_ref[...]

    def softmax_pv(pieces):
        m = functools.reduce(jnp.maximum, [jnp.max(s, axis=-1, keepdims=True) for s, _, _ in pieces])
        l = jnp.zeros((n_seq * n_rows, 1), F32)
        acc = jnp.zeros((n_seq * n_rows, HEADS_W), F32)
        for s, vals, lo in pieces:
            p = jnp.exp2(s - m)
            l = l + jnp.sum(p, axis=-1, keepdims=True)
            pb = p.astype(BF16)
            acc = acc + jnp.concatenate(
                [jnp.dot(pb[j * n_rows:(j + 1) * n_rows], vals[j][lo:], preferred_element_type=F32)
                 for j in range(n_seq)], axis=0)
        return acc, m, l

    lo0, lo_new = n_a - 2 * BAND, n_a - BAND
    parts = [
        softmax_pv([(scores(0, kas, lo0, b0_ref), vas, lo0)]),
        softmax_pv([(scores(1, kas, 0, b1_ref), vas, 0)]),
        softmax_pv([(scores(2, kbs, 0, b2_ref), vbs, 0), (scores(2, kas, lo_new, b2n_ref), vas, lo_new)]),
    ]
    m_all = functools.reduce(jnp.maximum, [m for _, m, _ in parts])
    num = jnp.zeros((n_seq * n_rows, HEADS_W), F32)
    den = jnp.zeros((n_seq * n_rows, 1), F32)
    for acc, m, l in parts:
        e = jnp.exp2(m - m_all)
        num = num + e * acc
        den = den + e * l
    out = num * (1.0 / den)
    for j in range(n_seq):
        for t in range(t_new):
            r0 = j * n_rows + t * N_HEADS
            own = jnp.where(hsel, out[r0:r0 + N_HEADS, :], 0.0)
            o_ref[t, pl.ds(base + j, 1), :] = jnp.sum(own, axis=0, keepdims=True)


def _attn_sample(q, kvn, chunks, biases):
    t_new, n, qw = q.shape
    step = SAMPLE_SEQ_STEP
    per_blk = SAMPLE_ROWS // step
    biases = [jnp.tile(b, (step, 1)) for b in biases]
    outs = []
    for first, recent, far in chunks:
        count = recent.shape[0]
        assert first % SAMPLE_ROWS == 0 and count % SAMPLE_ROWS == 0
        blk0 = first // SAMPLE_ROWS
        blk = lambda width: pl.BlockSpec((t_new, SAMPLE_ROWS, width), lambda i: (0, blk0 + i // per_blk, 0))
        rows = lambda a: pl.BlockSpec((step,) + a.shape[1:], lambda i: (i, 0, 0))
        outs.append(pl.pallas_call(
            _attn_sample_kernel,
            out_shape=jax.ShapeDtypeStruct((t_new, count, HEADS_W), F32),
            grid=(count // step,),
            in_specs=[blk(qw), blk(KV_W), rows(recent), rows(far)] + [_const_spec(b.shape) for b in biases],
            out_specs=pl.BlockSpec((t_new, SAMPLE_ROWS, HEADS_W), lambda i: (0, i // per_blk, 0)),
            compiler_params=_params(("arbitrary",), 40),
            name="attn_sample",
        )(q, kvn, recent, far, *biases))
    return jnp.concatenate(outs, axis=1)


def _attn_out_sample_kernel(x_ref, o_ref, wo_ref, g_ref, wup_ref, wdn_ref, out_ref):
    x = x_ref[...] + jnp.dot(o_ref[...].astype(BF16), wo_ref[...], preferred_element_type=F32)
    out_ref[...] = _mlp_residual(x, g_ref, wup_ref, wdn_ref)


def _attn_out_prompt_kernel(x_ref, a0_ref, m0_ref, l0_ref, a1_ref, m1_ref, l1_ref, a2_ref, m2_ref, l2_ref,
                            ex_ref, wo_ref, g_ref, wup_ref, wdn_ref, out_ref, slab_ref):
    rows = x_ref.shape[0]
    n_o = HEADS_W // LANES
    for gi, (a_ref, m_ref, l_ref) in enumerate(((a1_ref, m1_ref, l1_ref), (a2_ref, m2_ref, l2_ref))):
        d = WINDOWS[gi + 1][1]
        for r in range(d):
            dst = pl.ds(r, rows // d, stride=d)
            for s in range(n_o):
                slab_ref[gi, s, dst, :] = a_ref[r, :, _lane_block(s)]
            slab_ref[gi, n_o, dst, :] = m_ref[r]
            slab_ref[gi, n_o + 1, dst, :] = l_ref[r]
    accs = [a0_ref[0]] + [jnp.concatenate([slab_ref[gi, s] for s in range(n_o)], axis=1) for gi in range(2)]
    ms = [m0_ref[0], slab_ref[0, n_o], slab_ref[1, n_o]]
    ls = [l0_ref[0], slab_ref[0, n_o + 1], slab_ref[1, n_o + 1]]
    mx = functools.reduce(jnp.maximum, ms)
    es = [jnp.exp2(m - mx) for m in ms]
    inv = 1.0 / functools.reduce(jnp.add, [e * l for e, l in zip(es, ls)])
    o = jnp.zeros((rows, HEADS_W), F32)
    for e, og in zip(es, accs):
        w = e * inv
        hi = w.astype(BF16)
        lo = (w - hi.astype(F32)).astype(BF16)
        wide = (jnp.dot(hi, ex_ref[...], preferred_element_type=F32)
                + jnp.dot(lo, ex_ref[...], preferred_element_type=F32))
        o = o + wide * og
    x = x_ref[...] + jnp.dot(o.astype(BF16), wo_ref[...], preferred_element_type=F32)
    out_ref[...] = _mlp_residual(x, g_ref, wup_ref, wdn_ref)


def _attn_out_q_prompt_kernel(*refs):
    n_in = 15
    gm_ref, wq_ref, e_ref, gq_ref = refs[n_in:n_in + 4]
    out_ref, q0_ref, q1_ref, q2_ref, slab_ref, qslab_ref = refs[n_in + 4:]
    _attn_out_prompt_kernel(*refs[:n_in], out_ref, slab_ref)
    h = _rms(out_ref[...], gm_ref[...]).astype(BF16)
    _q_prompt_store(h, wq_ref, e_ref, gq_ref, q0_ref, q1_ref, q2_ref, qslab_ref)


def _mlp_weight_specs(layer, j, wo, g, wup, wdn):
    return [_layer_spec(wo, j)] + [_layer_spec(a, layer) for a in (g, wup, wdn)]


def _attn_out_sample(x, o, layer, j, wo, g, wup, wdn):
    m = x.shape[0]
    row = pl.BlockSpec((ROW_TILE, D_MODEL), lambda i: (i, 0))
    return pl.pallas_call(
        _attn_out_sample_kernel,
        out_shape=jax.ShapeDtypeStruct((m, D_MODEL), F32),
        grid=(m // ROW_TILE,),
        in_specs=[row, pl.BlockSpec((ROW_TILE, HEADS_W), lambda i: (i, 0))]
        + _mlp_weight_specs(layer, j, wo, g, wup, wdn),
        out_specs=row,
        compiler_params=_params(("arbitrary",), 48),
        name="attn_out_sample",
    )(x, o, wo, g, wup, wdn)


def _attn_out_prompt(x, parts, ex, layer, j, wo, g, wup, wdn, next_q=None):
    n, t, _ = x.shape
    tile = ATT_OUT_TILE
    assert tile == ROW_TILE
    row = pl.BlockSpec((None, tile, D_MODEL), lambda b, i: (b, i, 0))
    part_specs = []
    for _, d in WINDOWS:
        for width in (HEADS_W, LANES, LANES):
            part_specs.append(pl.BlockSpec((None, d, tile // d, width), lambda b, i: (b, 0, i, 0)))
    flat = [a for part in parts for a in part]
    in_specs = ([row] + part_specs + [_const_spec((LANES, HEADS_W))]
                + _mlp_weight_specs(layer, j, wo, g, wup, wdn))
    args = [x, *flat, ex, wo, g, wup, wdn]
    out_shape, out_specs = [jax.ShapeDtypeStruct((n, t, D_MODEL), F32)], [row]
    scratch = [pltpu.VMEM((2, HEADS_W // LANES + 2, tile, LANES), F32)]
    body = _attn_out_prompt_kernel
    if next_q is not None:
        g_mix, wq, e, gq = next_q
        in_specs += [_layer_spec(g_mix, layer + 1), _layer_spec(wq, j + 1), _const_spec(e.shape),
                     _layer_spec(gq, j + 1)]
        args += [g_mix, wq, e, gq]
        q_sds, q_specs = _dilated_shapes(n, t, HEADS_W, BF16)
        out_shape, out_specs = out_shape + q_sds, out_specs + q_specs
        scratch.append(pltpu.VMEM((HEADS_W // LANES, tile, LANES), F32))
        body = _attn_out_q_prompt_kernel
    res = pl.pallas_call(
        body,
        out_shape=out_shape,
        grid=(n, t // tile),
        in_specs=in_specs,
        out_specs=out_specs,
        scratch_shapes=scratch,
        compiler_params=_params(("arbitrary", "arbitrary"), 56),
        name="attn_out_prompt",
    )(*args)
    return res[0], res[1:]


def _alibi_slopes():
    n = N_GROUPS * N_HEADS
    s = 2.0 ** (-8.0 * jnp.arange(1, n + 1, dtype=F32) / n)
    return s.reshape(N_GROUPS, N_HEADS)


def _prompt_bias(slopes, grp):
    d = WINDOWS[grp][1]
    iq = jnp.arange(BAND, dtype=jnp.int32)[:, None]
    ik = jnp.arange(2 * BAND, dtype=jnp.int32)[None, :]
    steps = iq + BAND - ik
    valid = (steps >= 0) & (steps <= BAND)
    offs = (steps * d).astype(F32)
    per_head = jnp.where(valid[None], -slopes[grp][:, None, None] * offs[None] * LOG2_E, NEG_INF)
    return per_head.reshape(N_HEADS // 2, 2 * BAND, 2 * BAND)


def _sample_bias(slopes, t_new, past):
    d1, d2 = WINDOWS[1][1], WINDOWS[2][1]
    t = jnp.arange(t_new, dtype=jnp.int32)[:, None]

    def table(slope, dist, valid):
        b = jnp.where(valid[:, None, :], -slope[None, :, None] * dist.astype(F32)[:, None, :] * LOG2_E,
                      NEG_INF)
        return b.reshape(t_new * N_HEADS, -1)

    u = jnp.arange(BAND, dtype=jnp.int32)[None, :]
    dist_n = t - u
    ok_n = (u < t_new) & (dist_n >= 0)
    c = jnp.arange(BAND, dtype=jnp.int32)[None, :]
    dist_c = BAND + t - c
    b0 = table(slopes[0], jnp.concatenate([dist_c, dist_n], axis=1),
               jnp.concatenate([dist_c <= WINDOWS[0][0], ok_n], axis=1))
    c = jnp.arange(d1 * BAND, dtype=jnp.int32)[None, :]
    dist_c = d1 * BAND + t - c
    ok_c = (dist_c % d1 == 0) & (dist_c <= WINDOWS[1][0])
    b1 = table(slopes[1], jnp.concatenate([dist_c, dist_n], axis=1),
               jnp.concatenate([ok_c, ok_n & (dist_n % d1 == 0)], axis=1))
    per = past // d2
    col = jnp.arange(t_new * per, dtype=jnp.int32)[None, :]
    pos = (col % per) * d2 + col // per
    dist_c = past + t - pos
    ok_c = (dist_c % d2 == 0) & (dist_c <= WINDOWS[2][0])
    b2 = table(slopes[2], dist_c, ok_c)
    b2n = table(slopes[2], dist_n, ok_n & (dist_n % d2 == 0))
    return b0, b1, b2, b2n


def _head_mean_matrix():
    idx = jnp.arange(HEADS_W) // HEAD_DIM
    return jnp.where(idx[:, None] == idx[None, :], 1.0 / HEAD_DIM, 0.0).astype(BF16)


def _head_expand_matrix():
    lane = jnp.arange(LANES)[:, None]
    col = jnp.arange(HEADS_W)[None, :] // HEAD_DIM
    return jnp.where(lane == col, 1.0, 0.0).astype(BF16)


def kernel(x_prompt, x_sample, state_conv, cache_kv, norm_mix_g, norm_mlp_g, conv_w_pw1, conv_b_pw1,
           conv_w_dw, conv_b_dw, conv_ln_g, conv_ln_b, conv_w_pw2, conv_b_pw2, kv_norm_g, w_kv,
           k_norm_g, attn_w_q, q_norm_g, attn_w_o, mlp_w_up, mlp_w_down):
    n_p, t_p, _ = x_prompt.shape
    n_s, t_s, _ = x_sample.shape
    past = cache_kv.shape[1]
    scale = HEAD_DIM ** -0.5 * LOG2_E

    w1 = conv_w_pw1.astype(BF16)
    w2 = conv_w_pw2.astype(BF16)
    wkv = w_kv.astype(BF16)
    wq = attn_w_q.astype(BF16)
    wo = attn_w_o.astype(BF16)
    wup = mlp_w_up.astype(BF16)
    wdn = mlp_w_down.astype(BF16)
    vec = lambda a: a.reshape(-1, 1, a.shape[-1])
    g_mix, g_mlp, g_kv = vec(norm_mix_g), vec(norm_mlp_g), vec(kv_norm_g)
    e_mat = _head_mean_matrix()
    ex_mat = _head_expand_matrix()
    gk = vec(jnp.tile(k_norm_g, N_HEADS))
    gq = vec(jnp.tile(q_norm_g, (1, N_HEADS)) * scale)
    wkv = wkv[None]
    slopes = _alibi_slopes()
    bias_p = [_prompt_bias(slopes, g) for g in range(N_GROUPS)]
    bias_s = _sample_bias(slopes, t_s, past)
    conv_weights = (g_mix, w1, vec(conv_b_pw1), conv_w_dw, vec(conv_b_dw), vec(conv_ln_g),
                    vec(conv_ln_b), w2, vec(conv_b_pw2))

    job = _compaction_job(cache_kv, t_s)
    xp = x_prompt
    conv_p = []
    zero_hist = jnp.zeros((n_p, HIST_PAD, CONV_CH), F32)
    for layer in range(N_A_LAYERS):
        xp, st = _conv_prompt(xp, zero_hist, layer, conv_weights, job)
        conv_p.append(st[:, HIST_PAD - CONV_HIST:])
        xp = _mlp(xp.reshape(n_p * t_p, D_MODEL), layer, g_mlp, wup, wdn, job).reshape(n_p, t_p, D_MODEL)
    assert job["next"] == job["n"]
    win = min(WINDOWS[-1][0], t_p)
    kv_p, kvb_p, qs = _kvq_proj_prompt(xp, g_kv, wkv, e_mat, gk, win, N_A_LAYERS, g_mix, wq, gq)
    for j in range(N_B_LAYERS):
        layer = N_A_LAYERS + j
        parts = [_attn_group(qs[g], kvb_p[g], bias_p[g], g) for g in range(N_GROUPS)]
        next_q = (g_mix, wq, e_mat, gq) if j + 1 < N_B_LAYERS else None
        xp, qs = _attn_out_prompt(xp, parts, ex_mat, layer, j, wo, g_mlp, wup, wdn, next_q)
    y_prompt = xp
    conv_prompt = jnp.stack(conv_p, axis=0)
    kv_prompt = kv_p.reshape(n_p, win, 2, N_HEADS, HEAD_DIM)

    xs = jnp.transpose(x_sample, (1, 0, 2))
    st_in = jnp.transpose(state_conv, (0, 2, 1, 3))
    new_states = []
    for layer in range(N_A_LAYERS):
        prev = new_states if layer == N_A_LAYERS - 1 else ()
        xs, st = _conv_sample(xs, st_in, layer, conv_weights, prev)
        new_states.append(st)
        xs = _mlp(xs.reshape(t_s * n_s, D_MODEL), layer, g_mlp, wup, wdn).reshape(t_s, n_s, D_MODEL)
    xs = xs.reshape(t_s * n_s, D_MODEL)
    kvn = _kv_proj_sample(xs, g_kv, wkv, e_mat, gk).reshape(t_s, n_s, KV_W)
    for j in range(N_B_LAYERS):
        layer = N_A_LAYERS + j
        q = _q_proj_sample(xs, layer, j, g_mix, wq, e_mat, gq)
        o = _attn_sample(q.reshape(t_s, n_s, N_GROUPS * HEADS_W), kvn, job["chunks"], bias_s)
        xs = _attn_out_sample(xs, o.reshape(t_s * n_s, HEADS_W), layer, j, wo, g_mlp, wup, wdn)
    y_sample = jnp.transpose(xs.reshape(t_s, n_s, D_MODEL), (1, 0, 2))
    st_stack = new_states[-1] if N_A_LAYERS > 1 else new_states[-1][None]
    conv_sample = jnp.transpose(st_stack, (0, 2, 1, 3))
    kv_sample = jnp.transpose(kvn, (1, 0, 2)).reshape(n_s, t_s, 2, N_HEADS, HEAD_DIM)

    return (y_prompt, y_sample, conv_prompt, conv_sample, kv_prompt, kv_sample)
```

```python
import functools

import jax
import jax.numpy as jnp
from jax import lax
from jax.experimental import pallas as pl
from jax.experimental.pallas import tpu as pltpu

F32 = jnp.float32
BF16 = jnp.bfloat16

D_MODEL = 1024
CONV_CH = 1024
CONV_WIDTH = 31
CONV_HIST = CONV_WIDTH - 1
HIST_PAD = 32
D_FF = 4096
N_HEADS = 8
HEAD_DIM = 64
HEADS_W = N_HEADS * HEAD_DIM
KV_W = 2 * HEADS_W
WINDOWS = ((128, 1), (512, 4), (2048, 16))
N_GROUPS = len(WINDOWS)
BAND = 128
N_A_LAYERS = 2
N_B_LAYERS = 2
NORM_EPS = 1e-6
NEG_INF = float("-inf")
LOG2_E = 1.4426950408889634
LANES = 128
SUBLANES = 8

ROW_TILE = 512
ATT_OUT_TILE = 512
FF_CHUNK = 1024
ATT_QB = 1024
SAMPLE_CONV_NB = 32
SAMPLE_ROWS = 8
SAMPLE_SEQ_STEP = 4
MXU_DIM = 256
COMPACT_VMEM_MIB = 24
MIB = 1024 * 1024

assert all(w // d == BAND for w, d in WINDOWS)


def _const_spec(shape):
    nd = len(shape)
    return pl.BlockSpec(shape, lambda *_: (0,) * nd, pipeline_mode=pl.Buffered(1))


def _layer_spec(stacked, layer):
    return pl.BlockSpec((None,) + stacked.shape[1:], lambda *_: (layer, 0, 0),
                        pipeline_mode=pl.Buffered(1))


def _params(semantics, vmem_mib):
    return pltpu.CompilerParams(dimension_semantics=semantics, vmem_limit_bytes=vmem_mib * MIB)


def _lane_block(s, width=LANES):
    return slice(s * width, (s + 1) * width)


def _rms(x, g):
    return x * lax.rsqrt(jnp.mean(x * x, axis=-1, keepdims=True) + NORM_EPS) * g


def _mlp_residual(x, g_ref, wup_ref, wdn_ref):
    h = _rms(x, g_ref[...]).astype(BF16)
    acc = x
    for c in range(D_FF // FF_CHUNK):
        sl = _lane_block(c, FF_CHUNK)
        z = jnp.maximum(jnp.dot(h, wup_ref[:, sl], preferred_element_type=F32), 0.0)
        acc = acc + jnp.dot((z * z).astype(BF16), wdn_ref[sl, :], preferred_element_type=F32)
    return acc


def _head_mean_sq(v, e_ref):
    return jnp.dot((v * v).astype(BF16), e_ref[...], preferred_element_type=F32)


def _dot_t(a, b):
    return lax.dot_general(a, b, (((1,), (1,)), ((), ())), preferred_element_type=F32)


def _deinterleave_store(val, slab_ref, out_ref, d):
    rows, width = val.shape
    for s in range(width // LANES):
        slab_ref[s] = val[:, _lane_block(s)]
    for s in range(width // LANES):
        for r in range(d):
            out_ref[r, :, _lane_block(s)] = slab_ref[s, pl.ds(r, rows // d, stride=d), :].astype(BF16)


def _mlp_kernel(x_ref, g_ref, wup_ref, wdn_ref, o_ref):
    o_ref[...] = _mlp_residual(x_ref[...], g_ref, wup_ref, wdn_ref)


def _mlp(x, layer, g, wup, wdn, job=None):
    m = x.shape[0]
    row = pl.BlockSpec((ROW_TILE, D_MODEL), lambda i: (i, 0))
    return _call_with_compaction(
        _mlp_kernel, job, m // ROW_TILE, lambda i: i,
        out_shape=[jax.ShapeDtypeStruct((m, D_MODEL), F32)],
        grid=(m // ROW_TILE,),
        in_specs=[row] + [_layer_spec(a, layer) for a in (g, wup, wdn)],
        out_specs=[row],
        args=(x, g, wup, wdn),
        semantics=("arbitrary",), vmem_mib=34, name="mlp",
    )[0]


def _glu(h, w1_ref, b1_ref):
    u = jnp.dot(h, w1_ref[...], preferred_element_type=F32) + b1_ref[...]
    return u[:, :CONV_CH] * jax.nn.sigmoid(u[:, CONV_CH:])


def _conv_tail(c, x, bdw_ref, lng_ref, lnb_ref, w2_ref, b2_ref):
    c = c + bdw_ref[...]
    mu = jnp.mean(c, axis=-1, keepdims=True)
    xc = c - mu
    y = xc * lax.rsqrt(jnp.mean(xc * xc, axis=-1, keepdims=True) + NORM_EPS)
    y = y * lng_ref[...] + lnb_ref[...]
    y = y * jax.nn.sigmoid(y)
    out = jnp.dot(y.astype(BF16), w2_ref[...], preferred_element_type=F32) + b2_ref[...]
    return x + out


def _conv_prompt_kernel(x_ref, st_ref, g_ref, w1_ref, b1_ref, wdw_ref, bdw_ref, lng_ref, lnb_ref,
                        w2_ref, b2_ref, o_ref, sto_ref, buf_ref):
    n_slabs = CONV_CH // LANES

    @pl.when(pl.program_id(1) == 0)
    def _():
        for s in range(n_slabs):
            buf_ref[s, 0:HIST_PAD, :] = st_ref[:, _lane_block(s)]

    x = x_ref[...]
    h = _rms(x, g_ref[...]).astype(BF16)
    u = _glu(h, w1_ref, b1_ref)
    for s in range(n_slabs):
        buf_ref[s, HIST_PAD:HIST_PAD + ROW_TILE, :] = u[:, _lane_block(s)]
    off = HIST_PAD - CONV_HIST
    cs = []
    for s in range(n_slabs):
        ls = _lane_block(s)
        c = buf_ref[s, off:off + ROW_TILE, :] * wdw_ref[0:1, ls]
        for k in range(1, CONV_WIDTH):
            c = c + buf_ref[s, off + k:off + k + ROW_TILE, :] * wdw_ref[k:k + 1, ls]
        cs.append(c)
    c = jnp.concatenate(cs, axis=1)
    o_ref[...] = _conv_tail(c, x, bdw_ref, lng_ref, lnb_ref, w2_ref, b2_ref)
    for s in range(n_slabs):
        last = buf_ref[s, ROW_TILE:ROW_TILE + HIST_PAD, :]
        sto_ref[:, _lane_block(s)] = last
        buf_ref[s, 0:HIST_PAD, :] = last


def _conv_prompt(x, st, layer, weights, job=None):
    n, t, _ = x.shape
    tiles = t // ROW_TILE
    row = pl.BlockSpec((None, ROW_TILE, D_MODEL), lambda b, i: (b, i, 0))
    st_spec = pl.BlockSpec((None, HIST_PAD, CONV_CH), lambda b, i: (b, 0, 0))
    return _call_with_compaction(
        _conv_prompt_kernel, job, n * tiles, lambda b, i: b * tiles + i,
        out_shape=[jax.ShapeDtypeStruct((n, t, D_MODEL), F32),
                   jax.ShapeDtypeStruct((n, HIST_PAD, CONV_CH), F32)],
        grid=(n, tiles),
        in_specs=[row, st_spec] + [_layer_spec(a, layer) for a in weights],
        out_specs=[row, st_spec],
        scratch_shapes=[pltpu.VMEM((CONV_CH // LANES, HIST_PAD + ROW_TILE, LANES), F32)],
        args=(x, st, *weights),
        semantics=("arbitrary", "arbitrary"), vmem_mib=32, name="conv_prompt",
    )


def _conv_sample_kernel(x_ref, st_ref, g_ref, w1_ref, b1_ref, wdw_ref, bdw_ref, lng_ref, lnb_ref,
                        w2_ref, b2_ref, *rest):
    *prev_refs, o_ref, sto_ref = rest
    t_new, nb, _ = x_ref.shape
    x = x_ref[...].reshape(t_new * nb, D_MODEL)
    h = _rms(x, g_ref[...]).astype(BF16)
    glu = _glu(h, w1_ref, b1_ref)
    new = [glu[t * nb:(t + 1) * nb, :] for t in range(t_new)]

    def full(j):
        return st_ref[j] if j < CONV_HIST else new[j - CONV_HIST]

    outs = []
    for t in range(t_new):
        c = full(t) * wdw_ref[0:1, :]
        for k in range(1, CONV_WIDTH):
            c = c + full(t + k) * wdw_ref[k:k + 1, :]
        outs.append(c)
    c = jnp.concatenate(outs, axis=0)
    o_ref[...] = _conv_tail(c, x, bdw_ref, lng_ref, lnb_ref, w2_ref, b2_ref).reshape(t_new, nb, D_MODEL)
    if prev_refs:
        for l, prev_ref in enumerate(prev_refs):
            sto_ref[l] = prev_ref[...]
        for j in range(CONV_HIST):
            sto_ref[len(prev_refs), j] = full(j + t_new)
    else:
        for j in range(CONV_HIST):
            sto_ref[j] = full(j + t_new)


def _conv_sample(x, st_all, layer, weights, prev_states=()):
    t_new, n, _ = x.shape
    nb = SAMPLE_CONV_NB
    x_spec = pl.BlockSpec((t_new, nb, D_MODEL), lambda i: (0, i, 0))
    st_spec = pl.BlockSpec((None, CONV_HIST, nb, CONV_CH), lambda i: (layer, 0, i, 0))
    one_state = pl.BlockSpec((CONV_HIST, nb, CONV_CH), lambda i: (0, i, 0))
    if prev_states:
        n_stack = len(prev_states) + 1
        st_sds = jax.ShapeDtypeStruct((n_stack, CONV_HIST, n, CONV_CH), F32)
        sto_spec = pl.BlockSpec((n_stack, CONV_HIST, nb, CONV_CH), lambda i: (0, 0, i, 0))
    else:
        st_sds = jax.ShapeDtypeStruct((CONV_HIST, n, CONV_CH), F32)
        sto_spec = one_state
    return pl.pallas_call(
        _conv_sample_kernel,
        out_shape=(jax.ShapeDtypeStruct((t_new, n, D_MODEL), F32), st_sds),
        grid=(n // nb,),
        in_specs=[x_spec, st_spec] + [_layer_spec(a, layer) for a in weights] + [one_state] * len(prev_states),
        out_specs=(x_spec, sto_spec),
        compiler_params=_params(("arbitrary",), 48),
        name="conv_sample",
    )(x, st_all, *weights, *prev_states)


def _kv_rows(x_ref, g_ref, wkv_ref, e_ref, gk_ref):
    h = _rms(x_ref[...], g_ref[...]).astype(BF16)
    k = jnp.dot(h, wkv_ref[:, :HEADS_W], preferred_element_type=F32)
    v = jnp.dot(h, wkv_ref[:, HEADS_W:], preferred_element_type=F32)
    k = k * lax.rsqrt(_head_mean_sq(k, e_ref) + NORM_EPS) * gk_ref[...]
    return jnp.concatenate([k, v], axis=1)


def _kv_sample_kernel(x_ref, g_ref, wkv_ref, e_ref, gk_ref, kv_ref):
    kv_ref[...] = _kv_rows(x_ref, g_ref, wkv_ref, e_ref, gk_ref)


def _kvq_prompt_kernel(x_ref, g_ref, wkv_ref, e_ref, gk_ref, gm_ref, wq_ref, gq_ref,
                       kv_ref, kb0_ref, kb1_ref, kb2_ref, q0_ref, q1_ref, q2_ref, slab_ref):
    kv = _kv_rows(x_ref, g_ref, wkv_ref, e_ref, gk_ref)
    kv_ref[...] = kv
    kb0_ref[0] = kv.astype(BF16)
    _deinterleave_store(kv, slab_ref, kb1_ref, WINDOWS[1][1])
    _deinterleave_store(kv, slab_ref, kb2_ref, WINDOWS[2][1])
    h = _rms(x_ref[...], gm_ref[...]).astype(BF16)
    _q_prompt_store(h, wq_ref, e_ref, gq_ref, q0_ref, q1_ref, q2_ref, slab_ref)


def _proj_weight_specs(layer, g, w, e, gh):
    return [_layer_spec(g, layer), _layer_spec(w, layer), _const_spec(e.shape), _layer_spec(gh, layer)]


def _kv_proj_sample(x, g, wkv, e, gk):
    m = x.shape[0]
    return pl.pallas_call(
        _kv_sample_kernel,
        out_shape=jax.ShapeDtypeStruct((m, KV_W), F32),
        grid=(m // ROW_TILE,),
        in_specs=[pl.BlockSpec((ROW_TILE, D_MODEL), lambda i: (i, 0))] + _proj_weight_specs(0, g, wkv, e, gk),
        out_specs=pl.BlockSpec((ROW_TILE, KV_W), lambda i: (i, 0)),
        compiler_params=_params(("arbitrary",), 32),
        name="kv_proj_sample",
    )(x, g, wkv, e, gk)


def _dilated_shapes(n, t, width, dtype):
    sds, specs = [], []
    for _, d in WINDOWS:
        sds.append(jax.ShapeDtypeStruct((n, d, t // d, width), dtype))
        specs.append(pl.BlockSpec((None, d, ROW_TILE // d, width), lambda b, i: (b, 0, i, 0)))
    return sds, specs


def _kvq_proj_prompt(x, g, wkv, e, gk, win, layer, g_mix, wq, gq):
    n, t, _ = x.shape
    kv_sds, kv_specs = _dilated_shapes(n, t, KV_W, BF16)
    q_sds, q_specs = _dilated_shapes(n, t, HEADS_W, BF16)
    skip = (t - win) // ROW_TILE
    assert win % ROW_TILE == 0 and t % ROW_TILE == 0
    f32_spec = pl.BlockSpec((None, ROW_TILE, KV_W), lambda b, i: (b, jnp.maximum(i - skip, 0), 0))
    res = pl.pallas_call(
        _kvq_prompt_kernel,
        out_shape=[jax.ShapeDtypeStruct((n, win, KV_W), F32)] + kv_sds + q_sds,
        grid=(n, t // ROW_TILE),
        in_specs=[pl.BlockSpec((None, ROW_TILE, D_MODEL), lambda b, i: (b, i, 0))]
        + _proj_weight_specs(0, g, wkv, e, gk)
        + [_layer_spec(g_mix, layer), _layer_spec(wq, 0), _layer_spec(gq, 0)],
        out_specs=[f32_spec] + kv_specs + q_specs,
        scratch_shapes=[pltpu.VMEM((KV_W // LANES, ROW_TILE, LANES), F32)],
        compiler_params=_params(("arbitrary", "arbitrary"), 48),
        name="kvq_proj_prompt",
    )(x, g, wkv, e, gk, g_mix, wq, gq)
    return res[0], res[1:4], res[4:7]


def _q_group(h, wq_ref, e_ref, gq_ref, grp):
    q = jnp.dot(h, wq_ref[:, _lane_block(grp, HEADS_W)], preferred_element_type=F32)
    return q * lax.rsqrt(_head_mean_sq(q, e_ref) + NORM_EPS) * gq_ref[...]


def _q_sample_kernel(x_ref, g_ref, wq_ref, e_ref, gq_ref, q_ref):
    h = _rms(x_ref[...], g_ref[...]).astype(BF16)
    for grp in range(N_GROUPS):
        q = _q_group(h, wq_ref, e_ref, gq_ref, grp)
        q_ref[:, _lane_block(grp, HEADS_W)] = q.astype(BF16).astype(F32)


def _q_prompt_store(h, wq_ref, e_ref, gq_ref, q0_ref, q1_ref, q2_ref, slab_ref):
    q0_ref[0] = _q_group(h, wq_ref, e_ref, gq_ref, 0).astype(BF16)
    _deinterleave_store(_q_group(h, wq_ref, e_ref, gq_ref, 1), slab_ref, q1_ref, WINDOWS[1][1])
    _deinterleave_store(_q_group(h, wq_ref, e_ref, gq_ref, 2), slab_ref, q2_ref, WINDOWS[2][1])


def _q_proj_sample(x, layer, j, g, wq, e, gq):
    m = x.shape[0]
    qw = N_GROUPS * HEADS_W
    specs = [_layer_spec(g, layer), _layer_spec(wq, j), _const_spec(e.shape), _layer_spec(gq, j)]
    return pl.pallas_call(
        _q_sample_kernel,
        out_shape=jax.ShapeDtypeStruct((m, qw), F32),
        grid=(m // ROW_TILE,),
        in_specs=[pl.BlockSpec((ROW_TILE, D_MODEL), lambda i: (i, 0))] + specs,
        out_specs=pl.BlockSpec((ROW_TILE, qw), lambda i: (i, 0)),
        compiler_params=_params(("arbitrary",), 32),
        name="q_proj_sample",
    )(x, g, wq, e, gq)


def _attn_group_kernel(q_ref, kc_ref, kp_ref, vc_ref, vp_ref, bias_ref, acc_ref, m_ref, l_ref):
    first_block = pl.program_id(2) == 0
    key_col = lax.broadcasted_iota(jnp.int32, (2 * BAND, 2 * BAND), 1)
    lane = lax.broadcasted_iota(jnp.int32, (BAND, LANES), 1)
    left = lane < HEAD_DIM
    lane_row = lax.broadcasted_iota(jnp.int32, (1, LANES), 1)
    head_keep = [jnp.where(lane_row < HEAD_DIM, 1.0, 0.0).astype(BF16),
                 jnp.where(lane_row < HEAD_DIM, 0.0, 1.0).astype(BF16)]
    no_history = jnp.logical_and(first_block, key_col < BAND)

    for sb in range(q_ref.shape[0] // BAND):
        r0 = sb * BAND
        m_tile = jnp.zeros((BAND, LANES), F32)
        l_tile = jnp.ones((BAND, LANES), F32)
        for pair in range(N_HEADS // 2):
            ls = _lane_block(pair)
            q2 = q_ref[r0:r0 + BAND, ls]
            if sb == 0:
                k2 = jnp.concatenate([kp_ref[:, ls], kc_ref[0:BAND, ls]], axis=0)
                v2 = jnp.concatenate([vp_ref[:, ls], vc_ref[0:BAND, ls]], axis=0)
            else:
                k2 = kc_ref[r0 - BAND:r0 + BAND, ls]
                v2 = vc_ref[r0 - BAND:r0 + BAND, ls]
            qq = jnp.concatenate([q2 * head_keep[0], q2 * head_keep[1]], axis=0)
            s = _dot_t(qq, k2) + bias_ref[pair]
            if sb == 0:
                s = jnp.where(no_history, NEG_INF, s)
            m = jnp.max(s, axis=-1, keepdims=True)
            p = jnp.exp2(s - m)
            l = jnp.sum(p, axis=-1, keepdims=True)
            a = jnp.dot(p.astype(BF16), v2, preferred_element_type=F32)
            acc_ref[r0:r0 + BAND, ls] = jnp.where(left, a[:BAND], a[BAND:])
            for hh in range(2):
                rows = slice(hh * BAND, (hh + 1) * BAND)
                m_tile = jnp.where(lane == 2 * pair + hh, m[rows], m_tile)
                l_tile = jnp.where(lane == 2 * pair + hh, l[rows], l_tile)
        m_ref[r0:r0 + BAND, :] = m_tile
        l_ref[r0:r0 + BAND, :] = l_tile


def _attn_group(q, kvb, bias, grp):
    n, d, rows, _ = q.shape
    qb = min(ATT_QB, rows)
    per_blk = qb // BAND
    stat_sds = jax.ShapeDtypeStruct((n, d, rows, LANES), F32)
    stat_spec = pl.BlockSpec((None, None, qb, LANES), lambda b, r, i: (b, r, i, 0))

    def prev_idx(i):
        return jnp.maximum(i * per_blk - 1, 0)

    def cur(col):
        return pl.BlockSpec((None, None, qb, HEADS_W), lambda b, r, i: (b, r, i, col))

    def prev(col):
        return pl.BlockSpec((None, None, BAND, HEADS_W), lambda b, r, i: (b, r, prev_idx(i), col))

    return pl.pallas_call(
        _attn_group_kernel,
        out_shape=(jax.ShapeDtypeStruct((n, d, rows, HEADS_W), F32), stat_sds, stat_sds),
        grid=(n, d, rows // qb),
        in_specs=[cur(0), cur(0), prev(0), cur(1), prev(1),
                  _const_spec((N_HEADS // 2, 2 * BAND, 2 * BAND))],
        out_specs=(cur(0), stat_spec, stat_spec),
        compiler_params=_params(("arbitrary", "arbitrary", "arbitrary"), 32),
        name=f"attn_group{grp}",
    )(q, kvb, kvb, kvb, kvb, bias)


def _compact_cache_kernel(c_ref, eye_ref, sel_ref, recent_ref, far_ref, *, t_new):
    p = c_ref.shape[-1]
    d1, d2 = WINDOWS[1][1], WINDOWS[2][1]
    x = c_ref[...].reshape(KV_W, p).astype(BF16)
    lo = p - d1 * BAND
    for j in range(d1 * BAND // MXU_DIM):
        cols = x[:, lo + j * MXU_DIM:lo + (j + 1) * MXU_DIM]
        recent_ref[j * MXU_DIM:(j + 1) * MXU_DIM, :] = _dot_t(eye_ref[...], cols).astype(BF16)
    per = MXU_DIM // d2
    for c in range(p // MXU_DIM):
        picked = _dot_t(sel_ref[...], x[:, c * MXU_DIM:(c + 1) * MXU_DIM]).astype(BF16)
        for t in range(t_new):
            far_ref[t * BAND + c * per:t * BAND + (c + 1) * per, :] = picked[t * per:(t + 1) * per, :]


def _compaction_job(cache_kv, t_new):
    n, p = cache_kv.shape[:2]
    d1, d2 = WINDOWS[1][1], WINDOWS[2][1]
    per = MXU_DIM // d2
    assert p == WINDOWS[2][0] and p // d2 == BAND and t_new <= d1 and p % MXU_DIM == 0
    feat_major = jnp.transpose(cache_kv, (0, 2, 3, 4, 1))
    eye = jnp.eye(MXU_DIM, dtype=BF16)
    row = jnp.arange(t_new * per)[:, None]
    col = jnp.arange(MXU_DIM)[None, :]
    sel = jnp.where(col == (row % per) * d2 + row // per, 1.0, 0.0).astype(BF16)
    return dict(inputs=(feat_major, eye, sel), t_new=t_new, n=n, next=0, chunks=[])


def _fused_body(*refs, body, n_in, n_out, t_new):
    c_ref, eye_ref, sel_ref = refs[n_in:n_in + 3]
    rest = refs[n_in + 3:]
    recent_ref, far_ref = rest[n_out:n_out + 2]
    body(*refs[:n_in], *rest[:n_out], *rest[n_out + 2:])
    _compact_cache_kernel(c_ref, eye_ref, sel_ref, recent_ref, far_ref, t_new=t_new)


def _call_with_compaction(body, job, steps, step_of, *, grid, in_specs, out_specs, out_shape, args,
                          semantics, vmem_mib, name, scratch_shapes=()):
    if job is None:
        return pl.pallas_call(body, out_shape=out_shape, grid=grid, in_specs=in_specs, out_specs=out_specs,
                              scratch_shapes=scratch_shapes, compiler_params=_params(semantics, vmem_mib),
                              name=name)(*args)
    base, t_new = job["next"], job["t_new"]
    assert base + steps <= job["n"]
    job["next"] = base + steps
    feat, eye, sel = job["inputs"]
    n_out = len(out_shape)
    rows = (WINDOWS[1][1] * BAND, t_new * BAND)
    res = pl.pallas_call(
        functools.partial(_fused_body, body=body, n_in=len(args), n_out=n_out, t_new=t_new),
        out_shape=list(out_shape) + [jax.ShapeDtypeStruct((steps, r, KV_W), BF16) for r in rows],
        grid=grid,
        in_specs=list(in_specs)
        + [pl.BlockSpec((None,) + feat.shape[1:], lambda *idx: (base + step_of(*idx), 0, 0, 0, 0)),
           _const_spec(eye.shape), _const_spec(sel.shape)],
        out_specs=list(out_specs)
        + [pl.BlockSpec((None, r, KV_W), lambda *idx: (step_of(*idx), 0, 0)) for r in rows],
        scratch_shapes=scratch_shapes,
        compiler_params=_params(semantics, vmem_mib + COMPACT_VMEM_MIB),
        name=name,
    )(*args, feat, eye, sel)
    job["chunks"].append((base, *res[n_out:]))
    return res[:n_out]


def _attn_sample_kernel(q_ref, kvn_ref, recent_ref, far_ref, b0_ref, b1_ref, b2_ref, b2n_ref, o_ref):
    t_new = q_ref.shape[0]
    n_seq = SAMPLE_SEQ_STEP
    base = (pl.program_id(0) % (SAMPLE_ROWS // n_seq)) * n_seq
    n_rows = t_new * N_HEADS
    hsel = (lax.broadcasted_iota(jnp.int32, (N_HEADS, HEADS_W), 1) // HEAD_DIM
            == lax.broadcasted_iota(jnp.int32, (N_HEADS, HEADS_W), 0))
    row8 = lax.broadcasted_iota(jnp.int32, (SUBLANES, KV_W), 0)

    qs, kas, vas, kbs, vbs = [], [], [], [], []
    for j in range(n_seq):
        b = base + j
        qs.append([jnp.concatenate(
            [jnp.where(hsel, q_ref[t, pl.ds(b, 1), _lane_block(grp, HEADS_W)], 0.0) for t in range(t_new)],
            axis=0).astype(BF16) for grp in range(N_GROUPS)])
        new8 = jnp.zeros((SUBLANES, KV_W), F32)
        for t in range(t_new):
            new8 = jnp.where(row8 == t, kvn_ref[t, pl.ds(b, 1), :], new8)
        new = jnp.concatenate([new8, jnp.zeros((BAND - SUBLANES, KV_W), F32)], axis=0).astype(BF16)
        recent = recent_ref[j]
        kas.append(jnp.concatenate([recent[:, :HEADS_W], new[:, :HEADS_W]], axis=0))
        vas.append(jnp.concatenate([recent[:, HEADS_W:], new[:, HEADS_W:]], axis=0))
        kbs.append(far_ref[j, :, :HEADS_W])
        vbs.append(far_ref[j, :, HEADS_W:])
    n_a = kas[0].shape[0]

    def scores(grp, keys, lo, bias_ref):
        return jnp.concatenate([_dot_t(qs[j][grp], keys[j][lo:]) for j in range(n_seq)],
                               axis=0) + bias_ref[...]

    def softmax_pv(pieces):
        m = functools.reduce(jnp.maximum, [jnp.max(s, axis=-1, keepdims=True) for s, _, _ in pieces])
        l = jnp.zeros((n_seq * n_rows, 1), F32)
        acc = jnp.zeros((n_seq * n_rows, HEADS_W), F32)
        for s, vals, lo in pieces:
            p = jnp.exp2(s - m)
            l = l + jnp.sum(p, axis=-1, keepdims=True)
            pb = p.astype(BF16)
            acc = acc + jnp.concatenate(
                [jnp.dot(pb[j * n_rows:(j + 1) * n_rows], vals[j][lo:], preferred_element_type=F32)
                 for j in range(n_seq)], axis=0)
        return acc, m, l

    lo0, lo_new = n_a - 2 * BAND, n_a - BAND
    parts = [
        softmax_pv([(scores(0, kas, lo0, b0_ref), vas, lo0)]),
        softmax_pv([(scores(1, kas, 0, b1_ref), vas, 0)]),
        softmax_pv([(scores(2, kbs, 0, b2_ref), vbs, 0), (scores(2, kas, lo_new, b2n_ref), vas, lo_new)]),
    ]
    m_all = functools.reduce(jnp.maximum, [m for _, m, _ in parts])
    num = jnp.zeros((n_seq * n_rows, HEADS_W), F32)
    den = jnp.zeros((n_seq * n_rows, 1), F32)
    for acc, m, l in parts:
        e = jnp.exp2(m - m_all)
        num = num + e * acc
        den = den + e * l
    out = num * (1.0 / den)
    for j in range(n_seq):
        for t in range(t_new):
            r0 = j * n_rows + t * N_HEADS
            own = jnp.where(hsel, out[r0:r0 + N_HEADS, :], 0.0)
            o_ref[t, pl.ds(base + j, 1), :] = jnp.sum(own, axis=0, keepdims=True)


def _attn_sample(q, kvn, chunks, biases):
    t_new, n, qw = q.shape
    step = SAMPLE_SEQ_STEP
    per_blk = SAMPLE_ROWS // step
    biases = [jnp.tile(b, (step, 1)) for b in biases]
    outs = []
    for first, recent, far in chunks:
        count = recent.shape[0]
        assert first % SAMPLE_ROWS == 0 and count % SAMPLE_ROWS == 0
        blk0 = first // SAMPLE_ROWS
        blk = lambda width: pl.BlockSpec((t_new, SAMPLE_ROWS, width), lambda i: (0, blk0 + i // per_blk, 0))
        rows = lambda a: pl.BlockSpec((step,) + a.shape[1:], lambda i: (i, 0, 0))
        outs.append(pl.pallas_call(
            _attn_sample_kernel,
            out_shape=jax.ShapeDtypeStruct((t_new, count, HEADS_W), F32),
            grid=(count // step,),
            in_specs=[blk(qw), blk(KV_W), rows(recent), rows(far)] + [_const_spec(b.shape) for b in biases],
            out_specs=pl.BlockSpec((t_new, SAMPLE_ROWS, HEADS_W), lambda i: (0, i // per_blk, 0)),
            compiler_params=_params(("arbitrary",), 40),
            name="attn_sample",
        )(q, kvn, recent, far, *biases))
    return jnp.concatenate(outs, axis=1)


def _attn_out_sample_kernel(x_ref, o_ref, wo_ref, g_ref, wup_ref, wdn_ref, out_ref):
    x = x_ref[...] + jnp.dot(o_ref[...].astype(BF16), wo_ref[...], preferred_element_type=F32)
    out_ref[...] = _mlp_residual(x, g_ref, wup_ref, wdn_ref)


def _attn_out_prompt_kernel(x_ref, a0_ref, m0_ref, l0_ref, a1_ref, m1_ref, l1_ref, a2_ref, m2_ref, l2_ref,
                            ex_ref, wo_ref, g_ref, wup_ref, wdn_ref, out_ref, slab_ref):
    rows = x_ref.shape[0]
    n_o = HEADS_W // LANES
    for gi, (a_ref, m_ref, l_ref) in enumerate(((a1_ref, m1_ref, l1_ref), (a2_ref, m2_ref, l2_ref))):
        d = WINDOWS[gi + 1][1]
        for r in range(d):
            dst = pl.ds(r, rows // d, stride=d)
            for s in range(n_o):
                slab_ref[gi, s, dst, :] = a_ref[r, :, _lane_block(s)]
            slab_ref[gi, n_o, dst, :] = m_ref[r]
            slab_ref[gi, n_o + 1, dst, :] = l_ref[r]
    accs = [a0_ref[0]] + [jnp.concatenate([slab_ref[gi, s] for s in range(n_o)], axis=1) for gi in range(2)]
    ms = [m0_ref[0], slab_ref[0, n_o], slab_ref[1, n_o]]
    ls = [l0_ref[0], slab_ref[0, n_o + 1], slab_ref[1, n_o + 1]]
    mx = functools.reduce(jnp.maximum, ms)
    es = [jnp.exp2(m - mx) for m in ms]
    inv = 1.0 / functools.reduce(jnp.add, [e * l for e, l in zip(es, ls)])
    o = jnp.zeros((rows, HEADS_W), F32)
    for e, og in zip(es, accs):
        w = e * inv
        hi = w.astype(BF16)
        lo = (w - hi.astype(F32)).astype(BF16)
        wide = (jnp.dot(hi, ex_ref[...], preferred_element_type=F32)
                + jnp.dot(lo, ex_ref[...], preferred_element_type=F32))
        o = o + wide * og
    x = x_ref[...] + jnp.dot(o.astype(BF16), wo_ref[...], preferred_element_type=F32)
    out_ref[...] = _mlp_residual(x, g_ref, wup_ref, wdn_ref)


def _attn_out_q_prompt_kernel(*refs):
    n_in = 15
    gm_ref, wq_ref, e_ref, gq_ref = refs[n_in:n_in + 4]
    out_ref, q0_ref, q1_ref, q2_ref, slab_ref, qslab_ref = refs[n_in + 4:]
    _attn_out_prompt_kernel(*refs[:n_in], out_ref, slab_ref)
    h = _rms(out_ref[...], gm_ref[...]).astype(BF16)
    _q_prompt_store(h, wq_ref, e_ref, gq_ref, q0_ref, q1_ref, q2_ref, qslab_ref)


def _mlp_weight_specs(layer, j, wo, g, wup, wdn):
    return [_layer_spec(wo, j)] + [_layer_spec(a, layer) for a in (g, wup, wdn)]


def _attn_out_sample(x, o, layer, j, wo, g, wup, wdn):
    m = x.shape[0]
    row = pl.BlockSpec((ROW_TILE, D_MODEL), lambda i: (i, 0))
    return pl.pallas_call(
        _attn_out_sample_kernel,
        out_shape=jax.ShapeDtypeStruct((m, D_MODEL), F32),
        grid=(m // ROW_TILE,),
        in_specs=[row, pl.BlockSpec((ROW_TILE, HEADS_W), lambda i: (i, 0))]
        + _mlp_weight_specs(layer, j, wo, g, wup, wdn),
        out_specs=row,
        compiler_params=_params(("arbitrary",), 48),
        name="attn_out_sample",
    )(x, o, wo, g, wup, wdn)


def _attn_out_prompt(x, parts, ex, layer, j, wo, g, wup, wdn, next_q=None):
    n, t, _ = x.shape
    tile = ATT_OUT_TILE
    assert tile == ROW_TILE
    row = pl.BlockSpec((None, tile, D_MODEL), lambda b, i: (b, i, 0))
    part_specs = []
    for _, d in WINDOWS:
        for width in (HEADS_W, LANES, LANES):
            part_specs.append(pl.BlockSpec((None, d, tile // d, width), lambda b, i: (b, 0, i, 0)))
    flat = [a for part in parts for a in part]
    in_specs = ([row] + part_specs + [_const_spec((LANES, HEADS_W))]
                + _mlp_weight_specs(layer, j, wo, g, wup, wdn))
    args = [x, *flat, ex, wo, g, wup, wdn]
    out_shape, out_specs = [jax.ShapeDtypeStruct((n, t, D_MODEL), F32)], [row]
    scratch = [pltpu.VMEM((2, HEADS_W // LANES + 2, tile, LANES), F32)]
    body = _attn_out_prompt_kernel
    if next_q is not None:
        g_mix, wq, e, gq = next_q
        in_specs += [_layer_spec(g_mix, layer + 1), _layer_spec(wq, j + 1), _const_spec(e.shape),
                     _layer_spec(gq, j + 1)]
        args += [g_mix, wq, e, gq]
        q_sds, q_specs = _dilated_shapes(n, t, HEADS_W, BF16)
        out_shape, out_specs = out_shape + q_sds, out_specs + q_specs
        scratch.append(pltpu.VMEM((HEADS_W // LANES, tile, LANES), F32))
        body = _attn_out_q_prompt_kernel
    res = pl.pallas_call(
        body,
        out_shape=out_shape,
        grid=(n, t // tile),
        in_specs=in_specs,
        out_specs=out_specs,
        scratch_shapes=scratch,
        compiler_params=_params(("arbitrary", "arbitrary"), 56),
        name="attn_out_prompt",
    )(*args)
    return res[0], res[1:]


def _alibi_slopes():
    n = N_GROUPS * N_HEADS
    s = 2.0 ** (-8.0 * jnp.arange(1, n + 1, dtype=F32) / n)
    return s.reshape(N_GROUPS, N_HEADS)


def _prompt_bias(slopes, grp):
    d = WINDOWS[grp][1]
    iq = jnp.arange(BAND, dtype=jnp.int32)[:, None]
    ik = jnp.arange(2 * BAND, dtype=jnp.int32)[None, :]
    steps = iq + BAND - ik
    valid = (steps >= 0) & (steps <= BAND)
    offs = (steps * d).astype(F32)
    per_head = jnp.where(valid[None], -slopes[grp][:, None, None] * offs[None] * LOG2_E, NEG_INF)
    return per_head.reshape(N_HEADS // 2, 2 * BAND, 2 * BAND)


def _sample_bias(slopes, t_new, past):
    d1, d2 = WINDOWS[1][1], WINDOWS[2][1]
    t = jnp.arange(t_new, dtype=jnp.int32)[:, None]

    def table(slope, dist, valid):
        b = jnp.where(valid[:, None, :], -slope[None, :, None] * dist.astype(F32)[:, None, :] * LOG2_E,
                      NEG_INF)
        return b.reshape(t_new * N_HEADS, -1)

    u = jnp.arange(BAND, dtype=jnp.int32)[None, :]
    dist_n = t - u
    ok_n = (u < t_new) & (dist_n >= 0)
    c = jnp.arange(BAND, dtype=jnp.int32)[None, :]
    dist_c = BAND + t - c
    b0 = table(slopes[0], jnp.concatenate([dist_c, dist_n], axis=1),
               jnp.concatenate([dist_c <= WINDOWS[0][0], ok_n], axis=1))
    c = jnp.arange(d1 * BAND, dtype=jnp.int32)[None, :]
    dist_c = d1 * BAND + t - c
    ok_c = (dist_c % d1 == 0) & (dist_c <= WINDOWS[1][0])
    b1 = table(slopes[1], jnp.concatenate([dist_c, dist_n], axis=1),
               jnp.concatenate([ok_c, ok_n & (dist_n % d1 == 0)], axis=1))
    per = past // d2
    col = jnp.arange(t_new * per, dtype=jnp.int32)[None, :]
    pos = (col % per) * d2 + col // per
    dist_c = past + t - pos
    ok_c = (dist_c % d2 == 0) & (dist_c <= WINDOWS[2][0])
    b2 = table(slopes[2], dist_c, ok_c)
    b2n = table(slopes[2], dist_n, ok_n & (dist_n % d2 == 0))
    return b0, b1, b2, b2n


def _head_mean_matrix():
    idx = jnp.arange(HEADS_W) // HEAD_DIM
    return jnp.where(idx[:, None] == idx[None, :], 1.0 / HEAD_DIM, 0.0).astype(BF16)


def _head_expand_matrix():
    lane = jnp.arange(LANES)[:, None]
    col = jnp.arange(HEADS_W)[None, :] // HEAD_DIM
    return jnp.where(lane == col, 1.0, 0.0).astype(BF16)


def kernel(x_prompt, x_sample, state_conv, cache_kv, norm_mix_g, norm_mlp_g, conv_w_pw1, conv_b_pw1,
           conv_w_dw, conv_b_dw, conv_ln_g, conv_ln_b, conv_w_pw2, conv_b_pw2, kv_norm_g, w_kv,
           k_norm_g, attn_w_q, q_norm_g, attn_w_o, mlp_w_up, mlp_w_down):
    n_p, t_p, _ = x_prompt.shape
    n_s, t_s, _ = x_sample.shape
    past = cache_kv.shape[1]
    scale = HEAD_DIM ** -0.5 * LOG2_E

    w1 = conv_w_pw1.astype(BF16)
    w2 = conv_w_pw2.astype(BF16)
    wkv = w_kv.astype(BF16)
    wq = attn_w_q.astype(BF16)
    wo = attn_w_o.astype(BF16)
    wup = mlp_w_up.astype(BF16)
    wdn = mlp_w_down.astype(BF16)
    vec = lambda a: a.reshape(-1, 1, a.shape[-1])
    g_mix, g_mlp, g_kv = vec(norm_mix_g), vec(norm_mlp_g), vec(kv_norm_g)
    e_mat = _head_mean_matrix()
    ex_mat = _head_expand_matrix()
    gk = vec(jnp.tile(k_norm_g, N_HEADS))
    gq = vec(jnp.tile(q_norm_g, (1, N_HEADS)) * scale)
    wkv = wkv[None]
    slopes = _alibi_slopes()
    bias_p = [_prompt_bias(slopes, g) for g in range(N_GROUPS)]
    bias_s = _sample_bias(slopes, t_s, past)
    conv_weights = (g_mix, w1, vec(conv_b_pw1), conv_w_dw, vec(conv_b_dw), vec(conv_ln_g),
                    vec(conv_ln_b), w2, vec(conv_b_pw2))

    job = _compaction_job(cache_kv, t_s)
    xp = x_prompt
    conv_p = []
    zero_hist = jnp.zeros((n_p, HIST_PAD, CONV_CH), F32)
    for layer in range(N_A_LAYERS):
        xp, st = _conv_prompt(xp, zero_hist, layer, conv_weights, job)
        conv_p.append(st[:, HIST_PAD - CONV_HIST:])
        xp = _mlp(xp.reshape(n_p * t_p, D_MODEL), layer, g_mlp, wup, wdn, job).reshape(n_p, t_p, D_MODEL)
    assert job["next"] == job["n"]
    win = min(WINDOWS[-1][0], t_p)
    kv_p, kvb_p, qs = _kvq_proj_prompt(xp, g_kv, wkv, e_mat, gk, win, N_A_LAYERS, g_mix, wq, gq)
    for j in range(N_B_LAYERS):
        layer = N_A_LAYERS + j
        parts = [_attn_group(qs[g], kvb_p[g], bias_p[g], g) for g in range(N_GROUPS)]
        next_q = (g_mix, wq, e_mat, gq) if j + 1 < N_B_LAYERS else None
        xp, qs = _attn_out_prompt(xp, parts, ex_mat, layer, j, wo, g_mlp, wup, wdn, next_q)
    y_prompt = xp
    conv_prompt = jnp.stack(conv_p, axis=0)
    kv_prompt = kv_p.reshape(n_p, win, 2, N_HEADS, HEAD_DIM)

    xs = jnp.transpose(x_sample, (1, 0, 2))
    st_in = jnp.transpose(state_conv, (0, 2, 1, 3))
    new_states = []
    for layer in range(N_A_LAYERS):
        prev = new_states if layer == N_A_LAYERS - 1 else ()
        xs, st = _conv_sample(xs, st_in, layer, conv_weights, prev)
        new_states.append(st)
        xs = _mlp(xs.reshape(t_s * n_s, D_MODEL), layer, g_mlp, wup, wdn).reshape(t_s, n_s, D_MODEL)
    xs = xs.reshape(t_s * n_s, D_MODEL)
    kvn = _kv_proj_sample(xs, g_kv, wkv, e_mat, gk).reshape(t_s, n_s, KV_W)
    for j in range(N_B_LAYERS):
        layer = N_A_LAYERS + j
        q = _q_proj_sample(xs, layer, j, g_mix, wq, e_mat, gq)
        o = _attn_sample(q.reshape(t_s, n_s, N_GROUPS * HEADS_W), kvn, job["chunks"], bias_s)
        xs = _attn_out_sample(xs, o.reshape(t_s * n_s, HEADS_W), layer, j, wo, g_mlp, wup, wdn)
    y_sample = jnp.transpose(xs.reshape(t_s, n_s, D_MODEL), (1, 0, 2))
    st_stack = new_states[-1] if N_A_LAYERS > 1 else new_states[-1][None]
    conv_sample = jnp.transpose(st_stack, (0, 2, 1, 3))
    kv_sample = jnp.transpose(kvn, (1, 0, 2)).reshape(n_s, t_s, 2, N_HEADS, HEAD_DIM)

    return (y_prompt, y_sample, conv_prompt, conv_sample, kv_prompt, kv_sample)
```

```python
import functools

import jax
import jax.numpy as jnp
from jax import lax
from jax.experimental import pallas as pl
from jax.experimental.pallas import tpu as pltpu

F32 = jnp.float32
BF16 = jnp.bfloat16

D_MODEL = 1024
CONV_CH = 1024
CONV_WIDTH = 31
CONV_HIST = CONV_WIDTH - 1
HIST_PAD = 32
D_FF = 4096
N_HEADS = 8
HEAD_DIM = 64
HEADS_W = N_HEADS * HEAD_DIM
KV_W = 2 * HEADS_W
WINDOWS = ((128, 1), (512, 4), (2048, 16))
N_GROUPS = len(WINDOWS)
BAND = 128
N_A_LAYERS = 2
N_B_LAYERS = 2
NORM_EPS = 1e-6
NEG_INF = float("-inf")
LOG2_E = 1.4426950408889634
LANES = 128
SUBLANES = 8

ROW_TILE = 512
ATT_OUT_TILE = 512
FF_CHUNK = 1024
ATT_QB = 1024
SAMPLE_CONV_NB = 32
SAMPLE_ROWS = 8
SAMPLE_SEQ_STEP = 4
MXU_DIM = 256
COMPACT_VMEM_MIB = 24
MIB = 1024 * 1024

assert all(w // d == BAND for w, d in WINDOWS)


def _const_spec(shape):
    nd = len(shape)
    return pl.BlockSpec(shape, lambda *_: (0,) * nd, pipeline_mode=pl.Buffered(1))


def _layer_spec(stacked, layer):
    return pl.BlockSpec((None,) + stacked.shape[1:], lambda *_: (layer, 0, 0),
                        pipeline_mode=pl.Buffered(1))


def _params(semantics, vmem_mib):
    return pltpu.CompilerParams(dimension_semantics=semantics, vmem_limit_bytes=vmem_mib * MIB)


def _lane_block(s, width=LANES):
    return slice(s * width, (s + 1) * width)


def _rms(x, g):
    return x * lax.rsqrt(jnp.mean(x * x, axis=-1, keepdims=True) + NORM_EPS) * g


def _mlp_residual(x, g_ref, wup_ref, wdn_ref):
    h = _rms(x, g_ref[...]).astype(BF16)
    acc = x
    for c in range(D_FF // FF_CHUNK):
        sl = _lane_block(c, FF_CHUNK)
        z = jnp.maximum(jnp.dot(h, wup_ref[:, sl], preferred_element_type=F32), 0.0)
        acc = acc + jnp.dot((z * z).astype(BF16), wdn_ref[sl, :], preferred_element_type=F32)
    return acc


def _head_mean_sq(v, e_ref):
    return jnp.dot((v * v).astype(BF16), e_ref[...], preferred_element_type=F32)


def _dot_t(a, b):
    return lax.dot_general(a, b, (((1,), (1,)), ((), ())), preferred_element_type=F32)


def _deinterleave_store(val, slab_ref, out_ref, d):
    rows, width = val.shape
    for s in range(width // LANES):
        slab_ref[s] = val[:, _lane_block(s)]
    for s in range(width // LANES):
        for r in range(d):
            out_ref[r, :, _lane_block(s)] = slab_ref[s, pl.ds(r, rows // d, stride=d), :].astype(BF16)


def _mlp_kernel(x_ref, g_ref, wup_ref, wdn_ref, o_ref):
    o_ref[...] = _mlp_residual(x_ref[...], g_ref, wup_ref, wdn_ref)


def _mlp(x, layer, g, wup, wdn, job=None, cast_layer=None):
    m = x.shape[0]
    row = pl.BlockSpec((ROW_TILE, D_MODEL), lambda i: (i, 0))
    return _call_with_compaction(
        _mlp_kernel, job, m // ROW_TILE, lambda i: i,
        out_shape=[jax.ShapeDtypeStruct((m, D_MODEL), F32)],
        grid=(m // ROW_TILE,),
        in_specs=[row, _layer_spec(g, layer), _layer_spec(wup, 0), _layer_spec(wdn, 0)],
        out_specs=[row],
        args=(x, g, wup, wdn),
        semantics=("arbitrary",), vmem_mib=34, name="mlp", cast_layer=cast_layer,
    )[0]


def _glu(h, w1_ref, b1_ref):
    u = jnp.dot(h, w1_ref[...], preferred_element_type=F32) + b1_ref[...]
    return u[:, :CONV_CH] * jax.nn.sigmoid(u[:, CONV_CH:])


def _conv_tail(c, x, bdw_ref, lng_ref, lnb_ref, w2_ref, b2_ref):
    c = c + bdw_ref[...]
    mu = jnp.mean(c, axis=-1, keepdims=True)
    xc = c - mu
    y = xc * lax.rsqrt(jnp.mean(xc * xc, axis=-1, keepdims=True) + NORM_EPS)
    y = y * lng_ref[...] + lnb_ref[...]
    y = y * jax.nn.sigmoid(y)
    out = jnp.dot(y.astype(BF16), w2_ref[...], preferred_element_type=F32) + b2_ref[...]
    return x + out


def _conv_prompt_kernel(x_ref, st_ref, g_ref, w1_ref, b1_ref, wdw_ref, bdw_ref, lng_ref, lnb_ref,
                        w2_ref, b2_ref, o_ref, sto_ref, buf_ref):
    n_slabs = CONV_CH // LANES

    @pl.when(pl.program_id(1) == 0)
    def _():
        for s in range(n_slabs):
            buf_ref[s, 0:HIST_PAD, :] = st_ref[:, _lane_block(s)]

    x = x_ref[...]
    h = _rms(x, g_ref[...]).astype(BF16)
    u = _glu(h, w1_ref, b1_ref)
    for s in range(n_slabs):
        buf_ref[s, HIST_PAD:HIST_PAD + ROW_TILE, :] = u[:, _lane_block(s)]
    off = HIST_PAD - CONV_HIST
    cs = []
    for s in range(n_slabs):
        ls = _lane_block(s)
        c = buf_ref[s, off:off + ROW_TILE, :] * wdw_ref[0:1, ls]
        for k in range(1, CONV_WIDTH):
            c = c + buf_ref[s, off + k:off + k + ROW_TILE, :] * wdw_ref[k:k + 1, ls]
        cs.append(c)
    c = jnp.concatenate(cs, axis=1)
    o_ref[...] = _conv_tail(c, x, bdw_ref, lng_ref, lnb_ref, w2_ref, b2_ref)
    for s in range(n_slabs):
        last = buf_ref[s, ROW_TILE:ROW_TILE + HIST_PAD, :]
        sto_ref[:, _lane_block(s)] = last
        buf_ref[s, 0:HIST_PAD, :] = last


def _conv_prompt(x, st, layer, weights, job=None, cast_layer=None):
    n, t, _ = x.shape
    tiles = t // ROW_TILE
    row = pl.BlockSpec((None, ROW_TILE, D_MODEL), lambda b, i: (b, i, 0))
    st_spec = pl.BlockSpec((None, HIST_PAD, CONV_CH), lambda b, i: (b, 0, 0))
    return _call_with_compaction(
        _conv_prompt_kernel, job, n * tiles, lambda b, i: b * tiles + i,
        out_shape=[jax.ShapeDtypeStruct((n, t, D_MODEL), F32),
                   jax.ShapeDtypeStruct((n, HIST_PAD, CONV_CH), F32)],
        grid=(n, tiles),
        in_specs=[row, st_spec] + [_layer_spec(a, layer) for a in weights],
        out_specs=[row, st_spec],
        scratch_shapes=[pltpu.VMEM((CONV_CH // LANES, HIST_PAD + ROW_TILE, LANES), F32)],
        args=(x, st, *weights),
        semantics=("arbitrary", "arbitrary"), vmem_mib=32, name="conv_prompt", cast_layer=cast_layer,
    )


def _conv_sample_kernel(x_ref, st_ref, g_ref, w1_ref, b1_ref, wdw_ref, bdw_ref, lng_ref, lnb_ref,
                        w2_ref, b2_ref, *rest):
    *prev_refs, o_ref, sto_ref = rest
    t_new, nb, _ = x_ref.shape
    x = x_ref[...].reshape(t_new * nb, D_MODEL)
    h = _rms(x, g_ref[...]).astype(BF16)
    glu = _glu(h, w1_ref, b1_ref)
    new = [glu[t * nb:(t + 1) * nb, :] for t in range(t_new)]

    def full(j):
        return st_ref[j] if j < CONV_HIST else new[j - CONV_HIST]

    outs = []
    for t in range(t_new):
        c = full(t) * wdw_ref[0:1, :]
        for k in range(1, CONV_WIDTH):
            c = c + full(t + k) * wdw_ref[k:k + 1, :]
        outs.append(c)
    c = jnp.concatenate(outs, axis=0)
    o_ref[...] = _conv_tail(c, x, bdw_ref, lng_ref, lnb_ref, w2_ref, b2_ref).reshape(t_new, nb, D_MODEL)
    if prev_refs:
        for l, prev_ref in enumerate(prev_refs):
            sto_ref[l] = prev_ref[...]
        for j in range(CONV_HIST):
            sto_ref[len(prev_refs), j] = full(j + t_new)
    else:
        for j in range(CONV_HIST):
            sto_ref[j] = full(j + t_new)


def _conv_sample(x, st_all, layer, weights, prev_states=()):
    t_new, n, _ = x.shape
    nb = SAMPLE_CONV_NB
    x_spec = pl.BlockSpec((t_new, nb, D_MODEL), lambda i: (0, i, 0))
    st_spec = pl.BlockSpec((None, CONV_HIST, nb, CONV_CH), lambda i: (layer, 0, i, 0))
    one_state = pl.BlockSpec((CONV_HIST, nb, CONV_CH), lambda i: (0, i, 0))
    if prev_states:
        n_stack = len(prev_states) + 1
        st_sds = jax.ShapeDtypeStruct((n_stack, CONV_HIST, n, CONV_CH), F32)
        sto_spec = pl.BlockSpec((n_stack, CONV_HIST, nb, CONV_CH), lambda i: (0, 0, i, 0))
    else:
        st_sds = jax.ShapeDtypeStruct((CONV_HIST, n, CONV_CH), F32)
        sto_spec = one_state
    return pl.pallas_call(
        _conv_sample_kernel,
        out_shape=(jax.ShapeDtypeStruct((t_new, n, D_MODEL), F32), st_sds),
        grid=(n // nb,),
        in_specs=[x_spec, st_spec] + [_layer_spec(a, layer) for a in weights] + [one_state] * len(prev_states),
        out_specs=(x_spec, sto_spec),
        compiler_params=_params(("arbitrary",), 48),
        name="conv_sample",
    )(x, st_all, *weights, *prev_states)


def _kv_rows(x_ref, g_ref, wkv_ref, e_ref, gk_ref):
    h = _rms(x_ref[...], g_ref[...]).astype(BF16)
    k = jnp.dot(h, wkv_ref[:, :HEADS_W], preferred_element_type=F32)
    v = jnp.dot(h, wkv_ref[:, HEADS_W:], preferred_element_type=F32)
    k = k * lax.rsqrt(_head_mean_sq(k, e_ref) + NORM_EPS) * gk_ref[...]
    return jnp.concatenate([k, v], axis=1)


def _kv_sample_kernel(x_ref, g_ref, wkv_ref, e_ref, gk_ref, kv_ref):
    kv_ref[...] = _kv_rows(x_ref, g_ref, wkv_ref, e_ref, gk_ref)


def _kvq_prompt_kernel(x_ref, g_ref, wkv_ref, e_ref, gk_ref, gm_ref, wq_ref, gq_ref,
                       kv_ref, kb0_ref, kb1_ref, kb2_ref, q0_ref, q1_ref, q2_ref, slab_ref):
    kv = _kv_rows(x_ref, g_ref, wkv_ref, e_ref, gk_ref)
    kv_ref[...] = kv
    kb0_ref[0] = kv.astype(BF16)
    _deinterleave_store(kv, slab_ref, kb1_ref, WINDOWS[1][1])
    _deinterleave_store(kv, slab_ref, kb2_ref, WINDOWS[2][1])
    h = _rms(x_ref[...], gm_ref[...]).astype(BF16)
    _q_prompt_store(h, wq_ref, e_ref, gq_ref, q0_ref, q1_ref, q2_ref, slab_ref)


def _proj_weight_specs(layer, g, w, e, gh):
    return [_layer_spec(g, layer), _layer_spec(w, layer), _const_spec(e.shape), _layer_spec(gh, layer)]


def _kv_proj_sample(x, g, wkv, e, gk):
    m = x.shape[0]
    return pl.pallas_call(
        _kv_sample_kernel,
        out_shape=jax.ShapeDtypeStruct((m, KV_W), F32),
        grid=(m // ROW_TILE,),
        in_specs=[pl.BlockSpec((ROW_TILE, D_MODEL), lambda i: (i, 0))] + _proj_weight_specs(0, g, wkv, e, gk),
        out_specs=pl.BlockSpec((ROW_TILE, KV_W), lambda i: (i, 0)),
        compiler_params=_params(("arbitrary",), 32),
        name="kv_proj_sample",
    )(x, g, wkv, e, gk)


def _dilated_shapes(n, t, width, dtype):
    sds, specs = [], []
    for _, d in WINDOWS:
        sds.append(jax.ShapeDtypeStruct((n, d, t // d, width), dtype))
        specs.append(pl.BlockSpec((None, d, ROW_TILE // d, width), lambda b, i: (b, 0, i, 0)))
    return sds, specs


def _kvq_proj_prompt(x, g, wkv, e, gk, win, layer, g_mix, wq, gq):
    n, t, _ = x.shape
    kv_sds, kv_specs = _dilated_shapes(n, t, KV_W, BF16)
    q_sds, q_specs = _dilated_shapes(n, t, HEADS_W, BF16)
    skip = (t - win) // ROW_TILE
    assert win % ROW_TILE == 0 and t % ROW_TILE == 0
    f32_spec = pl.BlockSpec((None, ROW_TILE, KV_W), lambda b, i: (b, jnp.maximum(i - skip, 0), 0))
    res = pl.pallas_call(
        _kvq_prompt_kernel,
        out_shape=[jax.ShapeDtypeStruct((n, win, KV_W), F32)] + kv_sds + q_sds,
        grid=(n, t // ROW_TILE),
        in_specs=[pl.BlockSpec((None, ROW_TILE, D_MODEL), lambda b, i: (b, i, 0))]
        + _proj_weight_specs(0, g, wkv, e, gk)
        + [_layer_spec(g_mix, layer), _layer_spec(wq, 0), _layer_spec(gq, 0)],
        out_specs=[f32_spec] + kv_specs + q_specs,
        scratch_shapes=[pltpu.VMEM((KV_W // LANES, ROW_TILE, LANES), F32)],
        compiler_params=_params(("arbitrary", "arbitrary"), 48),
        name="kvq_proj_prompt",
    )(x, g, wkv, e, gk, g_mix, wq, gq)
    return res[0], res[1:4], res[4:7]


def _q_group(h, wq_ref, e_ref, gq_ref, grp):
    q = jnp.dot(h, wq_ref[:, _lane_block(grp, HEADS_W)], preferred_element_type=F32)
    return q * lax.rsqrt(_head_mean_sq(q, e_ref) + NORM_EPS) * gq_ref[...]


def _q_sample_kernel(x_ref, g_ref, wq_ref, e_ref, gq_ref, q_ref):
    h = _rms(x_ref[...], g_ref[...]).astype(BF16)
    for grp in range(N_GROUPS):
        q = _q_group(h, wq_ref, e_ref, gq_ref, grp)
        q_ref[:, _lane_block(grp, HEADS_W)] = q.astype(BF16).astype(F32)


def _q_prompt_store(h, wq_ref, e_ref, gq_ref, q0_ref, q1_ref, q2_ref, slab_ref):
    q0_ref[0] = _q_group(h, wq_ref, e_ref, gq_ref, 0).astype(BF16)
    _deinterleave_store(_q_group(h, wq_ref, e_ref, gq_ref, 1), slab_ref, q1_ref, WINDOWS[1][1])
    _deinterleave_store(_q_group(h, wq_ref, e_ref, gq_ref, 2), slab_ref, q2_ref, WINDOWS[2][1])


def _q_proj_sample(x, layer, j, g, wq, e, gq):
    m = x.shape[0]
    qw = N_GROUPS * HEADS_W
    specs = [_layer_spec(g, layer), _layer_spec(wq, j), _const_spec(e.shape), _layer_spec(gq, j)]
    return pl.pallas_call(
        _q_sample_kernel,
        out_shape=jax.ShapeDtypeStruct((m, qw), F32),
        grid=(m // ROW_TILE,),
        in_specs=[pl.BlockSpec((ROW_TILE, D_MODEL), lambda i: (i, 0))] + specs,
        out_specs=pl.BlockSpec((ROW_TILE, qw), lambda i: (i, 0)),
        compiler_params=_params(("arbitrary",), 32),
        name="q_proj_sample",
    )(x, g, wq, e, gq)


def _attn_group_kernel(q_ref, kc_ref, kp_ref, vc_ref, vp_ref, bias_ref, acc_ref, m_ref, l_ref):
    first_block = pl.program_id(2) == 0
    key_col = lax.broadcasted_iota(jnp.int32, (2 * BAND, 2 * BAND), 1)
    lane = lax.broadcasted_iota(jnp.int32, (BAND, LANES), 1)
    left = lane < HEAD_DIM
    lane_row = lax.broadcasted_iota(jnp.int32, (1, LANES), 1)
    head_keep = [jnp.where(lane_row < HEAD_DIM, 1.0, 0.0).astype(BF16),
                 jnp.where(lane_row < HEAD_DIM, 0.0, 1.0).astype(BF16)]
    no_history = jnp.logical_and(first_block, key_col < BAND)

    for sb in range(q_ref.shape[0] // BAND):
        r0 = sb * BAND
        m_tile = jnp.zeros((BAND, LANES), F32)
        l_tile = jnp.ones((BAND, LANES), F32)
        for pair in range(N_HEADS // 2):
            ls = _lane_block(pair)
            q2 = q_ref[r0:r0 + BAND, ls]
            if sb == 0:
                k2 = jnp.concatenate([kp_ref[:, ls], kc_ref[0:BAND, ls]], axis=0)
                v2 = jnp.concatenate([vp_ref[:, ls], vc_ref[0:BAND, ls]], axis=0)
            else:
                k2 = kc_ref[r0 - BAND:r0 + BAND, ls]
                v2 = vc_ref[r0 - BAND:r0 + BAND, ls]
            qq = jnp.concatenate([q2 * head_keep[0], q2 * head_keep[1]], axis=0)
            s = _dot_t(qq, k2) + bias_ref[pair]
            if sb == 0:
                s = jnp.where(no_history, NEG_INF, s)
            m = jnp.max(s, axis=-1, keepdims=True)
            p = jnp.exp2(s - m)
            l = jnp.sum(p, axis=-1, keepdims=True)
            a = jnp.dot(p.astype(BF16), v2, preferred_element_type=F32)
            acc_ref[r0:r0 + BAND, ls] = jnp.where(left, a[:BAND], a[BAND:])
            for hh in range(2):
                rows = slice(hh * BAND, (hh + 1) * BAND)
                m_tile = jnp.where(lane == 2 * pair + hh, m[rows], m_tile)
                l_tile = jnp.where(lane == 2 * pair + hh, l[rows], l_tile)
        m_ref[r0:r0 + BAND, :] = m_tile
        l_ref[r0:r0 + BAND, :] = l_tile


def _attn_group(q, kvb, bias, grp):
    n, d, rows, _ = q.shape
    qb = min(ATT_QB, rows)
    per_blk = qb // BAND
    stat_sds = jax.ShapeDtypeStruct((n, d, rows, LANES), F32)
    stat_spec = pl.BlockSpec((None, None, qb, LANES), lambda b, r, i: (b, r, i, 0))

    def prev_idx(i):
        return jnp.maximum(i * per_blk - 1, 0)

    def cur(col):
        return pl.BlockSpec((None, None, qb, HEADS_W), lambda b, r, i: (b, r, i, col))

    def prev(col):
        return pl.BlockSpec((None, None, BAND, HEADS_W), lambda b, r, i: (b, r, prev_idx(i), col))

    return pl.pallas_call(
        _attn_group_kernel,
        out_shape=(jax.ShapeDtypeStruct((n, d, rows, HEADS_W), F32), stat_sds, stat_sds),
        grid=(n, d, rows // qb),
        in_specs=[cur(0), cur(0), prev(0), cur(1), prev(1),
                  _const_spec((N_HEADS // 2, 2 * BAND, 2 * BAND))],
        out_specs=(cur(0), stat_spec, stat_spec),
        compiler_params=_params(("arbitrary", "arbitrary", "arbitrary"), 32),
        name=f"attn_group{grp}",
    )(q, kvb, kvb, kvb, kvb, bias)


def _compact_cache_kernel(c_ref, eye_ref, sel_ref, recent_ref, far_ref, *, t_new):
    p = c_ref.shape[-1]
    d1, d2 = WINDOWS[1][1], WINDOWS[2][1]
    x = c_ref[...].reshape(KV_W, p).astype(BF16)
    lo = p - d1 * BAND
    for j in range(d1 * BAND // MXU_DIM):
        cols = x[:, lo + j * MXU_DIM:lo + (j + 1) * MXU_DIM]
        recent_ref[j * MXU_DIM:(j + 1) * MXU_DIM, :] = _dot_t(eye_ref[...], cols).astype(BF16)
    per = MXU_DIM // d2
    for c in range(p // MXU_DIM):
        picked = _dot_t(sel_ref[...], x[:, c * MXU_DIM:(c + 1) * MXU_DIM]).astype(BF16)
        for t in range(t_new):
            far_ref[t * BAND + c * per:t * BAND + (c + 1) * per, :] = picked[t * per:(t + 1) * per, :]


def _compaction_job(cache_kv, t_new, mlp_f32):
    n, p = cache_kv.shape[:2]
    d1, d2 = WINDOWS[1][1], WINDOWS[2][1]
    per = MXU_DIM // d2
    assert p == WINDOWS[2][0] and p // d2 == BAND and t_new <= d1 and p % MXU_DIM == 0
    feat_major = jnp.transpose(cache_kv, (0, 2, 3, 4, 1))
    eye = jnp.eye(MXU_DIM, dtype=BF16)
    row = jnp.arange(t_new * per)[:, None]
    col = jnp.arange(MXU_DIM)[None, :]
    sel = jnp.where(col == (row % per) * d2 + row // per, 1.0, 0.0).astype(BF16)
    return dict(inputs=(feat_major, eye, sel), t_new=t_new, n=n, next=0, chunks=[],
                mlp_f32=tuple(mlp_f32), mlp_w={})


def _fused_body(*refs, body, n_in, n_out, t_new, n_cast):
    c_ref, eye_ref, sel_ref = refs[n_in:n_in + 3]
    cast_in = refs[n_in + 3:n_in + 3 + n_cast]
    rest = refs[n_in + 3 + n_cast:]
    recent_ref, far_ref = rest[n_out:n_out + 2]
    cast_out = rest[n_out + 2:n_out + 2 + n_cast]
    body(*refs[:n_in], *rest[:n_out], *rest[n_out + 2 + n_cast:])
    _compact_cache_kernel(c_ref, eye_ref, sel_ref, recent_ref, far_ref, t_new=t_new)
    for src, dst in zip(cast_in, cast_out):
        dst[...] = src[...].astype(BF16)


def _call_with_compaction(body, job, steps, step_of, *, grid, in_specs, out_specs, out_shape, args,
                          semantics, vmem_mib, name, scratch_shapes=(), cast_layer=None):
    if job is None:
        return pl.pallas_call(body, out_shape=out_shape, grid=grid, in_specs=in_specs, out_specs=out_specs,
                              scratch_shapes=scratch_shapes, compiler_params=_params(semantics, vmem_mib),
                              name=name)(*args)
    base, t_new = job["next"], job["t_new"]
    assert base + steps <= job["n"]
    job["next"] = base + steps
    feat, eye, sel = job["inputs"]
    n_out = len(out_shape)
    rows = (WINDOWS[1][1] * BAND, t_new * BAND)
    stacks = job["mlp_f32"] if cast_layer is not None else ()
    cast_in, cast_sds, cast_out = [], [], []
    for w in stacks:
        blk = (None, w.shape[1] // steps, w.shape[2])
        cast_in.append(pl.BlockSpec(blk, lambda *idx: (cast_layer, step_of(*idx), 0)))
        cast_sds.append(jax.ShapeDtypeStruct((1,) + w.shape[1:], BF16))
        cast_out.append(pl.BlockSpec(blk, lambda *idx: (0, step_of(*idx), 0)))
    res = pl.pallas_call(
        functools.partial(_fused_body, body=body, n_in=len(args), n_out=n_out, t_new=t_new,
                          n_cast=len(stacks)),
        out_shape=list(out_shape) + [jax.ShapeDtypeStruct((steps, r, KV_W), BF16) for r in rows] + cast_sds,
        grid=grid,
        in_specs=list(in_specs)
        + [pl.BlockSpec((None,) + feat.shape[1:], lambda *idx: (base + step_of(*idx), 0, 0, 0, 0)),
           _const_spec(eye.shape), _const_spec(sel.shape)] + cast_in,
        out_specs=list(out_specs)
        + [pl.BlockSpec((None, r, KV_W), lambda *idx: (step_of(*idx), 0, 0)) for r in rows] + cast_out,
        scratch_shapes=scratch_shapes,
        compiler_params=_params(semantics, vmem_mib + COMPACT_VMEM_MIB),
        name=name,
    )(*args, feat, eye, sel, *stacks)
    job["chunks"].append((base, *res[n_out:n_out + 2]))
    if stacks:
        job["mlp_w"][cast_layer] = tuple(res[n_out + 2:])
    return res[:n_out]


def _attn_sample_kernel(q_ref, kvn_ref, recent_ref, far_ref, b0_ref, b1_ref, b2_ref, b2n_ref, o_ref):
    t_new = q_ref.shape[0]
    n_seq = SAMPLE_SEQ_STEP
    base = (pl.program_id(0) % (SAMPLE_ROWS // n_seq)) * n_seq
    n_rows = t_new * N_HEADS
    hsel = (lax.broadcasted_iota(jnp.int32, (N_HEADS, HEADS_W), 1) // HEAD_DIM
            == lax.broadcasted_iota(jnp.int32, (N_HEADS, HEADS_W), 0))
    row8 = lax.broadcasted_iota(jnp.int32, (SUBLANES, KV_W), 0)

    qs, kas, vas, kbs, vbs = [], [], [], [], []
    for j in range(n_seq):
        b = base + j
        qs.append([jnp.concatenate(
            [jnp.where(hsel, q_ref[t, pl.ds(b, 1), _lane_block(grp, HEADS_W)], 0.0) for t in range(t_new)],
            axis=0).astype(BF16) for grp in range(N_GROUPS)])
        new8 = jnp.zeros((SUBLANES, KV_W), F32)
        for t in range(t_new):
            new8 = jnp.where(row8 == t, kvn_ref[t, pl.ds(b, 1), :], new8)
        new = jnp.concatenate([new8, jnp.zeros((BAND - SUBLANES, KV_W), F32)], axis=0).astype(BF16)
        recent = recent_ref[j]
        kas.append(jnp.concatenate([recent[:, :HEADS_W], new[:, :HEADS_W]], axis=0))
        vas.append(jnp.concatenate([recent[:, HEADS_W:], new[:, HEADS_W:]], axis=0))
        kbs.append(far_ref[j, :, :HEADS_W])
        vbs.append(far_ref[j, :, HEADS_W:])
    n_a = kas[0].shape[0]

    def scores(grp, keys, lo, bias_ref):
        return jnp.concatenate([_dot_t(qs[j][grp], keys[j][lo:]) for j in range(n_seq)],
                               axis=0) + bias_ref[...]

    def softmax_pv(pieces):
        m = functools.reduce(jnp.maximum, [jnp.max(s, axis=-1, keepdims=True) for s, _, _ in pieces])
        l = jnp.zeros((n_seq * n_rows, 1), F32)
        acc = jnp.zeros((n_seq * n_rows, HEADS_W), F32)
        for s, vals, lo in pieces:
            p = jnp.exp2(s - m)
            l = l + jnp.sum(p, axis=-1, keepdims=True)
            pb = p.astype(BF16)
            acc = acc + jnp.concatenate(
                [jnp.dot(pb[j * n_rows:(j + 1) * n_rows], vals[j][lo:], preferred_element_type=F32)
                 for j in range(n_seq)], axis=0)
        return acc, m, l

    lo0, lo_new = n_a - 2 * BAND, n_a - BAND
    parts = [
        softmax_pv([(scores(0, kas, lo0, b0_ref), vas, lo0)]),
        softmax_pv([(scores(1, kas, 0, b1_ref), vas, 0)]),
        softmax_pv([(scores(2, kbs, 0, b2_ref), vbs, 0), (scores(2, kas, lo_new, b2n_ref), vas, lo_new)]),
    ]
    m_all = functools.reduce(jnp.maximum, [m for _, m, _ in parts])
    num = jnp.zeros((n_seq * n_rows, HEADS_W), F32)
    den = jnp.zeros((n_seq * n_rows, 1), F32)
    for acc, m, l in parts:
        e = jnp.exp2(m - m_all)
        num = num + e * acc
        den = den + e * l
    out = num * (1.0 / den)
    for j in range(n_seq):
        for t in range(t_new):
            r0 = j * n_rows + t * N_HEADS
            own = jnp.where(hsel, out[r0:r0 + N_HEADS, :], 0.0)
            o_ref[t, pl.ds(base + j, 1), :] = jnp.sum(own, axis=0, keepdims=True)


def _attn_sample(q, kvn, chunks, biases):
    t_new, n, qw = q.shape
    step = SAMPLE_SEQ_STEP
    per_blk = SAMPLE_ROWS // step
    biases = [jnp.tile(b, (step, 1)) for b in biases]
    outs = []
    for first, recent, far in chunks:
        count = recent.shape[0]
        assert first % SAMPLE_ROWS == 0 and count % SAMPLE_ROWS == 0
        blk0 = first // SAMPLE_ROWS
        blk = lambda width: pl.BlockSpec((t_new, SAMPLE_ROWS, width), lambda i: (0, blk0 + i // per_blk, 0))
        rows = lambda a: pl.BlockSpec((step,) + a.shape[1:], lambda i: (i, 0, 0))
        outs.append(pl.pallas_call(
            _attn_sample_kernel,
            out_shape=jax.ShapeDtypeStruct((t_new, count, HEADS_W), F32),
            grid=(count // step,),
            in_specs=[blk(qw), blk(KV_W), rows(recent), rows(far)] + [_const_spec(b.shape) for b in biases],
            out_specs=pl.BlockSpec((t_new, SAMPLE_ROWS, HEADS_W), lambda i: (0, i // per_blk, 0)),
            compiler_params=_params(("arbitrary",), 40),
            name="attn_sample",
        )(q, kvn, recent, far, *biases))
    return jnp.concatenate(outs, axis=1)


def _attn_out_sample_kernel(x_ref, o_ref, wo_ref, g_ref, wup_ref, wdn_ref, out_ref):
    x = x_ref[...] + jnp.dot(o_ref[...].astype(BF16), wo_ref[...], preferred_element_type=F32)
    out_ref[...] = _mlp_residual(x, g_ref, wup_ref, wdn_ref)


def _attn_out_prompt_kernel(x_ref, a0_ref, m0_ref, l0_ref, a1_ref, m1_ref, l1_ref, a2_ref, m2_ref, l2_ref,
                            ex_ref, wo_ref, g_ref, wup_ref, wdn_ref, out_ref, slab_ref):
    rows = x_ref.shape[0]
    n_o = HEADS_W // LANES
    for gi, (a_ref, m_ref, l_ref) in enumerate(((a1_ref, m1_ref, l1_ref), (a2_ref, m2_ref, l2_ref))):
        d = WINDOWS[gi + 1][1]
        for r in range(d):
            dst = pl.ds(r, rows // d, stride=d)
            for s in range(n_o):
                slab_ref[gi, s, dst, :] = a_ref[r, :, _lane_block(s)]
            slab_ref[gi, n_o, dst, :] = m_ref[r]
            slab_ref[gi, n_o + 1, dst, :] = l_ref[r]
    accs = [a0_ref[0]] + [jnp.concatenate([slab_ref[gi, s] for s in range(n_o)], axis=1) for gi in range(2)]
    ms = [m0_ref[0], slab_ref[0, n_o], slab_ref[1, n_o]]
    ls = [l0_ref[0], slab_ref[0, n_o + 1], slab_ref[1, n_o + 1]]
    mx = functools.reduce(jnp.maximum, ms)
    es = [jnp.exp2(m - mx) for m in ms]
    inv = 1.0 / functools.reduce(jnp.add, [e * l for e, l in zip(es, ls)])
    o = jnp.zeros((rows, HEADS_W), F32)
    for e, og in zip(es, accs):
        w = e * inv
        hi = w.astype(BF16)
        lo = (w - hi.astype(F32)).astype(BF16)
        wide = (jnp.dot(hi, ex_ref[...], preferred_element_type=F32)
                + jnp.dot(lo, ex_ref[...], preferred_element_type=F32))
        o = o + wide * og
    x = x_ref[...] + jnp.dot(o.astype(BF16), wo_ref[...], preferred_element_type=F32)
    out_ref[...] = _mlp_residual(x, g_ref, wup_ref, wdn_ref)


def _attn_out_q_prompt_kernel(*refs):
    n_in = 15
    gm_ref, wq_ref, e_ref, gq_ref = refs[n_in:n_in + 4]
    out_ref, q0_ref, q1_ref, q2_ref, slab_ref, qslab_ref = refs[n_in + 4:]
    _attn_out_prompt_kernel(*refs[:n_in], out_ref, slab_ref)
    h = _rms(out_ref[...], gm_ref[...]).astype(BF16)
    _q_prompt_store(h, wq_ref, e_ref, gq_ref, q0_ref, q1_ref, q2_ref, qslab_ref)


def _mlp_weight_specs(layer, j, wo, g, wup, wdn):
    return [_layer_spec(wo, j), _layer_spec(g, layer), _layer_spec(wup, 0), _layer_spec(wdn, 0)]


def _attn_out_sample(x, o, layer, j, wo, g, wup, wdn):
    m = x.shape[0]
    row = pl.BlockSpec((ROW_TILE, D_MODEL), lambda i: (i, 0))
    return pl.pallas_call(
        _attn_out_sample_kernel,
        out_shape=jax.ShapeDtypeStruct((m, D_MODEL), F32),
        grid=(m // ROW_TILE,),
        in_specs=[row, pl.BlockSpec((ROW_TILE, HEADS_W), lambda i: (i, 0))]
        + _mlp_weight_specs(layer, j, wo, g, wup, wdn),
        out_specs=row,
        compiler_params=_params(("arbitrary",), 48),
        name="attn_out_sample",
    )(x, o, wo, g, wup, wdn)


def _attn_out_prompt(x, parts, ex, layer, j, wo, g, wup, wdn, next_q=None):
    n, t, _ = x.shape
    tile = ATT_OUT_TILE
    assert tile == ROW_TILE
    row = pl.BlockSpec((None, tile, D_MODEL), lambda b, i: (b, i, 0))
    part_specs = []
    for _, d in WINDOWS:
        for width in (HEADS_W, LANES, LANES):
            part_specs.append(pl.BlockSpec((None, d, tile // d, width), lambda b, i: (b, 0, i, 0)))
    flat = [a for part in parts for a in part]
    in_specs = ([row] + part_specs + [_const_spec((LANES, HEADS_W))]
                + _mlp_weight_specs(layer, j, wo, g, wup, wdn))
    args = [x, *flat, ex, wo, g, wup, wdn]
    out_shape, out_specs = [jax.ShapeDtypeStruct((n, t, D_MODEL), F32)], [row]
    scratch = [pltpu.VMEM((2, HEADS_W // LANES + 2, tile, LANES), F32)]
    body = _attn_out_prompt_kernel
    if next_q is not None:
        g_mix, wq, e, gq = next_q
        in_specs += [_layer_spec(g_mix, layer + 1), _layer_spec(wq, j + 1), _const_spec(e.shape),
                     _layer_spec(gq, j + 1)]
        args += [g_mix, wq, e, gq]
        q_sds, q_specs = _dilated_shapes(n, t, HEADS_W, BF16)
        out_shape, out_specs = out_shape + q_sds, out_specs + q_specs
        scratch.append(pltpu.VMEM((HEADS_W // LANES, tile, LANES), F32))
        body = _attn_out_q_prompt_kernel
    res = pl.pallas_call(
        body,
        out_shape=out_shape,
        grid=(n, t // tile),
        in_specs=in_specs,
        out_specs=out_specs,
        scratch_shapes=scratch,
        compiler_params=_params(("arbitrary", "arbitrary"), 56),
        name="attn_out_prompt",
    )(*args)
    return res[0], res[1:]


def _alibi_slopes():
    n = N_GROUPS * N_HEADS
    s = 2.0 ** (-8.0 * jnp.arange(1, n + 1, dtype=F32) / n)
    return s.reshape(N_GROUPS, N_HEADS)


def _prompt_bias(slopes, grp):
    d = WINDOWS[grp][1]
    iq = jnp.arange(BAND, dtype=jnp.int32)[:, None]
    ik = jnp.arange(2 * BAND, dtype=jnp.int32)[None, :]
    steps = iq + BAND - ik
    valid = (steps >= 0) & (steps <= BAND)
    offs = (steps * d).astype(F32)
    per_head = jnp.where(valid[None], -slopes[grp][:, None, None] * offs[None] * LOG2_E, NEG_INF)
    return per_head.reshape(N_HEADS // 2, 2 * BAND, 2 * BAND)


def _sample_bias(slopes, t_new, past):
    d1, d2 = WINDOWS[1][1], WINDOWS[2][1]
    t = jnp.arange(t_new, dtype=jnp.int32)[:, None]

    def table(slope, dist, valid):
        b = jnp.where(valid[:, None, :], -slope[None, :, None] * dist.astype(F32)[:, None, :] * LOG2_E,
                      NEG_INF)
        return b.reshape(t_new * N_HEADS, -1)

    u = jnp.arange(BAND, dtype=jnp.int32)[None, :]
    dist_n = t - u
    ok_n = (u < t_new) & (dist_n >= 0)
    c = jnp.arange(BAND, dtype=jnp.int32)[None, :]
    dist_c = BAND + t - c
    b0 = table(slopes[0], jnp.concatenate([dist_c, dist_n], axis=1),
               jnp.concatenate([dist_c <= WINDOWS[0][0], ok_n], axis=1))
    c = jnp.arange(d1 * BAND, dtype=jnp.int32)[None, :]
    dist_c = d1 * BAND + t - c
    ok_c = (dist_c % d1 == 0) & (dist_c <= WINDOWS[1][0])
    b1 = table(slopes[1], jnp.concatenate([dist_c, dist_n], axis=1),
               jnp.concatenate([ok_c, ok_n & (dist_n % d1 == 0)], axis=1))
    per = past // d2
    col = jnp.arange(t_new * per, dtype=jnp.int32)[None, :]
    pos = (col % per) * d2 + col // per
    dist_c = past + t - pos
    ok_c = (dist_c % d2 == 0) & (dist_c <= WINDOWS[2][0])
    b2 = table(slopes[2], dist_c, ok_c)
    b2n = table(slopes[2], dist_n, ok_n & (dist_n % d2 == 0))
    return b0, b1, b2, b2n


def _head_mean_matrix():
    idx = jnp.arange(HEADS_W) // HEAD_DIM
    return jnp.where(idx[:, None] == idx[None, :], 1.0 / HEAD_DIM, 0.0).astype(BF16)


def _head_expand_matrix():
    lane = jnp.arange(LANES)[:, None]
    col = jnp.arange(HEADS_W)[None, :] // HEAD_DIM
    return jnp.where(lane == col, 1.0, 0.0).astype(BF16)


def kernel(x_prompt, x_sample, state_conv, cache_kv, norm_mix_g, norm_mlp_g, conv_w_pw1, conv_b_pw1,
           conv_w_dw, conv_b_dw, conv_ln_g, conv_ln_b, conv_w_pw2, conv_b_pw2, kv_norm_g, w_kv,
           k_norm_g, attn_w_q, q_norm_g, attn_w_o, mlp_w_up, mlp_w_down):
    n_p, t_p, _ = x_prompt.shape
    n_s, t_s, _ = x_sample.shape
    past = cache_kv.shape[1]
    scale = HEAD_DIM ** -0.5 * LOG2_E

    w1 = conv_w_pw1.astype(BF16)
    w2 = conv_w_pw2.astype(BF16)
    wkv = w_kv.astype(BF16)
    wq = attn_w_q.astype(BF16)
    wo = attn_w_o.astype(BF16)
    vec = lambda a: a.reshape(-1, 1, a.shape[-1])
    g_mix, g_mlp, g_kv = vec(norm_mix_g), vec(norm_mlp_g), vec(kv_norm_g)
    e_mat = _head_mean_matrix()
    ex_mat = _head_expand_matrix()
    gk = vec(jnp.tile(k_norm_g, N_HEADS))
    gq = vec(jnp.tile(q_norm_g, (1, N_HEADS)) * scale)
    wkv = wkv[None]
    slopes = _alibi_slopes()
    bias_p = [_prompt_bias(slopes, g) for g in range(N_GROUPS)]
    bias_s = _sample_bias(slopes, t_s, past)
    conv_weights = (g_mix, w1, vec(conv_b_pw1), conv_w_dw, vec(conv_b_dw), vec(conv_ln_g),
                    vec(conv_ln_b), w2, vec(conv_b_pw2))

    job = _compaction_job(cache_kv, t_s, (mlp_w_up, mlp_w_down))
    mlp_w = job["mlp_w"]
    xp = x_prompt
    conv_p = []
    zero_hist = jnp.zeros((n_p, HIST_PAD, CONV_CH), F32)
    for layer in range(N_A_LAYERS):
        xp, st = _conv_prompt(xp, zero_hist, layer, conv_weights, job, cast_layer=layer)
        conv_p.append(st[:, HIST_PAD - CONV_HIST:])
        xp = _mlp(xp.reshape(n_p * t_p, D_MODEL), layer, g_mlp, *mlp_w[layer], job,
                  cast_layer=N_A_LAYERS + layer).reshape(n_p, t_p, D_MODEL)
    assert job["next"] == job["n"] and len(mlp_w) == N_A_LAYERS + N_B_LAYERS
    win = min(WINDOWS[-1][0], t_p)
    kv_p, kvb_p, qs = _kvq_proj_prompt(xp, g_kv, wkv, e_mat, gk, win, N_A_LAYERS, g_mix, wq, gq)
    for j in range(N_B_LAYERS):
        layer = N_A_LAYERS + j
        parts = [_attn_group(qs[g], kvb_p[g], bias_p[g], g) for g in range(N_GROUPS)]
        next_q = (g_mix, wq, e_mat, gq) if j + 1 < N_B_LAYERS else None
        xp, qs = _attn_out_prompt(xp, parts, ex_mat, layer, j, wo, g_mlp, *mlp_w[layer], next_q)
    y_prompt = xp
    conv_prompt = jnp.stack(conv_p, axis=0)
    kv_prompt = kv_p.reshape(n_p, win, 2, N_HEADS, HEAD_DIM)

    xs = jnp.transpose(x_sample, (1, 0, 2))
    st_in = jnp.transpose(state_conv, (0, 2, 1, 3))
    new_states = []
    for layer in range(N_A_LAYERS):
        prev = new_states if layer == N_A_LAYERS - 1 else ()
        xs, st = _conv_sample(xs, st_in, layer, conv_weights, prev)
        new_states.append(st)
        xs = _mlp(xs.reshape(t_s * n_s, D_MODEL), layer, g_mlp, *mlp_w[layer]).reshape(t_s, n_s, D_MODEL)
    xs = xs.reshape(t_s * n_s, D_MODEL)
    kvn = _kv_proj_sample(xs, g_kv, wkv, e_mat, gk).reshape(t_s, n_s, KV_W)
    for j in range(N_B_LAYERS):
        layer = N_A_LAYERS + j
        q = _q_proj_sample(xs, layer, j, g_mix, wq, e_mat, gq)
        o = _attn_sample(q.reshape(t_s, n_s, N_GROUPS * HEADS_W), kvn, job["chunks"], bias_s)
        xs = _attn_out_sample(xs, o.reshape(t_s * n_s, HEADS_W), layer, j, wo, g_mlp, *mlp_w[layer])
    y_sample = jnp.transpose(xs.reshape(t_s, n_s, D_MODEL), (1, 0, 2))
    st_stack = new_states[-1] if N_A_LAYERS > 1 else new_states[-1][None]
    conv_sample = jnp.transpose(st_stack, (0, 2, 1, 3))
    kv_sample = jnp.transpose(kvn, (1, 0, 2)).reshape(n_s, t_s, 2, N_HEADS, HEAD_DIM)

    return (y_prompt, y_sample, conv_prompt, conv_sample, kv_prompt, kv_sample)
```

```python
import functools

import jax
import jax.numpy as jnp
from jax import lax
from jax.experimental import pallas as pl
from jax.experimental.pallas import tpu as pltpu

F32 = jnp.float32
BF16 = jnp.bfloat16

D_MODEL = 1024
CONV_CH = 1024
CONV_WIDTH = 31
CONV_HIST = CONV_WIDTH - 1
HIST_PAD = 32
D_FF = 4096
N_HEADS = 8
HEAD_DIM = 64
HEADS_W = N_HEADS * HEAD_DIM
KV_W = 2 * HEADS_W
WINDOWS = ((128, 1), (512, 4), (2048, 16))
N_GROUPS = len(WINDOWS)
BAND = 128
N_A_LAYERS = 2
N_B_LAYERS = 2
NORM_EPS = 1e-6
NEG_INF = float("-inf")
LOG2_E = 1.4426950408889634
LANES = 128
SUBLANES = 8

ROW_TILE = 512
ATT_OUT_TILE = 512
FF_CHUNK = 1024
ATT_QB = 1024
SAMPLE_CONV_NB = 32
SAMPLE_ROWS = 8
SAMPLE_SEQ_STEP = 4
MXU_DIM = 256
COMPACT_VMEM_MIB = 24
MIB = 1024 * 1024

assert all(w // d == BAND for w, d in WINDOWS)


def _const_spec(shape):
    nd = len(shape)
    return pl.BlockSpec(shape, lambda *_: (0,) * nd, pipeline_mode=pl.Buffered(1))


def _layer_spec(stacked, layer):
    return pl.BlockSpec((None,) + stacked.shape[1:], lambda *_: (layer, 0, 0),
                        pipeline_mode=pl.Buffered(1))


def _params(semantics, vmem_mib):
    return pltpu.CompilerParams(dimension_semantics=semantics, vmem_limit_bytes=vmem_mib * MIB)


def _lane_block(s, width=LANES):
    return slice(s * width, (s + 1) * width)


def _rms(x, g):
    return x * lax.rsqrt(jnp.mean(x * x, axis=-1, keepdims=True) + NORM_EPS) * g


def _mlp_residual(x, g_ref, wup_ref, wdn_ref):
    h = _rms(x, g_ref[...]).astype(BF16)
    acc = x
    for c in range(D_FF // FF_CHUNK):
        sl = _lane_block(c, FF_CHUNK)
        z = jnp.maximum(jnp.dot(h, wup_ref[:, sl], preferred_element_type=F32), 0.0)
        acc = acc + jnp.dot((z * z).astype(BF16), wdn_ref[sl, :], preferred_element_type=F32)
    return acc


def _head_mean_sq(v, e_ref):
    return jnp.dot((v * v).astype(BF16), e_ref[...], preferred_element_type=F32)


def _dot_t(a, b):
    return lax.dot_general(a, b, (((1,), (1,)), ((), ())), preferred_element_type=F32)


def _deinterleave_store(val, slab_ref, out_ref, d):
    rows, width = val.shape
    for s in range(width // LANES):
        slab_ref[s] = val[:, _lane_block(s)]
    for s in range(width // LANES):
        for r in range(d):
            out_ref[r, :, _lane_block(s)] = slab_ref[s, pl.ds(r, rows // d, stride=d), :].astype(BF16)


def _mlp_kernel(x_ref, g_ref, wup_ref, wdn_ref, o_ref):
    o_ref[...] = _mlp_residual(x_ref[...], g_ref, wup_ref, wdn_ref)


def _mlp(x, layer, g, wup, wdn, job=None, cast_layer=None):
    m = x.shape[0]
    row = pl.BlockSpec((ROW_TILE, D_MODEL), lambda i: (i, 0))
    return _call_with_compaction(
        _mlp_kernel, job, m // ROW_TILE, lambda i: i,
        out_shape=[jax.ShapeDtypeStruct((m, D_MODEL), F32)],
        grid=(m // ROW_TILE,),
        in_specs=[row, _layer_spec(g, layer), _layer_spec(wup, 0), _layer_spec(wdn, 0)],
        out_specs=[row],
        args=(x, g, wup, wdn),
        semantics=("arbitrary",), vmem_mib=34, name="mlp", cast_layer=cast_layer,
    )[0]


def _glu(h, w1_ref, b1_ref):
    u = jnp.dot(h, w1_ref[...], preferred_element_type=F32) + b1_ref[...]
    return u[:, :CONV_CH] * jax.nn.sigmoid(u[:, CONV_CH:])


def _conv_tail(c, x, bdw_ref, lng_ref, lnb_ref, w2_ref, b2_ref):
    c = c + bdw_ref[...]
    mu = jnp.mean(c, axis=-1, keepdims=True)
    xc = c - mu
    y = xc * lax.rsqrt(jnp.mean(xc * xc, axis=-1, keepdims=True) + NORM_EPS)
    y = y * lng_ref[...] + lnb_ref[...]
    y = y * jax.nn.sigmoid(y)
    out = jnp.dot(y.astype(BF16), w2_ref[...], preferred_element_type=F32) + b2_ref[...]
    return x + out


def _conv_prompt_kernel(x_ref, st_ref, g_ref, w1_ref, b1_ref, wdw_ref, bdw_ref, lng_ref, lnb_ref,
                        w2_ref, b2_ref, o_ref, sto_ref, buf_ref):
    n_slabs = CONV_CH // LANES

    @pl.when(pl.program_id(1) == 0)
    def _():
        for s in range(n_slabs):
            buf_ref[s, 0:HIST_PAD, :] = st_ref[:, _lane_block(s)]

    x = x_ref[...]
    h = _rms(x, g_ref[...]).astype(BF16)
    u = _glu(h, w1_ref, b1_ref)
    for s in range(n_slabs):
        buf_ref[s, HIST_PAD:HIST_PAD + ROW_TILE, :] = u[:, _lane_block(s)]
    off = HIST_PAD - CONV_HIST
    cs = []
    for s in range(n_slabs):
        ls = _lane_block(s)
        c = buf_ref[s, off:off + ROW_TILE, :] * wdw_ref[0:1, ls]
        for k in range(1, CONV_WIDTH):
            c = c + buf_ref[s, off + k:off + k + ROW_TILE, :] * wdw_ref[k:k + 1, ls]
        cs.append(c)
    c = jnp.concatenate(cs, axis=1)
    o_ref[...] = _conv_tail(c, x, bdw_ref, lng_ref, lnb_ref, w2_ref, b2_ref)
    for s in range(n_slabs):
        last = buf_ref[s, ROW_TILE:ROW_TILE + HIST_PAD, :]
        sto_ref[:, _lane_block(s)] = last
        buf_ref[s, 0:HIST_PAD, :] = last


def _conv_prompt(x, st, layer, weights, job=None, cast_layer=None):
    n, t, _ = x.shape
    tiles = t // ROW_TILE
    row = pl.BlockSpec((None, ROW_TILE, D_MODEL), lambda b, i: (b, i, 0))
    st_spec = pl.BlockSpec((None, HIST_PAD, CONV_CH), lambda b, i: (b, 0, 0))
    return _call_with_compaction(
        _conv_prompt_kernel, job, n * tiles, lambda b, i: b * tiles + i,
        out_shape=[jax.ShapeDtypeStruct((n, t, D_MODEL), F32),
                   jax.ShapeDtypeStruct((n, HIST_PAD, CONV_CH), F32)],
        grid=(n, tiles),
        in_specs=[row, st_spec] + [_layer_spec(a, layer) for a in weights],
        out_specs=[row, st_spec],
        scratch_shapes=[pltpu.VMEM((CONV_CH // LANES, HIST_PAD + ROW_TILE, LANES), F32)],
        args=(x, st, *weights),
        semantics=("arbitrary", "arbitrary"), vmem_mib=32, name="conv_prompt", cast_layer=cast_layer,
    )


def _conv_sample_kernel(x_ref, st_ref, g_ref, w1_ref, b1_ref, wdw_ref, bdw_ref, lng_ref, lnb_ref,
                        w2_ref, b2_ref, *rest):
    *prev_refs, o_ref, sto_ref = rest
    t_new, nb, _ = x_ref.shape
    x = x_ref[...].reshape(t_new * nb, D_MODEL)
    h = _rms(x, g_ref[...]).astype(BF16)
    glu = _glu(h, w1_ref, b1_ref)
    new = [glu[t * nb:(t + 1) * nb, :] for t in range(t_new)]

    def full(j):
        return st_ref[j] if j < CONV_HIST else new[j - CONV_HIST]

    outs = []
    for t in range(t_new):
        c = full(t) * wdw_ref[0:1, :]
        for k in range(1, CONV_WIDTH):
            c = c + full(t + k) * wdw_ref[k:k + 1, :]
        outs.append(c)
    c = jnp.concatenate(outs, axis=0)
    o_ref[...] = _conv_tail(c, x, bdw_ref, lng_ref, lnb_ref, w2_ref, b2_ref).reshape(t_new, nb, D_MODEL)
    if prev_refs:
        for l, prev_ref in enumerate(prev_refs):
            sto_ref[l] = prev_ref[...]
        for j in range(CONV_HIST):
            sto_ref[len(prev_refs), j] = full(j + t_new)
    else:
        for j in range(CONV_HIST):
            sto_ref[j] = full(j + t_new)


def _conv_sample(x, st_all, layer, weights, prev_states=()):
    t_new, n, _ = x.shape
    nb = SAMPLE_CONV_NB
    x_spec = pl.BlockSpec((t_new, nb, D_MODEL), lambda i: (0, i, 0))
    st_spec = pl.BlockSpec((None, CONV_HIST, nb, CONV_CH), lambda i: (layer, 0, i, 0))
    one_state = pl.BlockSpec((CONV_HIST, nb, CONV_CH), lambda i: (0, i, 0))
    if prev_states:
        n_stack = len(prev_states) + 1
        st_sds = jax.ShapeDtypeStruct((n_stack, CONV_HIST, n, CONV_CH), F32)
        sto_spec = pl.BlockSpec((n_stack, CONV_HIST, nb, CONV_CH), lambda i: (0, 0, i, 0))
    else:
        st_sds = jax.ShapeDtypeStruct((CONV_HIST, n, CONV_CH), F32)
        sto_spec = one_state
    return pl.pallas_call(
        _conv_sample_kernel,
        out_shape=(jax.ShapeDtypeStruct((t_new, n, D_MODEL), F32), st_sds),
        grid=(n // nb,),
        in_specs=[x_spec, st_spec] + [_layer_spec(a, layer) for a in weights] + [one_state] * len(prev_states),
        out_specs=(x_spec, sto_spec),
        compiler_params=_params(("arbitrary",), 48),
        name="conv_sample",
    )(x, st_all, *weights, *prev_states)


def _kv_rows(x_ref, g_ref, wkv_ref, e_ref, gk_ref):
    h = _rms(x_ref[...], g_ref[...]).astype(BF16)
    k = jnp.dot(h, wkv_ref[:, :HEADS_W], preferred_element_type=F32)
    v = jnp.dot(h, wkv_ref[:, HEADS_W:], preferred_element_type=F32)
    k = k * lax.rsqrt(_head_mean_sq(k, e_ref) + NORM_EPS) * gk_ref[...]
    return jnp.concatenate([k, v], axis=1)


def _kv_sample_kernel(x_ref, g_ref, wkv_ref, e_ref, gk_ref, kv_ref):
    kv_ref[...] = _kv_rows(x_ref, g_ref, wkv_ref, e_ref, gk_ref)


def _kvq_prompt_kernel(x_ref, g_ref, wkv_ref, e_ref, gk_ref, gm_ref, wq_ref, gq_ref,
                       kv_ref, kb0_ref, kb1_ref, kb2_ref, q0_ref, q1_ref, q2_ref, slab_ref):
    kv = _kv_rows(x_ref, g_ref, wkv_ref, e_ref, gk_ref)
    kv_ref[...] = kv
    kb0_ref[0] = kv.astype(BF16)
    _deinterleave_store(kv, slab_ref, kb1_ref, WINDOWS[1][1])
    _deinterleave_store(kv, slab_ref, kb2_ref, WINDOWS[2][1])
    h = _rms(x_ref[...], gm_ref[...]).astype(BF16)
    _q_prompt_store(h, wq_ref, e_ref, gq_ref, q0_ref, q1_ref, q2_ref, slab_ref)


def _proj_weight_specs(layer, g, w, e, gh):
    return [_layer_spec(g, layer), _layer_spec(w, layer), _const_spec(e.shape), _layer_spec(gh, layer)]


def _kv_proj_sample(x, g, wkv, e, gk):
    m = x.shape[0]
    return pl.pallas_call(
        _kv_sample_kernel,
        out_shape=jax.ShapeDtypeStruct((m, KV_W), F32),
        grid=(m // ROW_TILE,),
        in_specs=[pl.BlockSpec((ROW_TILE, D_MODEL), lambda i: (i, 0))] + _proj_weight_specs(0, g, wkv, e, gk),
        out_specs=pl.BlockSpec((ROW_TILE, KV_W), lambda i: (i, 0)),
        compiler_params=_params(("arbitrary",), 32),
        name="kv_proj_sample",
    )(x, g, wkv, e, gk)


def _dilated_shapes(n, t, width, dtype):
    sds, specs = [], []
    for _, d in WINDOWS:
        sds.append(jax.ShapeDtypeStruct((n, d, t // d, width), dtype))
        specs.append(pl.BlockSpec((None, d, ROW_TILE // d, width), lambda b, i: (b, 0, i, 0)))
    return sds, specs


def _kvq_proj_prompt(x, g, wkv, e, gk, win, layer, g_mix, wq, gq):
    n, t, _ = x.shape
    kv_sds, kv_specs = _dilated_shapes(n, t, KV_W, BF16)
    q_sds, q_specs = _dilated_shapes(n, t, HEADS_W, BF16)
    skip = (t - win) // ROW_TILE
    assert win % ROW_TILE == 0 and t % ROW_TILE == 0
    f32_spec = pl.BlockSpec((None, ROW_TILE, KV_W), lambda b, i: (b, jnp.maximum(i - skip, 0), 0))
    res = pl.pallas_call(
        _kvq_prompt_kernel,
        out_shape=[jax.ShapeDtypeStruct((n, win, KV_W), F32)] + kv_sds + q_sds,
        grid=(n, t // ROW_TILE),
        in_specs=[pl.BlockSpec((None, ROW_TILE, D_MODEL), lambda b, i: (b, i, 0))]
        + _proj_weight_specs(0, g, wkv, e, gk)
        + [_layer_spec(g_mix, layer), _layer_spec(wq, 0), _layer_spec(gq, 0)],
        out_specs=[f32_spec] + kv_specs + q_specs,
        scratch_shapes=[pltpu.VMEM((KV_W // LANES, ROW_TILE, LANES), F32)],
        compiler_params=_params(("arbitrary", "arbitrary"), 48),
        name="kvq_proj_prompt",
    )(x, g, wkv, e, gk, g_mix, wq, gq)
    return res[0], res[1:4], res[4:7]


def _q_group(h, wq_ref, e_ref, gq_ref, grp):
    q = jnp.dot(h, wq_ref[:, _lane_block(grp, HEADS_W)], preferred_element_type=F32)
    return q * lax.rsqrt(_head_mean_sq(q, e_ref) + NORM_EPS) * gq_ref[...]


def _q_sample_kernel(x_ref, g_ref, wq_ref, e_ref, gq_ref, q_ref):
    h = _rms(x_ref[...], g_ref[...]).astype(BF16)
    for grp in range(N_GROUPS):
        q = _q_group(h, wq_ref, e_ref, gq_ref, grp)
        q_ref[:, _lane_block(grp, HEADS_W)] = q.astype(BF16).astype(F32)


def _q_prompt_store(h, wq_ref, e_ref, gq_ref, q0_ref, q1_ref, q2_ref, slab_ref):
    q0_ref[0] = _q_group(h, wq_ref, e_ref, gq_ref, 0).astype(BF16)
    _deinterleave_store(_q_group(h, wq_ref, e_ref, gq_ref, 1), slab_ref, q1_ref, WINDOWS[1][1])
    _deinterleave_store(_q_group(h, wq_ref, e_ref, gq_ref, 2), slab_ref, q2_ref, WINDOWS[2][1])


def _q_proj_sample(x, layer, j, g, wq, e, gq):
    m = x.shape[0]
    qw = N_GROUPS * HEADS_W
    specs = [_layer_spec(g, layer), _layer_spec(wq, j), _const_spec(e.shape), _layer_spec(gq, j)]
    return pl.pallas_call(
        _q_sample_kernel,
        out_shape=jax.ShapeDtypeStruct((m, qw), F32),
        grid=(m // ROW_TILE,),
        in_specs=[pl.BlockSpec((ROW_TILE, D_MODEL), lambda i: (i, 0))] + specs,
        out_specs=pl.BlockSpec((ROW_TILE, qw), lambda i: (i, 0)),
        compiler_params=_params(("arbitrary",), 32),
        name="q_proj_sample",
    )(x, g, wq, e, gq)


def _attn_group_kernel(q_ref, kc_ref, kp_ref, vc_ref, vp_ref, bias_ref, acc_ref, m_ref, l_ref):
    for rr in range(q_ref.shape[0]):
        _attn_residue(q_ref.at[rr], kc_ref.at[rr], kp_ref.at[rr], vc_ref.at[rr], vp_ref.at[rr], bias_ref,
                      acc_ref.at[rr], m_ref.at[rr], l_ref.at[rr])


def _attn_residue(q_ref, kc_ref, kp_ref, vc_ref, vp_ref, bias_ref, acc_ref, m_ref, l_ref):
    first_block = pl.program_id(2) == 0
    key_col = lax.broadcasted_iota(jnp.int32, (2 * BAND, 2 * BAND), 1)
    lane = lax.broadcasted_iota(jnp.int32, (BAND, LANES), 1)
    left = lane < HEAD_DIM
    lane_row = lax.broadcasted_iota(jnp.int32, (1, LANES), 1)
    head_keep = [jnp.where(lane_row < HEAD_DIM, 1.0, 0.0).astype(BF16),
                 jnp.where(lane_row < HEAD_DIM, 0.0, 1.0).astype(BF16)]
    no_history = jnp.logical_and(first_block, key_col < BAND)

    for sb in range(q_ref.shape[0] // BAND):
        r0 = sb * BAND
        m_tile = jnp.zeros((BAND, LANES), F32)
        l_tile = jnp.ones((BAND, LANES), F32)
        for pair in range(N_HEADS // 2):
            ls = _lane_block(pair)
            q2 = q_ref[r0:r0 + BAND, ls]
            if sb == 0:
                k2 = jnp.concatenate([kp_ref[:, ls], kc_ref[0:BAND, ls]], axis=0)
                v2 = jnp.concatenate([vp_ref[:, ls], vc_ref[0:BAND, ls]], axis=0)
            else:
                k2 = kc_ref[r0 - BAND:r0 + BAND, ls]
                v2 = vc_ref[r0 - BAND:r0 + BAND, ls]
            qq = jnp.concatenate([q2 * head_keep[0], q2 * head_keep[1]], axis=0)
            s = _dot_t(qq, k2) + bias_ref[pair]
            if sb == 0:
                s = jnp.where(no_history, NEG_INF, s)
            m = jnp.max(s, axis=-1, keepdims=True)
            p = jnp.exp2(s - m)
            l = jnp.sum(p, axis=-1, keepdims=True)
            a = jnp.dot(p.astype(BF16), v2, preferred_element_type=F32)
            acc_ref[r0:r0 + BAND, ls] = jnp.where(left, a[:BAND], a[BAND:])
            for hh in range(2):
                rows = slice(hh * BAND, (hh + 1) * BAND)
                m_tile = jnp.where(lane == 2 * pair + hh, m[rows], m_tile)
                l_tile = jnp.where(lane == 2 * pair + hh, l[rows], l_tile)
        m_ref[r0:r0 + BAND, :] = m_tile
        l_ref[r0:r0 + BAND, :] = l_tile


def _attn_group(q, kvb, bias, grp):
    n, d, rows, _ = q.shape
    qb = min(ATT_QB, rows)
    per_blk = qb // BAND
    res = min(d, ATT_QB // qb)
    stat_sds = jax.ShapeDtypeStruct((n, d, rows, LANES), F32)
    stat_spec = pl.BlockSpec((None, res, qb, LANES), lambda b, r, i: (b, r, i, 0))

    def prev_idx(i):
        return jnp.maximum(i * per_blk - 1, 0)

    def cur(col):
        return pl.BlockSpec((None, res, qb, HEADS_W), lambda b, r, i: (b, r, i, col))

    def prev(col):
        return pl.BlockSpec((None, res, BAND, HEADS_W), lambda b, r, i: (b, r, prev_idx(i), col))

    return pl.pallas_call(
        _attn_group_kernel,
        out_shape=(jax.ShapeDtypeStruct((n, d, rows, HEADS_W), F32), stat_sds, stat_sds),
        grid=(n, d // res, rows // qb),
        in_specs=[cur(0), cur(0), prev(0), cur(1), prev(1),
                  _const_spec((N_HEADS // 2, 2 * BAND, 2 * BAND))],
        out_specs=(cur(0), stat_spec, stat_spec),
        compiler_params=_params(("arbitrary", "arbitrary", "arbitrary"), 32),
        name=f"attn_group{grp}",
    )(q, kvb, kvb, kvb, kvb, bias)


def _compact_cache_kernel(c_ref, eye_ref, sel_ref, recent_ref, far_ref, *, t_new):
    p = c_ref.shape[-1]
    d1, d2 = WINDOWS[1][1], WINDOWS[2][1]
    x = c_ref[...].reshape(KV_W, p).astype(BF16)
    lo = p - d1 * BAND
    for j in range(d1 * BAND // MXU_DIM):
        cols = x[:, lo + j * MXU_DIM:lo + (j + 1) * MXU_DIM]
        recent_ref[j * MXU_DIM:(j + 1) * MXU_DIM, :] = _dot_t(eye_ref[...], cols).astype(BF16)
    per = MXU_DIM // d2
    for c in range(p // MXU_DIM):
        picked = _dot_t(sel_ref[...], x[:, c * MXU_DIM:(c + 1) * MXU_DIM]).astype(BF16)
        for t in range(t_new):
            far_ref[t * BAND + c * per:t * BAND + (c + 1) * per, :] = picked[t * per:(t + 1) * per, :]


def _compaction_job(cache_kv, t_new, mlp_f32):
    n, p = cache_kv.shape[:2]
    d1, d2 = WINDOWS[1][1], WINDOWS[2][1]
    per = MXU_DIM // d2
    assert p == WINDOWS[2][0] and p // d2 == BAND and t_new <= d1 and p % MXU_DIM == 0
    feat_major = jnp.transpose(cache_kv, (0, 2, 3, 4, 1))
    eye = jnp.eye(MXU_DIM, dtype=BF16)
    row = jnp.arange(t_new * per)[:, None]
    col = jnp.arange(MXU_DIM)[None, :]
    sel = jnp.where(col == (row % per) * d2 + row // per, 1.0, 0.0).astype(BF16)
    return dict(inputs=(feat_major, eye, sel), t_new=t_new, n=n, next=0, chunks=[],
                mlp_f32=tuple(mlp_f32), mlp_w={})


def _fused_body(*refs, body, n_in, n_out, t_new, n_cast):
    c_ref, eye_ref, sel_ref = refs[n_in:n_in + 3]
    cast_in = refs[n_in + 3:n_in + 3 + n_cast]
    rest = refs[n_in + 3 + n_cast:]
    recent_ref, far_ref = rest[n_out:n_out + 2]
    cast_out = rest[n_out + 2:n_out + 2 + n_cast]
    body(*refs[:n_in], *rest[:n_out], *rest[n_out + 2 + n_cast:])
    _compact_cache_kernel(c_ref, eye_ref, sel_ref, recent_ref, far_ref, t_new=t_new)
    for src, dst in zip(cast_in, cast_out):
        dst[...] = src[...].astype(BF16)


def _call_with_compaction(body, job, steps, step_of, *, grid, in_specs, out_specs, out_shape, args,
                          semantics, vmem_mib, name, scratch_shapes=(), cast_layer=None):
    if job is None:
        return pl.pallas_call(body, out_shape=out_shape, grid=grid, in_specs=in_specs, out_specs=out_specs,
                              scratch_shapes=scratch_shapes, compiler_params=_params(semantics, vmem_mib),
                              name=name)(*args)
    base, t_new = job["next"], job["t_new"]
    assert base + steps <= job["n"]
    job["next"] = base + steps
    feat, eye, sel = job["inputs"]
    n_out = len(out_shape)
    rows = (WINDOWS[1][1] * BAND, t_new * BAND)
    stacks = job["mlp_f32"] if cast_layer is not None else ()
    cast_in, cast_sds, cast_out = [], [], []
    for w in stacks:
        blk = (None, w.shape[1] // steps, w.shape[2])
        cast_in.append(pl.BlockSpec(blk, lambda *idx: (cast_layer, step_of(*idx), 0)))
        cast_sds.append(jax.ShapeDtypeStruct((1,) + w.shape[1:], BF16))
        cast_out.append(pl.BlockSpec(blk, lambda *idx: (0, step_of(*idx), 0)))
    res = pl.pallas_call(
        functools.partial(_fused_body, body=body, n_in=len(args), n_out=n_out, t_new=t_new,
                          n_cast=len(stacks)),
        out_shape=list(out_shape) + [jax.ShapeDtypeStruct((steps, r, KV_W), BF16) for r in rows] + cast_sds,
        grid=grid,
        in_specs=list(in_specs)
        + [pl.BlockSpec((None,) + feat.shape[1:], lambda *idx: (base + step_of(*idx), 0, 0, 0, 0)),
           _const_spec(eye.shape), _const_spec(sel.shape)] + cast_in,
        out_specs=list(out_specs)
        + [pl.BlockSpec((None, r, KV_W), lambda *idx: (step_of(*idx), 0, 0)) for r in rows] + cast_out,
        scratch_shapes=scratch_shapes,
        compiler_params=_params(semantics, vmem_mib + COMPACT_VMEM_MIB),
        name=name,
    )(*args, feat, eye, sel, *stacks)
    job["chunks"].append((base, *res[n_out:n_out + 2]))
    if stacks:
        job["mlp_w"][cast_layer] = tuple(res[n_out + 2:])
    return res[:n_out]


def _attn_sample_kernel(q_ref, kvn_ref, recent_ref, far_ref, b0_ref, b1_ref, b2_ref, b2n_ref, o_ref):
    t_new = q_ref.shape[0]
    n_seq = SAMPLE_SEQ_STEP
    base = (pl.program_id(0) % (SAMPLE_ROWS // n_seq)) * n_seq
    n_rows = t_new * N_HEADS
    hsel = (lax.broadcasted_iota(jnp.int32, (N_HEADS, HEADS_W), 1) // HEAD_DIM
            == lax.broadcasted_iota(jnp.int32, (N_HEADS, HEADS_W), 0))
    row8 = lax.broadcasted_iota(jnp.int32, (SUBLANES, KV_W), 0)

    qs, kas, vas, kbs, vbs = [], [], [], [], []
    for j in range(n_seq):
        b = base + j
        qs.append([jnp.concatenate(
            [jnp.where(hsel, q_ref[t, pl.ds(b, 1), _lane_block(grp, HEADS_W)], 0.0) for t in range(t_new)],
            axis=0).astype(BF16) for grp in range(N_GROUPS)])
        new8 = jnp.zeros((SUBLANES, KV_W), F32)
        for t in range(t_new):
            new8 = jnp.where(row8 == t, kvn_ref[t, pl.ds(b, 1), :], new8)
        new = jnp.concatenate([new8, jnp.zeros((BAND - SUBLANES, KV_W), F32)], axis=0).astype(BF16)
        recent = recent_ref[j]
        kas.append(jnp.concatenate([recent[:, :HEADS_W], new[:, :HEADS_W]], axis=0))
        vas.append(jnp.concatenate([recent[:, HEADS_W:], new[:, HEADS_W:]], axis=0))
        kbs.append(far_ref[j, :, :HEADS_W])
        vbs.append(far_ref[j, :, HEADS_W:])
    n_a = kas[0].shape[0]

    def scores(grp, keys, lo, bias_ref):
        return jnp.concatenate([_dot_t(qs[j][grp], keys[j][lo:]) for j in range(n_seq)],
                               axis=0) + bias_ref[...]

    def softmax_pv(pieces):
        m = functools.reduce(jnp.maximum, [jnp.max(s, axis=-1, keepdims=True) for s, _, _ in pieces])
        l = jnp.zeros((n_seq * n_rows, 1), F32)
        acc = jnp.zeros((n_seq * n_rows, HEADS_W), F32)
        for s, vals, lo in pieces:
            p = jnp.exp2(s - m)
            l = l + jnp.sum(p, axis=-1, keepdims=True)
            pb = p.astype(BF16)
            acc = acc + jnp.concatenate(
                [jnp.dot(pb[j * n_rows:(j + 1) * n_rows], vals[j][lo:], preferred_element_type=F32)
                 for j in range(n_seq)], axis=0)
        return acc, m, l

    lo0, lo_new = n_a - 2 * BAND, n_a - BAND
    parts = [
        softmax_pv([(scores(0, kas, lo0, b0_ref), vas, lo0)]),
        softmax_pv([(scores(1, kas, 0, b1_ref), vas, 0)]),
        softmax_pv([(scores(2, kbs, 0, b2_ref), vbs, 0), (scores(2, kas, lo_new, b2n_ref), vas, lo_new)]),
    ]
    m_all = functools.reduce(jnp.maximum, [m for _, m, _ in parts])
    num = jnp.zeros((n_seq * n_rows, HEADS_W), F32)
    den = jnp.zeros((n_seq * n_rows, 1), F32)
    for acc, m, l in parts:
        e = jnp.exp2(m - m_all)
        num = num + e * acc
        den = den + e * l
    out = num * (1.0 / den)
    for j in range(n_seq):
        for t in range(t_new):
            r0 = j * n_rows + t * N_HEADS
            own = jnp.where(hsel, out[r0:r0 + N_HEADS, :], 0.0)
            o_ref[t, pl.ds(base + j, 1), :] = jnp.sum(own, axis=0, keepdims=True)


def _attn_sample(q, kvn, chunks, biases):
    t_new, n, qw = q.shape
    step = SAMPLE_SEQ_STEP
    per_blk = SAMPLE_ROWS // step
    biases = [jnp.tile(b, (step, 1)) for b in biases]
    outs = []
    for first, recent, far in chunks:
        count = recent.shape[0]
        assert first % SAMPLE_ROWS == 0 and count % SAMPLE_ROWS == 0
        blk0 = first // SAMPLE_ROWS
        blk = lambda width: pl.BlockSpec((t_new, SAMPLE_ROWS, width), lambda i: (0, blk0 + i // per_blk, 0))
        rows = lambda a: pl.BlockSpec((step,) + a.shape[1:], lambda i: (i, 0, 0))
        outs.append(pl.pallas_call(
            _attn_sample_kernel,
            out_shape=jax.ShapeDtypeStruct((t_new, count, HEADS_W), F32),
            grid=(count // step,),
            in_specs=[blk(qw), blk(KV_W), rows(recent), rows(far)] + [_const_spec(b.shape) for b in biases],
            out_specs=pl.BlockSpec((t_new, SAMPLE_ROWS, HEADS_W), lambda i: (0, i // per_blk, 0)),
            compiler_params=_params(("arbitrary",), 40),
            name="attn_sample",
        )(q, kvn, recent, far, *biases))
    return jnp.concatenate(outs, axis=1)


def _attn_out_sample_kernel(x_ref, o_ref, wo_ref, g_ref, wup_ref, wdn_ref, out_ref):
    x = x_ref[...] + jnp.dot(o_ref[...].astype(BF16), wo_ref[...], preferred_element_type=F32)
    out_ref[...] = _mlp_residual(x, g_ref, wup_ref, wdn_ref)


def _attn_out_prompt_kernel(x_ref, a0_ref, m0_ref, l0_ref, a1_ref, m1_ref, l1_ref, a2_ref, m2_ref, l2_ref,
                            ex_ref, wo_ref, g_ref, wup_ref, wdn_ref, out_ref, slab_ref):
    rows = x_ref.shape[0]
    n_o = HEADS_W // LANES
    for gi, (a_ref, m_ref, l_ref) in enumerate(((a1_ref, m1_ref, l1_ref), (a2_ref, m2_ref, l2_ref))):
        d = WINDOWS[gi + 1][1]
        for r in range(d):
            dst = pl.ds(r, rows // d, stride=d)
            for s in range(n_o):
                slab_ref[gi, s, dst, :] = a_ref[r, :, _lane_block(s)]
            slab_ref[gi, n_o, dst, :] = m_ref[r]
            slab_ref[gi, n_o + 1, dst, :] = l_ref[r]
    accs = [a0_ref[0]] + [jnp.concatenate([slab_ref[gi, s] for s in range(n_o)], axis=1) for gi in range(2)]
    ms = [m0_ref[0], slab_ref[0, n_o], slab_ref[1, n_o]]
    ls = [l0_ref[0], slab_ref[0, n_o + 1], slab_ref[1, n_o + 1]]
    mx = functools.reduce(jnp.maximum, ms)
    es = [jnp.exp2(m - mx) for m in ms]
    inv = 1.0 / functools.reduce(jnp.add, [e * l for e, l in zip(es, ls)])
    o = jnp.zeros((rows, HEADS_W), F32)
    for e, og in zip(es, accs):
        w = e * inv
        hi = w.astype(BF16)
        lo = (w - hi.astype(F32)).astype(BF16)
        wide = (jnp.dot(hi, ex_ref[...], preferred_element_type=F32)
                + jnp.dot(lo, ex_ref[...], preferred_element_type=F32))
        o = o + wide * og
    x = x_ref[...] + jnp.dot(o.astype(BF16), wo_ref[...], preferred_element_type=F32)
    out_ref[...] = _mlp_residual(x, g_ref, wup_ref, wdn_ref)


def _attn_out_q_prompt_kernel(*refs):
    n_in = 15
    gm_ref, wq_ref, e_ref, gq_ref = refs[n_in:n_in + 4]
    out_ref, q0_ref, q1_ref, q2_ref, slab_ref, qslab_ref = refs[n_in + 4:]
    _attn_out_prompt_kernel(*refs[:n_in], out_ref, slab_ref)
    h = _rms(out_ref[...], gm_ref[...]).astype(BF16)
    _q_prompt_store(h, wq_ref, e_ref, gq_ref, q0_ref, q1_ref, q2_ref, qslab_ref)


def _mlp_weight_specs(layer, j, wo, g, wup, wdn):
    return [_layer_spec(wo, j), _layer_spec(g, layer), _layer_spec(wup, 0), _layer_spec(wdn, 0)]


def _attn_out_sample(x, o, layer, j, wo, g, wup, wdn):
    m = x.shape[0]
    row = pl.BlockSpec((ROW_TILE, D_MODEL), lambda i: (i, 0))
    return pl.pallas_call(
        _attn_out_sample_kernel,
        out_shape=jax.ShapeDtypeStruct((m, D_MODEL), F32),
        grid=(m // ROW_TILE,),
        in_specs=[row, pl.BlockSpec((ROW_TILE, HEADS_W), lambda i: (i, 0))]
        + _mlp_weight_specs(layer, j, wo, g, wup, wdn),
        out_specs=row,
        compiler_params=_params(("arbitrary",), 48),
        name="attn_out_sample",
    )(x, o, wo, g, wup, wdn)


def _attn_out_prompt(x, parts, ex, layer, j, wo, g, wup, wdn, next_q=None):
    n, t, _ = x.shape
    tile = ATT_OUT_TILE
    assert tile == ROW_TILE
    row = pl.BlockSpec((None, tile, D_MODEL), lambda b, i: (b, i, 0))
    part_specs = []
    for _, d in WINDOWS:
        for width in (HEADS_W, LANES, LANES):
            part_specs.append(pl.BlockSpec((None, d, tile // d, width), lambda b, i: (b, 0, i, 0)))
    flat = [a for part in parts for a in part]
    in_specs = ([row] + part_specs + [_const_spec((LANES, HEADS_W))]
                + _mlp_weight_specs(layer, j, wo, g, wup, wdn))
    args = [x, *flat, ex, wo, g, wup, wdn]
    out_shape, out_specs = [jax.ShapeDtypeStruct((n, t, D_MODEL), F32)], [row]
    scratch = [pltpu.VMEM((2, HEADS_W // LANES + 2, tile, LANES), F32)]
    body = _attn_out_prompt_kernel
    if next_q is not None:
        g_mix, wq, e, gq = next_q
        in_specs += [_layer_spec(g_mix, layer + 1), _layer_spec(wq, j + 1), _const_spec(e.shape),
                     _layer_spec(gq, j + 1)]
        args += [g_mix, wq, e, gq]
        q_sds, q_specs = _dilated_shapes(n, t, HEADS_W, BF16)
        out_shape, out_specs = out_shape + q_sds, out_specs + q_specs
        scratch.append(pltpu.VMEM((HEADS_W // LANES, tile, LANES), F32))
        body = _attn_out_q_prompt_kernel
    res = pl.pallas_call(
        body,
        out_shape=out_shape,
        grid=(n, t // tile),
        in_specs=in_specs,
        out_specs=out_specs,
        scratch_shapes=scratch,
        compiler_params=_params(("arbitrary", "arbitrary"), 56),
        name="attn_out_prompt",
    )(*args)
    return res[0], res[1:]


def _alibi_slopes():
    n = N_GROUPS * N_HEADS
    s = 2.0 ** (-8.0 * jnp.arange(1, n + 1, dtype=F32) / n)
    return s.reshape(N_GROUPS, N_HEADS)


def _prompt_bias(slopes, grp):
    d = WINDOWS[grp][1]
    iq = jnp.arange(BAND, dtype=jnp.int32)[:, None]
    ik = jnp.arange(2 * BAND, dtype=jnp.int32)[None, :]
    steps = iq + BAND - ik
    valid = (steps >= 0) & (steps <= BAND)
    offs = (steps * d).astype(F32)
    per_head = jnp.where(valid[None], -slopes[grp][:, None, None] * offs[None] * LOG2_E, NEG_INF)
    return per_head.reshape(N_HEADS // 2, 2 * BAND, 2 * BAND)


def _sample_bias(slopes, t_new, past):
    d1, d2 = WINDOWS[1][1], WINDOWS[2][1]
    t = jnp.arange(t_new, dtype=jnp.int32)[:, None]

    def table(slope, dist, valid):
        b = jnp.where(valid[:, None, :], -slope[None, :, None] * dist.astype(F32)[:, None, :] * LOG2_E,
                      NEG_INF)
        return b.reshape(t_new * N_HEADS, -1)

    u = jnp.arange(BAND, dtype=jnp.int32)[None, :]
    dist_n = t - u
    ok_n = (u < t_new) & (dist_n >= 0)
    c = jnp.arange(BAND, dtype=jnp.int32)[None, :]
    dist_c = BAND + t - c
    b0 = table(slopes[0], jnp.concatenate([dist_c, dist_n], axis=1),
               jnp.concatenate([dist_c <= WINDOWS[0][0], ok_n], axis=1))
    c = jnp.arange(d1 * BAND, dtype=jnp.int32)[None, :]
    dist_c = d1 * BAND + t - c
    ok_c = (dist_c % d1 == 0) & (dist_c <= WINDOWS[1][0])
    b1 = table(slopes[1], jnp.concatenate([dist_c, dist_n], axis=1),
               jnp.concatenate([ok_c, ok_n & (dist_n % d1 == 0)], axis=1))
    per = past // d2
    col = jnp.arange(t_new * per, dtype=jnp.int32)[None, :]
    pos = (col % per) * d2 + col // per
    dist_c = past + t - pos
    ok_c = (dist_c % d2 == 0) & (dist_c <= WINDOWS[2][0])
    b2 = table(slopes[2], dist_c, ok_c)
    b2n = table(slopes[2], dist_n, ok_n & (dist_n % d2 == 0))
    return b0, b1, b2, b2n


def _head_mean_matrix():
    idx = jnp.arange(HEADS_W) // HEAD_DIM
    return jnp.where(idx[:, None] == idx[None, :], 1.0 / HEAD_DIM, 0.0).astype(BF16)


def _head_expand_matrix():
    lane = jnp.arange(LANES)[:, None]
    col = jnp.arange(HEADS_W)[None, :] // HEAD_DIM
    return jnp.where(lane == col, 1.0, 0.0).astype(BF16)


def kernel(x_prompt, x_sample, state_conv, cache_kv, norm_mix_g, norm_mlp_g, conv_w_pw1, conv_b_pw1,
           conv_w_dw, conv_b_dw, conv_ln_g, conv_ln_b, conv_w_pw2, conv_b_pw2, kv_norm_g, w_kv,
           k_norm_g, attn_w_q, q_norm_g, attn_w_o, mlp_w_up, mlp_w_down):
    n_p, t_p, _ = x_prompt.shape
    n_s, t_s, _ = x_sample.shape
    past = cache_kv.shape[1]
    scale = HEAD_DIM ** -0.5 * LOG2_E

    w1 = conv_w_pw1.astype(BF16)
    w2 = conv_w_pw2.astype(BF16)
    wkv = w_kv.astype(BF16)
    wq = attn_w_q.astype(BF16)
    wo = attn_w_o.astype(BF16)
    vec = lambda a: a.reshape(-1, 1, a.shape[-1])
    g_mix, g_mlp, g_kv = vec(norm_mix_g), vec(norm_mlp_g), vec(kv_norm_g)
    e_mat = _head_mean_matrix()
    ex_mat = _head_expand_matrix()
    gk = vec(jnp.tile(k_norm_g, N_HEADS))
    gq = vec(jnp.tile(q_norm_g, (1, N_HEADS)) * scale)
    wkv = wkv[None]
    slopes = _alibi_slopes()
    bias_p = [_prompt_bias(slopes, g) for g in range(N_GROUPS)]
    bias_s = _sample_bias(slopes, t_s, past)
    conv_weights = (g_mix, w1, vec(conv_b_pw1), conv_w_dw, vec(conv_b_dw), vec(conv_ln_g),
                    vec(conv_ln_b), w2, vec(conv_b_pw2))

    job = _compaction_job(cache_kv, t_s, (mlp_w_up, mlp_w_down))
    mlp_w = job["mlp_w"]
    xp = x_prompt
    conv_p = []
    zero_hist = jnp.zeros((n_p, HIST_PAD, CONV_CH), F32)
    for layer in range(N_A_LAYERS):
        xp, st = _conv_prompt(xp, zero_hist, layer, conv_weights, job, cast_layer=layer)
        conv_p.append(st[:, HIST_PAD - CONV_HIST:])
        xp = _mlp(xp.reshape(n_p * t_p, D_MODEL), layer, g_mlp, *mlp_w[layer], job,
                  cast_layer=N_A_LAYERS + layer).reshape(n_p, t_p, D_MODEL)
    assert job["next"] == job["n"] and len(mlp_w) == N_A_LAYERS + N_B_LAYERS
    win = min(WINDOWS[-1][0], t_p)
    kv_p, kvb_p, qs = _kvq_proj_prompt(xp, g_kv, wkv, e_mat, gk, win, N_A_LAYERS, g_mix, wq, gq)
    for j in range(N_B_LAYERS):
        layer = N_A_LAYERS + j
        parts = [_attn_group(qs[g], kvb_p[g], bias_p[g], g) for g in range(N_GROUPS)]
        next_q = (g_mix, wq, e_mat, gq) if j + 1 < N_B_LAYERS else None
        xp, qs = _attn_out_prompt(xp, parts, ex_mat, layer, j, wo, g_mlp, *mlp_w[layer], next_q)
    y_prompt = xp
    conv_prompt = jnp.stack(conv_p, axis=0)
    kv_prompt = kv_p.reshape(n_p, win, 2, N_HEADS, HEAD_DIM)

    xs = jnp.transpose(x_sample, (1, 0, 2))
    st_in = jnp.transpose(state_conv, (0, 2, 1, 3))
    new_states = []
    for layer in range(N_A_LAYERS):
        prev = new_states if layer == N_A_LAYERS - 1 else ()
        xs, st = _conv_sample(xs, st_in, layer, conv_weights, prev)
        new_states.append(st)
        xs = _mlp(xs.reshape(t_s * n_s, D_MODEL), layer, g_mlp, *mlp_w[layer]).reshape(t_s, n_s, D_MODEL)
    xs = xs.reshape(t_s * n_s, D_MODEL)
    kvn = _kv_proj_sample(xs, g_kv, wkv, e_mat, gk).reshape(t_s, n_s, KV_W)
    for j in range(N_B_LAYERS):
        layer = N_A_LAYERS + j
        q = _q_proj_sample(xs, layer, j, g_mix, wq, e_mat, gq)
        o = _attn_sample(q.reshape(t_s, n_s, N_GROUPS * HEADS_W), kvn, job["chunks"], bias_s)
        xs = _attn_out_sample(xs, o.reshape(t_s * n_s, HEADS_W), layer, j, wo, g_mlp, *mlp_w[layer])
    y_sample = jnp.transpose(xs.reshape(t_s, n_s, D_MODEL), (1, 0, 2))
    st_stack = new_states[-1] if N_A_LAYERS > 1 else new_states[-1][None]
    conv_sample = jnp.transpose(st_stack, (0, 2, 1, 3))
    kv_sample = jnp.transpose(kvn, (1, 0, 2)).reshape(n_s, t_s, 2, N_HEADS, HEAD_DIM)

    return (y_prompt, y_sample, conv_prompt, conv_sample, kv_prompt, kv_sample)
```

```python
import functools

import jax
import jax.numpy as jnp
from jax import lax
from jax.experimental import pallas as pl
from jax.experimental.pallas import tpu as pltpu

F32 = jnp.float32
BF16 = jnp.bfloat16

D_MODEL = 1024
CONV_CH = 1024
CONV_WIDTH = 31
CONV_HIST = CONV_WIDTH - 1
HIST_PAD = 32
D_FF = 4096
N_HEADS = 8
HEAD_DIM = 64
HEADS_W = N_HEADS * HEAD_DIM
KV_W = 2 * HEADS_W
WINDOWS = ((128, 1), (512, 4), (2048, 16))
N_GROUPS = len(WINDOWS)
BAND = 128
N_A_LAYERS = 2
N_B_LAYERS = 2
NORM_EPS = 1e-6
NEG_INF = float("-inf")
LOG2_E = 1.4426950408889634
LANES = 128
SUBLANES = 8

ROW_TILE = 512
ATT_OUT_TILE = 512
FF_CHUNK = 1024
ATT_QB = 2048
SAMPLE_CONV_NB = 32
SAMPLE_ROWS = 8
SAMPLE_SEQ_STEP = 4
MXU_DIM = 256
COMPACT_VMEM_MIB = 24
MIB = 1024 * 1024

assert all(w // d == BAND for w, d in WINDOWS)


def _const_spec(shape):
    nd = len(shape)
    return pl.BlockSpec(shape, lambda *_: (0,) * nd, pipeline_mode=pl.Buffered(1))


def _layer_spec(stacked, layer):
    return pl.BlockSpec((None,) + stacked.shape[1:], lambda *_: (layer, 0, 0),
                        pipeline_mode=pl.Buffered(1))


def _params(semantics, vmem_mib):
    return pltpu.CompilerParams(dimension_semantics=semantics, vmem_limit_bytes=vmem_mib * MIB)


def _lane_block(s, width=LANES):
    return slice(s * width, (s + 1) * width)


def _rms(x, g):
    return x * lax.rsqrt(jnp.mean(x * x, axis=-1, keepdims=True) + NORM_EPS) * g


def _mlp_residual(x, g_ref, wup_ref, wdn_ref):
    h = _rms(x, g_ref[...]).astype(BF16)
    acc = x
    for c in range(D_FF // FF_CHUNK):
        sl = _lane_block(c, FF_CHUNK)
        z = jnp.maximum(jnp.dot(h, wup_ref[:, sl], preferred_element_type=F32), 0.0)
        acc = acc + jnp.dot((z * z).astype(BF16), wdn_ref[sl, :], preferred_element_type=F32)
    return acc


def _head_mean_sq(v, e_ref):
    return jnp.dot((v * v).astype(BF16), e_ref[...], preferred_element_type=F32)


def _dot_t(a, b):
    return lax.dot_general(a, b, (((1,), (1,)), ((), ())), preferred_element_type=F32)


def _deinterleave_store(val, slab_ref, out_ref, d):
    rows, width = val.shape
    for s in range(width // LANES):
        slab_ref[s] = val[:, _lane_block(s)]
    for s in range(width // LANES):
        for r in range(d):
            out_ref[r, :, _lane_block(s)] = slab_ref[s, pl.ds(r, rows // d, stride=d), :].astype(BF16)


def _mlp_kernel(x_ref, g_ref, wup_ref, wdn_ref, o_ref):
    o_ref[...] = _mlp_residual(x_ref[...], g_ref, wup_ref, wdn_ref)


def _mlp(x, layer, g, wup, wdn, job=None, cast_layer=None):
    m = x.shape[0]
    row = pl.BlockSpec((ROW_TILE, D_MODEL), lambda i: (i, 0))
    return _call_with_compaction(
        _mlp_kernel, job, m // ROW_TILE, lambda i: i,
        out_shape=[jax.ShapeDtypeStruct((m, D_MODEL), F32)],
        grid=(m // ROW_TILE,),
        in_specs=[row, _layer_spec(g, layer), _layer_spec(wup, 0), _layer_spec(wdn, 0)],
        out_specs=[row],
        args=(x, g, wup, wdn),
        semantics=("arbitrary",), vmem_mib=34, name="mlp", cast_layer=cast_layer,
    )[0]


def _glu(h, w1_ref, b1_ref):
    u = jnp.dot(h, w1_ref[...], preferred_element_type=F32) + b1_ref[...]
    return u[:, :CONV_CH] * jax.nn.sigmoid(u[:, CONV_CH:])


def _conv_tail(c, x, bdw_ref, lng_ref, lnb_ref, w2_ref, b2_ref):
    c = c + bdw_ref[...]
    mu = jnp.mean(c, axis=-1, keepdims=True)
    xc = c - mu
    y = xc * lax.rsqrt(jnp.mean(xc * xc, axis=-1, keepdims=True) + NORM_EPS)
    y = y * lng_ref[...] + lnb_ref[...]
    y = y * jax.nn.sigmoid(y)
    out = jnp.dot(y.astype(BF16), w2_ref[...], preferred_element_type=F32) + b2_ref[...]
    return x + out


def _conv_prompt_kernel(x_ref, st_ref, g_ref, w1_ref, b1_ref, wdw_ref, bdw_ref, lng_ref, lnb_ref,
                        w2_ref, b2_ref, o_ref, sto_ref, buf_ref):
    n_slabs = CONV_CH // LANES

    @pl.when(pl.program_id(1) == 0)
    def _():
        for s in range(n_slabs):
            buf_ref[s, 0:HIST_PAD, :] = st_ref[:, _lane_block(s)]

    x = x_ref[...]
    h = _rms(x, g_ref[...]).astype(BF16)
    u = _glu(h, w1_ref, b1_ref)
    for s in range(n_slabs):
        buf_ref[s, HIST_PAD:HIST_PAD + ROW_TILE, :] = u[:, _lane_block(s)]
    off = HIST_PAD - CONV_HIST
    cs = []
    for s in range(n_slabs):
        ls = _lane_block(s)
        c = buf_ref[s, off:off + ROW_TILE, :] * wdw_ref[0:1, ls]
        for k in range(1, CONV_WIDTH):
            c = c + buf_ref[s, off + k:off + k + ROW_TILE, :] * wdw_ref[k:k + 1, ls]
        cs.append(c)
    c = jnp.concatenate(cs, axis=1)
    o_ref[...] = _conv_tail(c, x, bdw_ref, lng_ref, lnb_ref, w2_ref, b2_ref)
    for s in range(n_slabs):
        last = buf_ref[s, ROW_TILE:ROW_TILE + HIST_PAD, :]
        sto_ref[:, _lane_block(s)] = last
        buf_ref[s, 0:HIST_PAD, :] = last


def _conv_prompt(x, st, layer, weights, job=None, cast_layer=None):
    n, t, _ = x.shape
    tiles = t // ROW_TILE
    row = pl.BlockSpec((None, ROW_TILE, D_MODEL), lambda b, i: (b, i, 0))
    st_spec = pl.BlockSpec((None, HIST_PAD, CONV_CH), lambda b, i: (b, 0, 0))
    return _call_with_compaction(
        _conv_prompt_kernel, job, n * tiles, lambda b, i: b * tiles + i,
        out_shape=[jax.ShapeDtypeStruct((n, t, D_MODEL), F32),
                   jax.ShapeDtypeStruct((n, HIST_PAD, CONV_CH), F32)],
        grid=(n, tiles),
        in_specs=[row, st_spec] + [_layer_spec(a, layer) for a in weights],
        out_specs=[row, st_spec],
        scratch_shapes=[pltpu.VMEM((CONV_CH // LANES, HIST_PAD + ROW_TILE, LANES), F32)],
        args=(x, st, *weights),
        semantics=("arbitrary", "arbitrary"), vmem_mib=32, name="conv_prompt", cast_layer=cast_layer,
    )


def _conv_sample_kernel(x_ref, st_ref, g_ref, w1_ref, b1_ref, wdw_ref, bdw_ref, lng_ref, lnb_ref,
                        w2_ref, b2_ref, *rest):
    *prev_refs, o_ref, sto_ref = rest
    t_new, nb, _ = x_ref.shape
    x = x_ref[...].reshape(t_new * nb, D_MODEL)
    h = _rms(x, g_ref[...]).astype(BF16)
    glu = _glu(h, w1_ref, b1_ref)
    new = [glu[t * nb:(t + 1) * nb, :] for t in range(t_new)]

    def full(j):
        return st_ref[j] if j < CONV_HIST else new[j - CONV_HIST]

    outs = []
    for t in range(t_new):
        c = full(t) * wdw_ref[0:1, :]
        for k in range(1, CONV_WIDTH):
            c = c + full(t + k) * wdw_ref[k:k + 1, :]
        outs.append(c)
    c = jnp.concatenate(outs, axis=0)
    o_ref[...] = _conv_tail(c, x, bdw_ref, lng_ref, lnb_ref, w2_ref, b2_ref).reshape(t_new, nb, D_MODEL)
    if prev_refs:
        for l, prev_ref in enumerate(prev_refs):
            sto_ref[l] = prev_ref[...]
        for j in range(CONV_HIST):
            sto_ref[len(prev_refs), j] = full(j + t_new)
    else:
        for j in range(CONV_HIST):
            sto_ref[j] = full(j + t_new)


def _conv_sample(x, st_all, layer, weights, prev_states=()):
    t_new, n, _ = x.shape
    nb = SAMPLE_CONV_NB
    x_spec = pl.BlockSpec((t_new, nb, D_MODEL), lambda i: (0, i, 0))
    st_spec = pl.BlockSpec((None, CONV_HIST, nb, CONV_CH), lambda i: (layer, 0, i, 0))
    one_state = pl.BlockSpec((CONV_HIST, nb, CONV_CH), lambda i: (0, i, 0))
    if prev_states:
        n_stack = len(prev_states) + 1
        st_sds = jax.ShapeDtypeStruct((n_stack, CONV_HIST, n, CONV_CH), F32)
        sto_spec = pl.BlockSpec((n_stack, CONV_HIST, nb, CONV_CH), lambda i: (0, 0, i, 0))
    else:
        st_sds = jax.ShapeDtypeStruct((CONV_HIST, n, CONV_CH), F32)
        sto_spec = one_state
    return pl.pallas_call(
        _conv_sample_kernel,
        out_shape=(jax.ShapeDtypeStruct((t_new, n, D_MODEL), F32), st_sds),
        grid=(n // nb,),
        in_specs=[x_spec, st_spec] + [_layer_spec(a, layer) for a in weights] + [one_state] * len(prev_states),
        out_specs=(x_spec, sto_spec),
        compiler_params=_params(("arbitrary",), 48),
        name="conv_sample",
    )(x, st_all, *weights, *prev_states)


def _kv_rows(x_ref, g_ref, wkv_ref, e_ref, gk_ref):
    h = _rms(x_ref[...], g_ref[...]).astype(BF16)
    k = jnp.dot(h, wkv_ref[:, :HEADS_W], preferred_element_type=F32)
    v = jnp.dot(h, wkv_ref[:, HEADS_W:], preferred_element_type=F32)
    k = k * lax.rsqrt(_head_mean_sq(k, e_ref) + NORM_EPS) * gk_ref[...]
    return jnp.concatenate([k, v], axis=1)


def _kv_sample_kernel(x_ref, g_ref, wkv_ref, e_ref, gk_ref, kv_ref):
    kv_ref[...] = _kv_rows(x_ref, g_ref, wkv_ref, e_ref, gk_ref)


def _kvq_prompt_kernel(x_ref, g_ref, wkv_ref, e_ref, gk_ref, gm_ref, wq_ref, gq_ref,
                       kv_ref, kb0_ref, kb1_ref, kb2_ref, q0_ref, q1_ref, q2_ref, slab_ref):
    kv = _kv_rows(x_ref, g_ref, wkv_ref, e_ref, gk_ref)
    kv_ref[...] = kv
    kb0_ref[0] = kv.astype(BF16)
    _deinterleave_store(kv, slab_ref, kb1_ref, WINDOWS[1][1])
    _deinterleave_store(kv, slab_ref, kb2_ref, WINDOWS[2][1])
    h = _rms(x_ref[...], gm_ref[...]).astype(BF16)
    _q_prompt_store(h, wq_ref, e_ref, gq_ref, q0_ref, q1_ref, q2_ref, slab_ref)


def _proj_weight_specs(layer, g, w, e, gh):
    return [_layer_spec(g, layer), _layer_spec(w, layer), _const_spec(e.shape), _layer_spec(gh, layer)]


def _kv_proj_sample(x, g, wkv, e, gk):
    m = x.shape[0]
    return pl.pallas_call(
        _kv_sample_kernel,
        out_shape=jax.ShapeDtypeStruct((m, KV_W), F32),
        grid=(m // ROW_TILE,),
        in_specs=[pl.BlockSpec((ROW_TILE, D_MODEL), lambda i: (i, 0))] + _proj_weight_specs(0, g, wkv, e, gk),
        out_specs=pl.BlockSpec((ROW_TILE, KV_W), lambda i: (i, 0)),
        compiler_params=_params(("arbitrary",), 32),
        name="kv_proj_sample",
    )(x, g, wkv, e, gk)


def _dilated_shapes(n, t, width, dtype):
    sds, specs = [], []
    for _, d in WINDOWS:
        sds.append(jax.ShapeDtypeStruct((n, d, t // d, width), dtype))
        specs.append(pl.BlockSpec((None, d, ROW_TILE // d, width), lambda b, i: (b, 0, i, 0)))
    return sds, specs


def _kvq_proj_prompt(x, g, wkv, e, gk, win, layer, g_mix, wq, gq):
    n, t, _ = x.shape
    kv_sds, kv_specs = _dilated_shapes(n, t, KV_W, BF16)
    q_sds, q_specs = _dilated_shapes(n, t, HEADS_W, BF16)
    skip = (t - win) // ROW_TILE
    assert win % ROW_TILE == 0 and t % ROW_TILE == 0
    f32_spec = pl.BlockSpec((None, ROW_TILE, KV_W), lambda b, i: (b, jnp.maximum(i - skip, 0), 0))
    res = pl.pallas_call(
        _kvq_prompt_kernel,
        out_shape=[jax.ShapeDtypeStruct((n, win, KV_W), F32)] + kv_sds + q_sds,
        grid=(n, t // ROW_TILE),
        in_specs=[pl.BlockSpec((None, ROW_TILE, D_MODEL), lambda b, i: (b, i, 0))]
        + _proj_weight_specs(0, g, wkv, e, gk)
        + [_layer_spec(g_mix, layer), _layer_spec(wq, 0), _layer_spec(gq, 0)],
        out_specs=[f32_spec] + kv_specs + q_specs,
        scratch_shapes=[pltpu.VMEM((KV_W // LANES, ROW_TILE, LANES), F32)],
        compiler_params=_params(("arbitrary", "arbitrary"), 48),
        name="kvq_proj_prompt",
    )(x, g, wkv, e, gk, g_mix, wq, gq)
    return res[0], res[1:4], res[4:7]


def _q_group(h, wq_ref, e_ref, gq_ref, grp):
    q = jnp.dot(h, wq_ref[:, _lane_block(grp, HEADS_W)], preferred_element_type=F32)
    return q * lax.rsqrt(_head_mean_sq(q, e_ref) + NORM_EPS) * gq_ref[...]


def _q_sample_kernel(x_ref, g_ref, wq_ref, e_ref, gq_ref, q_ref):
    h = _rms(x_ref[...], g_ref[...]).astype(BF16)
    for grp in range(N_GROUPS):
        q = _q_group(h, wq_ref, e_ref, gq_ref, grp)
        q_ref[:, _lane_block(grp, HEADS_W)] = q.astype(BF16).astype(F32)


def _q_prompt_store(h, wq_ref, e_ref, gq_ref, q0_ref, q1_ref, q2_ref, slab_ref):
    q0_ref[0] = _q_group(h, wq_ref, e_ref, gq_ref, 0).astype(BF16)
    _deinterleave_store(_q_group(h, wq_ref, e_ref, gq_ref, 1), slab_ref, q1_ref, WINDOWS[1][1])
    _deinterleave_store(_q_group(h, wq_ref, e_ref, gq_ref, 2), slab_ref, q2_ref, WINDOWS[2][1])


def _q_proj_sample(x, layer, j, g, wq, e, gq):
    m = x.shape[0]
    qw = N_GROUPS * HEADS_W
    specs = [_layer_spec(g, layer), _layer_spec(wq, j), _const_spec(e.shape), _layer_spec(gq, j)]
    return pl.pallas_call(
        _q_sample_kernel,
        out_shape=jax.ShapeDtypeStruct((m, qw), F32),
        grid=(m // ROW_TILE,),
        in_specs=[pl.BlockSpec((ROW_TILE, D_MODEL), lambda i: (i, 0))] + specs,
        out_specs=pl.BlockSpec((ROW_TILE, qw), lambda i: (i, 0)),
        compiler_params=_params(("arbitrary",), 32),
        name="q_proj_sample",
    )(x, g, wq, e, gq)


def _attn_group_kernel(q_ref, kc_ref, kp_ref, vc_ref, vp_ref, bias_ref, acc_ref, m_ref, l_ref):
    for rr in range(q_ref.shape[0]):
        _attn_residue(q_ref.at[rr], kc_ref.at[rr], kp_ref.at[rr], vc_ref.at[rr], vp_ref.at[rr], bias_ref,
                      acc_ref.at[rr], m_ref.at[rr], l_ref.at[rr])


def _attn_residue(q_ref, kc_ref, kp_ref, vc_ref, vp_ref, bias_ref, acc_ref, m_ref, l_ref):
    first_block = pl.program_id(2) == 0
    key_col = lax.broadcasted_iota(jnp.int32, (2 * BAND, 2 * BAND), 1)
    lane = lax.broadcasted_iota(jnp.int32, (BAND, LANES), 1)
    left = lane < HEAD_DIM
    lane_row = lax.broadcasted_iota(jnp.int32, (1, LANES), 1)
    head_keep = [jnp.where(lane_row < HEAD_DIM, 1.0, 0.0).astype(BF16),
                 jnp.where(lane_row < HEAD_DIM, 0.0, 1.0).astype(BF16)]
    no_history = jnp.logical_and(first_block, key_col < BAND)

    for sb in range(q_ref.shape[0] // BAND):
        r0 = sb * BAND
        m_tile = jnp.zeros((BAND, LANES), F32)
        l_tile = jnp.ones((BAND, LANES), F32)
        for pair in range(N_HEADS // 2):
            ls = _lane_block(pair)
            q2 = q_ref[r0:r0 + BAND, ls]
            if sb == 0:
                k2 = jnp.concatenate([kp_ref[:, ls], kc_ref[0:BAND, ls]], axis=0)
                v2 = jnp.concatenate([vp_ref[:, ls], vc_ref[0:BAND, ls]], axis=0)
            else:
                k2 = kc_ref[r0 - BAND:r0 + BAND, ls]
                v2 = vc_ref[r0 - BAND:r0 + BAND, ls]
            qq = jnp.concatenate([q2 * head_keep[0], q2 * head_keep[1]], axis=0)
            s = _dot_t(qq, k2) + bias_ref[pair]
            if sb == 0:
                s = jnp.where(no_history, NEG_INF, s)
            m = jnp.max(s, axis=-1, keepdims=True)
            p = jnp.exp2(s - m)
            l = jnp.sum(p, axis=-1, keepdims=True)
            a = jnp.dot(p.astype(BF16), v2, preferred_element_type=F32)
            acc_ref[r0:r0 + BAND, ls] = jnp.where(left, a[:BAND], a[BAND:])
            for hh in range(2):
                rows = slice(hh * BAND, (hh + 1) * BAND)
                m_tile = jnp.where(lane == 2 * pair + hh, m[rows], m_tile)
                l_tile = jnp.where(lane == 2 * pair + hh, l[rows], l_tile)
        m_ref[r0:r0 + BAND, :] = m_tile
        l_ref[r0:r0 + BAND, :] = l_tile


def _attn_group(q, kvb, bias, grp):
    n, d, rows, _ = q.shape
    qb = min(ATT_QB, rows)
    per_blk = qb // BAND
    res = min(d, ATT_QB // qb)
    stat_sds = jax.ShapeDtypeStruct((n, d, rows, LANES), F32)
    stat_spec = pl.BlockSpec((None, res, qb, LANES), lambda b, r, i: (b, r, i, 0))

    def prev_idx(i):
        return jnp.maximum(i * per_blk - 1, 0)

    def cur(col):
        return pl.BlockSpec((None, res, qb, HEADS_W), lambda b, r, i: (b, r, i, col))

    def prev(col):
        return pl.BlockSpec((None, res, BAND, HEADS_W), lambda b, r, i: (b, r, prev_idx(i), col))

    return pl.pallas_call(
        _attn_group_kernel,
        out_shape=(jax.ShapeDtypeStruct((n, d, rows, HEADS_W), F32), stat_sds, stat_sds),
        grid=(n, d // res, rows // qb),
        in_specs=[cur(0), cur(0), prev(0), cur(1), prev(1),
                  _const_spec((N_HEADS // 2, 2 * BAND, 2 * BAND))],
        out_specs=(cur(0), stat_spec, stat_spec),
        compiler_params=_params(("arbitrary", "arbitrary", "arbitrary"), 40),
        name=f"attn_group{grp}",
    )(q, kvb, kvb, kvb, kvb, bias)


def _compact_cache_kernel(c_ref, eye_ref, sel_ref, recent_ref, far_ref, *, t_new):
    p = c_ref.shape[-1]
    d1, d2 = WINDOWS[1][1], WINDOWS[2][1]
    x = c_ref[...].reshape(KV_W, p).astype(BF16)
    lo = p - d1 * BAND
    for j in range(d1 * BAND // MXU_DIM):
        cols = x[:, lo + j * MXU_DIM:lo + (j + 1) * MXU_DIM]
        recent_ref[j * MXU_DIM:(j + 1) * MXU_DIM, :] = _dot_t(eye_ref[...], cols).astype(BF16)
    per = MXU_DIM // d2
    for c in range(p // MXU_DIM):
        picked = _dot_t(sel_ref[...], x[:, c * MXU_DIM:(c + 1) * MXU_DIM]).astype(BF16)
        for t in range(t_new):
            far_ref[t * BAND + c * per:t * BAND + (c + 1) * per, :] = picked[t * per:(t + 1) * per, :]


def _compaction_job(cache_kv, t_new, mlp_f32):
    n, p = cache_kv.shape[:2]
    d1, d2 = WINDOWS[1][1], WINDOWS[2][1]
    per = MXU_DIM // d2
    assert p == WINDOWS[2][0] and p // d2 == BAND and t_new <= d1 and p % MXU_DIM == 0
    feat_major = jnp.transpose(cache_kv, (0, 2, 3, 4, 1))
    eye = jnp.eye(MXU_DIM, dtype=BF16)
    row = jnp.arange(t_new * per)[:, None]
    col = jnp.arange(MXU_DIM)[None, :]
    sel = jnp.where(col == (row % per) * d2 + row // per, 1.0, 0.0).astype(BF16)
    return dict(inputs=(feat_major, eye, sel), t_new=t_new, n=n, next=0, chunks=[],
                mlp_f32=tuple(mlp_f32), mlp_w={})


def _fused_body(*refs, body, n_in, n_out, t_new, n_cast):
    c_ref, eye_ref, sel_ref = refs[n_in:n_in + 3]
    cast_in = refs[n_in + 3:n_in + 3 + n_cast]
    rest = refs[n_in + 3 + n_cast:]
    recent_ref, far_ref = rest[n_out:n_out + 2]
    cast_out = rest[n_out + 2:n_out + 2 + n_cast]
    body(*refs[:n_in], *rest[:n_out], *rest[n_out + 2 + n_cast:])
    _compact_cache_kernel(c_ref, eye_ref, sel_ref, recent_ref, far_ref, t_new=t_new)
    for src, dst in zip(cast_in, cast_out):
        dst[...] = src[...].astype(BF16)


def _call_with_compaction(body, job, steps, step_of, *, grid, in_specs, out_specs, out_shape, args,
                          semantics, vmem_mib, name, scratch_shapes=(), cast_layer=None):
    if job is None:
        return pl.pallas_call(body, out_shape=out_shape, grid=grid, in_specs=in_specs, out_specs=out_specs,
                              scratch_shapes=scratch_shapes, compiler_params=_params(semantics, vmem_mib),
                              name=name)(*args)
    base, t_new = job["next"], job["t_new"]
    assert base + steps <= job["n"]
    job["next"] = base + steps
    feat, eye, sel = job["inputs"]
    n_out = len(out_shape)
    rows = (WINDOWS[1][1] * BAND, t_new * BAND)
    stacks = job["mlp_f32"] if cast_layer is not None else ()
    cast_in, cast_sds, cast_out = [], [], []
    for w in stacks:
        blk = (None, w.shape[1] // steps, w.shape[2])
        cast_in.append(pl.BlockSpec(blk, lambda *idx: (cast_layer, step_of(*idx), 0)))
        cast_sds.append(jax.ShapeDtypeStruct((1,) + w.shape[1:], BF16))
        cast_out.append(pl.BlockSpec(blk, lambda *idx: (0, step_of(*idx), 0)))
    res = pl.pallas_call(
        functools.partial(_fused_body, body=body, n_in=len(args), n_out=n_out, t_new=t_new,
                          n_cast=len(stacks)),
        out_shape=list(out_shape) + [jax.ShapeDtypeStruct((steps, r, KV_W), BF16) for r in rows] + cast_sds,
        grid=grid,
        in_specs=list(in_specs)
        + [pl.BlockSpec((None,) + feat.shape[1:], lambda *idx: (base + step_of(*idx), 0, 0, 0, 0)),
           _const_spec(eye.shape), _const_spec(sel.shape)] + cast_in,
        out_specs=list(out_specs)
        + [pl.BlockSpec((None, r, KV_W), lambda *idx: (step_of(*idx), 0, 0)) for r in rows] + cast_out,
        scratch_shapes=scratch_shapes,
        compiler_params=_params(semantics, vmem_mib + COMPACT_VMEM_MIB),
        name=name,
    )(*args, feat, eye, sel, *stacks)
    job["chunks"].append((base, *res[n_out:n_out + 2]))
    if stacks:
        job["mlp_w"][cast_layer] = tuple(res[n_out + 2:])
    return res[:n_out]


def _attn_sample_kernel(q_ref, kvn_ref, recent_ref, far_ref, b0_ref, b1_ref, b2_ref, b2n_ref, o_ref):
    t_new = q_ref.shape[0]
    n_seq = SAMPLE_SEQ_STEP
    base = (pl.program_id(0) % (SAMPLE_ROWS // n_seq)) * n_seq
    n_rows = t_new * N_HEADS
    hsel = (lax.broadcasted_iota(jnp.int32, (N_HEADS, HEADS_W), 1) // HEAD_DIM
            == lax.broadcasted_iota(jnp.int32, (N_HEADS, HEADS_W), 0))
    row8 = lax.broadcasted_iota(jnp.int32, (SUBLANES, KV_W), 0)

    qs, kas, vas, kbs, vbs = [], [], [], [], []
    for j in range(n_seq):
        b = base + j
        qs.append([jnp.concatenate(
            [jnp.where(hsel, q_ref[t, pl.ds(b, 1), _lane_block(grp, HEADS_W)], 0.0) for t in range(t_new)],
            axis=0).astype(BF16) for grp in range(N_GROUPS)])
        new8 = jnp.zeros((SUBLANES, KV_W), F32)
        for t in range(t_new):
            new8 = jnp.where(row8 == t, kvn_ref[t, pl.ds(b, 1), :], new8)
        new = jnp.concatenate([new8, jnp.zeros((BAND - SUBLANES, KV_W), F32)], axis=0).astype(BF16)
        recent = recent_ref[j]
        kas.append(jnp.concatenate([recent[:, :HEADS_W], new[:, :HEADS_W]], axis=0))
        vas.append(jnp.concatenate([recent[:, HEADS_W:], new[:, HEADS_W:]], axis=0))
        kbs.append(far_ref[j, :, :HEADS_W])
        vbs.append(far_ref[j, :, HEADS_W:])
    n_a = kas[0].shape[0]

    def scores(grp, keys, lo, bias_ref):
        return jnp.concatenate([_dot_t(qs[j][grp], keys[j][lo:]) for j in range(n_seq)],
                               axis=0) + bias_ref[...]

    def softmax_pv(pieces):
        m = functools.reduce(jnp.maximum, [jnp.max(s, axis=-1, keepdims=True) for s, _, _ in pieces])
        l = jnp.zeros((n_seq * n_rows, 1), F32)
        acc = jnp.zeros((n_seq * n_rows, HEADS_W), F32)
        for s, vals, lo in pieces:
            p = jnp.exp2(s - m)
            l = l + jnp.sum(p, axis=-1, keepdims=True)
            pb = p.astype(BF16)
            acc = acc + jnp.concatenate(
                [jnp.dot(pb[j * n_rows:(j + 1) * n_rows], vals[j][lo:], preferred_element_type=F32)
                 for j in range(n_seq)], axis=0)
        return acc, m, l

    lo0, lo_new = n_a - 2 * BAND, n_a - BAND
    parts = [
        softmax_pv([(scores(0, kas, lo0, b0_ref), vas, lo0)]),
        softmax_pv([(scores(1, kas, 0, b1_ref), vas, 0)]),
        softmax_pv([(scores(2, kbs, 0, b2_ref), vbs, 0), (scores(2, kas, lo_new, b2n_ref), vas, lo_new)]),
    ]
    m_all = functools.reduce(jnp.maximum, [m for _, m, _ in parts])
    num = jnp.zeros((n_seq * n_rows, HEADS_W), F32)
    den = jnp.zeros((n_seq * n_rows, 1), F32)
    for acc, m, l in parts:
        e = jnp.exp2(m - m_all)
        num = num + e * acc
        den = den + e * l
    out = num * (1.0 / den)
    for j in range(n_seq):
        for t in range(t_new):
            r0 = j * n_rows + t * N_HEADS
            own = jnp.where(hsel, out[r0:r0 + N_HEADS, :], 0.0)
            o_ref[t, pl.ds(base + j, 1), :] = jnp.sum(own, axis=0, keepdims=True)


def _attn_sample(q, kvn, chunks, biases):
    t_new, n, qw = q.shape
    step = SAMPLE_SEQ_STEP
    per_blk = SAMPLE_ROWS // step
    biases = [jnp.tile(b, (step, 1)) for b in biases]
    outs = []
    for first, recent, far in chunks:
        count = recent.shape[0]
        assert first % SAMPLE_ROWS == 0 and count % SAMPLE_ROWS == 0
        blk0 = first // SAMPLE_ROWS
        blk = lambda width: pl.BlockSpec((t_new, SAMPLE_ROWS, width), lambda i: (0, blk0 + i // per_blk, 0))
        rows = lambda a: pl.BlockSpec((step,) + a.shape[1:], lambda i: (i, 0, 0))
        outs.append(pl.pallas_call(
            _attn_sample_kernel,
            out_shape=jax.ShapeDtypeStruct((t_new, count, HEADS_W), F32),
            grid=(count // step,),
            in_specs=[blk(qw), blk(KV_W), rows(recent), rows(far)] + [_const_spec(b.shape) for b in biases],
            out_specs=pl.BlockSpec((t_new, SAMPLE_ROWS, HEADS_W), lambda i: (0, i // per_blk, 0)),
            compiler_params=_params(("arbitrary",), 40),
            name="attn_sample",
        )(q, kvn, recent, far, *biases))
    return jnp.concatenate(outs, axis=1)


def _attn_out_sample_kernel(x_ref, o_ref, wo_ref, g_ref, wup_ref, wdn_ref, out_ref):
    x = x_ref[...] + jnp.dot(o_ref[...].astype(BF16), wo_ref[...], preferred_element_type=F32)
    out_ref[...] = _mlp_residual(x, g_ref, wup_ref, wdn_ref)


def _attn_out_prompt_kernel(x_ref, a0_ref, m0_ref, l0_ref, a1_ref, m1_ref, l1_ref, a2_ref, m2_ref, l2_ref,
                            ex_ref, wo_ref, g_ref, wup_ref, wdn_ref, out_ref, slab_ref):
    rows = x_ref.shape[0]
    n_o = HEADS_W // LANES
    for gi, (a_ref, m_ref, l_ref) in enumerate(((a1_ref, m1_ref, l1_ref), (a2_ref, m2_ref, l2_ref))):
        d = WINDOWS[gi + 1][1]
        for r in range(d):
            dst = pl.ds(r, rows // d, stride=d)
            for s in range(n_o):
                slab_ref[gi, s, dst, :] = a_ref[r, :, _lane_block(s)]
            slab_ref[gi, n_o, dst, :] = m_ref[r]
            slab_ref[gi, n_o + 1, dst, :] = l_ref[r]
    accs = [a0_ref[0]] + [jnp.concatenate([slab_ref[gi, s] for s in range(n_o)], axis=1) for gi in range(2)]
    ms = [m0_ref[0], slab_ref[0, n_o], slab_ref[1, n_o]]
    ls = [l0_ref[0], slab_ref[0, n_o + 1], slab_ref[1, n_o + 1]]
    mx = functools.reduce(jnp.maximum, ms)
    es = [jnp.exp2(m - mx) for m in ms]
    inv = 1.0 / functools.reduce(jnp.add, [e * l for e, l in zip(es, ls)])
    o = jnp.zeros((rows, HEADS_W), F32)
    for e, og in zip(es, accs):
        w = e * inv
        hi = w.astype(BF16)
        lo = (w - hi.astype(F32)).astype(BF16)
        wide = (jnp.dot(hi, ex_ref[...], preferred_element_type=F32)
                + jnp.dot(lo, ex_ref[...], preferred_element_type=F32))
        o = o + wide * og
    x = x_ref[...] + jnp.dot(o.astype(BF16), wo_ref[...], preferred_element_type=F32)
    out_ref[...] = _mlp_residual(x, g_ref, wup_ref, wdn_ref)


def _attn_out_q_prompt_kernel(*refs):
    n_in = 15
    gm_ref, wq_ref, e_ref, gq_ref = refs[n_in:n_in + 4]
    out_ref, q0_ref, q1_ref, q2_ref, slab_ref, qslab_ref = refs[n_in + 4:]
    _attn_out_prompt_kernel(*refs[:n_in], out_ref, slab_ref)
    h = _rms(out_ref[...], gm_ref[...]).astype(BF16)
    _q_prompt_store(h, wq_ref, e_ref, gq_ref, q0_ref, q1_ref, q2_ref, qslab_ref)


def _mlp_weight_specs(layer, j, wo, g, wup, wdn):
    return [_layer_spec(wo, j), _layer_spec(g, layer), _layer_spec(wup, 0), _layer_spec(wdn, 0)]


def _attn_out_sample(x, o, layer, j, wo, g, wup, wdn):
    m = x.shape[0]
    row = pl.BlockSpec((ROW_TILE, D_MODEL), lambda i: (i, 0))
    return pl.pallas_call(
        _attn_out_sample_kernel,
        out_shape=jax.ShapeDtypeStruct((m, D_MODEL), F32),
        grid=(m // ROW_TILE,),
        in_specs=[row, pl.BlockSpec((ROW_TILE, HEADS_W), lambda i: (i, 0))]
        + _mlp_weight_specs(layer, j, wo, g, wup, wdn),
        out_specs=row,
        compiler_params=_params(("arbitrary",), 48),
        name="attn_out_sample",
    )(x, o, wo, g, wup, wdn)


def _attn_out_prompt(x, parts, ex, layer, j, wo, g, wup, wdn, next_q=None):
    n, t, _ = x.shape
    tile = ATT_OUT_TILE
    assert tile == ROW_TILE
    row = pl.BlockSpec((None, tile, D_MODEL), lambda b, i: (b, i, 0))
    part_specs = []
    for _, d in WINDOWS:
        for width in (HEADS_W, LANES, LANES):
            part_specs.append(pl.BlockSpec((None, d, tile // d, width), lambda b, i: (b, 0, i, 0)))
    flat = [a for part in parts for a in part]
    in_specs = ([row] + part_specs + [_const_spec((LANES, HEADS_W))]
                + _mlp_weight_specs(layer, j, wo, g, wup, wdn))
    args = [x, *flat, ex, wo, g, wup, wdn]
    out_shape, out_specs = [jax.ShapeDtypeStruct((n, t, D_MODEL), F32)], [row]
    scratch = [pltpu.VMEM((2, HEADS_W // LANES + 2, tile, LANES), F32)]
    body = _attn_out_prompt_kernel
    if next_q is not None:
        g_mix, wq, e, gq = next_q
        in_specs += [_layer_spec(g_mix, layer + 1), _layer_spec(wq, j + 1), _const_spec(e.shape),
                     _layer_spec(gq, j + 1)]
        args += [g_mix, wq, e, gq]
        q_sds, q_specs = _dilated_shapes(n, t, HEADS_W, BF16)
        out_shape, out_specs = out_shape + q_sds, out_specs + q_specs
        scratch.append(pltpu.VMEM((HEADS_W // LANES, tile, LANES), F32))
        body = _attn_out_q_prompt_kernel
    res = pl.pallas_call(
        body,
        out_shape=out_shape,
        grid=(n, t // tile),
        in_specs=in_specs,
        out_specs=out_specs,
        scratch_shapes=scratch,
        compiler_params=_params(("arbitrary", "arbitrary"), 56),
        name="attn_out_prompt",
    )(*args)
    return res[0], res[1:]


def _alibi_slopes():
    n = N_GROUPS * N_HEADS
    s = 2.0 ** (-8.0 * jnp.arange(1, n + 1, dtype=F32) / n)
    return s.reshape(N_GROUPS, N_HEADS)


def _prompt_bias(slopes, grp):
    d = WINDOWS[grp][1]
    iq = jnp.arange(BAND, dtype=jnp.int32)[:, None]
    ik = jnp.arange(2 * BAND, dtype=jnp.int32)[None, :]
    steps = iq + BAND - ik
    valid = (steps >= 0) & (steps <= BAND)
    offs = (steps * d).astype(F32)
    per_head = jnp.where(valid[None], -slopes[grp][:, None, None] * offs[None] * LOG2_E, NEG_INF)
    return per_head.reshape(N_HEADS // 2, 2 * BAND, 2 * BAND)


def _sample_bias(slopes, t_new, past):
    d1, d2 = WINDOWS[1][1], WINDOWS[2][1]
    t = jnp.arange(t_new, dtype=jnp.int32)[:, None]

    def table(slope, dist, valid):
        b = jnp.where(valid[:, None, :], -slope[None, :, None] * dist.astype(F32)[:, None, :] * LOG2_E,
                      NEG_INF)
        return b.reshape(t_new * N_HEADS, -1)

    u = jnp.arange(BAND, dtype=jnp.int32)[None, :]
    dist_n = t - u
    ok_n = (u < t_new) & (dist_n >= 0)
    c = jnp.arange(BAND, dtype=jnp.int32)[None, :]
    dist_c = BAND + t - c
    b0 = table(slopes[0], jnp.concatenate([dist_c, dist_n], axis=1),
               jnp.concatenate([dist_c <= WINDOWS[0][0], ok_n], axis=1))
    c = jnp.arange(d1 * BAND, dtype=jnp.int32)[None, :]
    dist_c = d1 * BAND + t - c
    ok_c = (dist_c % d1 == 0) & (dist_c <= WINDOWS[1][0])
    b1 = table(slopes[1], jnp.concatenate([dist_c, dist_n], axis=1),
               jnp.concatenate([ok_c, ok_n & (dist_n % d1 == 0)], axis=1))
    per = past // d2
    col = jnp.arange(t_new * per, dtype=jnp.int32)[None, :]
    pos = (col % per) * d2 + col // per
    dist_c = past + t - pos
    ok_c = (dist_c % d2 == 0) & (dist_c <= WINDOWS[2][0])
    b2 = table(slopes[2], dist_c, ok_c)
    b2n = table(slopes[2], dist_n, ok_n & (dist_n % d2 == 0))
    return b0, b1, b2, b2n


def _head_mean_matrix():
    idx = jnp.arange(HEADS_W) // HEAD_DIM
    return jnp.where(idx[:, None] == idx[None, :], 1.0 / HEAD_DIM, 0.0).astype(BF16)


def _head_expand_matrix():
    lane = jnp.arange(LANES)[:, None]
    col = jnp.arange(HEADS_W)[None, :] // HEAD_DIM
    return jnp.where(lane == col, 1.0, 0.0).astype(BF16)


def kernel(x_prompt, x_sample, state_conv, cache_kv, norm_mix_g, norm_mlp_g, conv_w_pw1, conv_b_pw1,
           conv_w_dw, conv_b_dw, conv_ln_g, conv_ln_b, conv_w_pw2, conv_b_pw2, kv_norm_g, w_kv,
           k_norm_g, attn_w_q, q_norm_g, attn_w_o, mlp_w_up, mlp_w_down):
    n_p, t_p, _ = x_prompt.shape
    n_s, t_s, _ = x_sample.shape
    past = cache_kv.shape[1]
    scale = HEAD_DIM ** -0.5 * LOG2_E

    w1 = conv_w_pw1.astype(BF16)
    w2 = conv_w_pw2.astype(BF16)
    wkv = w_kv.astype(BF16)
    wq = attn_w_q.astype(BF16)
    wo = attn_w_o.astype(BF16)
    vec = lambda a: a.reshape(-1, 1, a.shape[-1])
    g_mix, g_mlp, g_kv = vec(norm_mix_g), vec(norm_mlp_g), vec(kv_norm_g)
    e_mat = _head_mean_matrix()
    ex_mat = _head_expand_matrix()
    gk = vec(jnp.tile(k_norm_g, N_HEADS))
    gq = vec(jnp.tile(q_norm_g, (1, N_HEADS)) * scale)
    wkv = wkv[None]
    slopes = _alibi_slopes()
    bias_p = [_prompt_bias(slopes, g) for g in range(N_GROUPS)]
    bias_s = _sample_bias(slopes, t_s, past)
    conv_weights = (g_mix, w1, vec(conv_b_pw1), conv_w_dw, vec(conv_b_dw), vec(conv_ln_g),
                    vec(conv_ln_b), w2, vec(conv_b_pw2))

    job = _compaction_job(cache_kv, t_s, (mlp_w_up, mlp_w_down))
    mlp_w = job["mlp_w"]
    xp = x_prompt
    conv_p = []
    zero_hist = jnp.zeros((n_p, HIST_PAD, CONV_CH), F32)
    for layer in range(N_A_LAYERS):
        xp, st = _conv_prompt(xp, zero_hist, layer, conv_weights, job, cast_layer=layer)
        conv_p.append(st[:, HIST_PAD - CONV_HIST:])
        xp = _mlp(xp.reshape(n_p * t_p, D_MODEL), layer, g_mlp, *mlp_w[layer], job,
                  cast_layer=N_A_LAYERS + layer).reshape(n_p, t_p, D_MODEL)
    assert job["next"] == job["n"] and len(mlp_w) == N_A_LAYERS + N_B_LAYERS
    win = min(WINDOWS[-1][0], t_p)
    kv_p, kvb_p, qs = _kvq_proj_prompt(xp, g_kv, wkv, e_mat, gk, win, N_A_LAYERS, g_mix, wq, gq)
    for j in range(N_B_LAYERS):
        layer = N_A_LAYERS + j
        parts = [_attn_group(qs[g], kvb_p[g], bias_p[g], g) for g in range(N_GROUPS)]
        next_q = (g_mix, wq, e_mat, gq) if j + 1 < N_B_LAYERS else None
        xp, qs = _attn_out_prompt(xp, parts, ex_mat, layer, j, wo, g_mlp, *mlp_w[layer], next_q)
    y_prompt = xp
    conv_prompt = jnp.stack(conv_p, axis=0)
    kv_prompt = kv_p.reshape(n_p, win, 2, N_HEADS, HEAD_DIM)

    xs = jnp.transpose(x_sample, (1, 0, 2))
    st_in = jnp.transpose(state_conv, (0, 2, 1, 3))
    new_states = []
    for layer in range(N_A_LAYERS):
        prev = new_states if layer == N_A_LAYERS - 1 else ()
        xs, st = _conv_sample(xs, st_in, layer, conv_weights, prev)
        new_states.append(st)
        xs = _mlp(xs.reshape(t_s * n_s, D_MODEL), layer, g_mlp, *mlp_w[layer]).reshape(t_s, n_s, D_MODEL)
    xs = xs.reshape(t_s * n_s, D_MODEL)
    kvn = _kv_proj_sample(xs, g_kv, wkv, e_mat, gk).reshape(t_s, n_s, KV_W)
    for j in range(N_B_LAYERS):
        layer = N_A_LAYERS + j
        q = _q_proj_sample(xs, layer, j, g_mix, wq, e_mat, gq)
        o = _attn_sample(q.reshape(t_s, n_s, N_GROUPS * HEADS_W), kvn, job["chunks"], bias_s)
        xs = _attn_out_sample(xs, o.reshape(t_s * n_s, HEADS_W), layer, j, wo, g_mlp, *mlp_w[layer])
    y_sample = jnp.transpose(xs.reshape(t_s, n_s, D_MODEL), (1, 0, 2))
    st_stack = new_states[-1] if N_A_LAYERS > 1 else new_states[-1][None]
    conv_sample = jnp.transpose(st_stack, (0, 2, 1, 3))
    kv_sample = jnp.transpose(kvn, (1, 0, 2)).reshape(n_s, t_s, 2, N_HEADS, HEAD_DIM)

    return (y_prompt, y_sample, conv_prompt, conv_sample, kv_prompt, kv_sample)
```

```python
import functools

import jax
import jax.numpy as jnp
from jax import lax
from jax.experimental import pallas as pl
from jax.experimental.pallas import tpu as pltpu

F32 = jnp.float32
BF16 = jnp.bfloat16

D_MODEL = 1024
CONV_CH = 1024
CONV_WIDTH = 31
CONV_HIST = CONV_WIDTH - 1
HIST_PAD = 32
D_FF = 4096
N_HEADS = 8
HEAD_DIM = 64
HEADS_W = N_HEADS * HEAD_DIM
KV_W = 2 * HEADS_W
WINDOWS = ((128, 1), (512, 4), (2048, 16))
N_GROUPS = len(WINDOWS)
BAND = 128
N_A_LAYERS = 2
N_B_LAYERS = 2
NORM_EPS = 1e-6
NEG_INF = float("-inf")
LOG2_E = 1.4426950408889634
LANES = 128
SUBLANES = 8

ROW_TILE = 512
ATT_OUT_TILE = 512
FF_CHUNK = 1024
ATT_QB = 2048
SAMPLE_CONV_NB = 32
SAMPLE_ROWS = 8
SAMPLE_SEQ_STEP = 4
MXU_DIM = 256
COMPACT_VMEM_MIB = 24
MIB = 1024 * 1024

assert all(w // d == BAND for w, d in WINDOWS)


def _const_spec(shape):
    nd = len(shape)
    return pl.BlockSpec(shape, lambda *_: (0,) * nd, pipeline_mode=pl.Buffered(1))


def _layer_spec(stacked, layer):
    return pl.BlockSpec((None,) + stacked.shape[1:], lambda *_: (layer, 0, 0),
                        pipeline_mode=pl.Buffered(1))


def _params(semantics, vmem_mib):
    return pltpu.CompilerParams(dimension_semantics=semantics, vmem_limit_bytes=vmem_mib * MIB)


def _lane_block(s, width=LANES):
    return slice(s * width, (s + 1) * width)


def _rms(x, g):
    return x * lax.rsqrt(jnp.mean(x * x, axis=-1, keepdims=True) + NORM_EPS) * g


def _mlp_residual(x, g_ref, wup_ref, wdn_ref):
    h = _rms(x, g_ref[...]).astype(BF16)
    acc = x
    for c in range(D_FF // FF_CHUNK):
        sl = _lane_block(c, FF_CHUNK)
        z = jnp.maximum(jnp.dot(h, wup_ref[:, sl], preferred_element_type=F32), 0.0)
        acc = acc + jnp.dot((z * z).astype(BF16), wdn_ref[sl, :], preferred_element_type=F32)
    return acc


def _head_mean_sq(v, e_ref):
    return jnp.dot((v * v).astype(BF16), e_ref[...], preferred_element_type=F32)


def _dot_t(a, b):
    return lax.dot_general(a, b, (((1,), (1,)), ((), ())), preferred_element_type=F32)


def _deinterleave_store(val, slab_ref, out_ref, d):
    rows, width = val.shape
    for s in range(width // LANES):
        slab_ref[s] = val[:, _lane_block(s)]
    for s in range(width // LANES):
        for r in range(d):
            out_ref[r, :, _lane_block(s)] = slab_ref[s, pl.ds(r, rows // d, stride=d), :].astype(BF16)


def _mlp_kernel(x_ref, g_ref, wup_ref, wdn_ref, o_ref):
    o_ref[...] = _mlp_residual(x_ref[...], g_ref, wup_ref, wdn_ref)


def _mlp(x, layer, g, wup, wdn, job=None, cast_layer=None):
    m = x.shape[0]
    row = pl.BlockSpec((ROW_TILE, D_MODEL), lambda i: (i, 0))
    return _call_with_compaction(
        _mlp_kernel, job, m // ROW_TILE, lambda i: i,
        out_shape=[jax.ShapeDtypeStruct((m, D_MODEL), F32)],
        grid=(m // ROW_TILE,),
        in_specs=[row, _layer_spec(g, layer), _layer_spec(wup, 0), _layer_spec(wdn, 0)],
        out_specs=[row],
        args=(x, g, wup, wdn),
        semantics=("arbitrary",), vmem_mib=34, name="mlp", cast_layer=cast_layer,
    )[0]


def _glu(h, w1_ref, b1_ref):
    u = jnp.dot(h, w1_ref[...], preferred_element_type=F32) + b1_ref[...]
    return u[:, :CONV_CH] * jax.nn.sigmoid(u[:, CONV_CH:])


def _conv_tail(c, x, bdw_ref, lng_ref, lnb_ref, w2_ref, b2_ref):
    c = c + bdw_ref[...]
    mu = jnp.mean(c, axis=-1, keepdims=True)
    xc = c - mu
    y = xc * lax.rsqrt(jnp.mean(xc * xc, axis=-1, keepdims=True) + NORM_EPS)
    y = y * lng_ref[...] + lnb_ref[...]
    y = y * jax.nn.sigmoid(y)
    out = jnp.dot(y.astype(BF16), w2_ref[...], preferred_element_type=F32) + b2_ref[...]
    return x + out


def _conv_prompt_kernel(x_ref, st_ref, g_ref, w1_ref, b1_ref, wdw_ref, bdw_ref, lng_ref, lnb_ref,
                        w2_ref, b2_ref, o_ref, sto_ref, buf_ref):
    n_slabs = CONV_CH // LANES

    @pl.when(pl.program_id(1) == 0)
    def _():
        for s in range(n_slabs):
            buf_ref[s, 0:HIST_PAD, :] = st_ref[:, _lane_block(s)]

    x = x_ref[...]
    h = _rms(x, g_ref[...]).astype(BF16)
    u = _glu(h, w1_ref, b1_ref)
    for s in range(n_slabs):
        buf_ref[s, HIST_PAD:HIST_PAD + ROW_TILE, :] = u[:, _lane_block(s)]
    off = HIST_PAD - CONV_HIST
    cs = []
    for s in range(n_slabs):
        ls = _lane_block(s)
        c = buf_ref[s, off:off + ROW_TILE, :] * wdw_ref[0:1, ls]
        for k in range(1, CONV_WIDTH):
            c = c + buf_ref[s, off + k:off + k + ROW_TILE, :] * wdw_ref[k:k + 1, ls]
        cs.append(c)
    c = jnp.concatenate(cs, axis=1)
    o_ref[...] = _conv_tail(c, x, bdw_ref, lng_ref, lnb_ref, w2_ref, b2_ref)
    for s in range(n_slabs):
        last = buf_ref[s, ROW_TILE:ROW_TILE + HIST_PAD, :]
        sto_ref[:, _lane_block(s)] = last
        buf_ref[s, 0:HIST_PAD, :] = last


def _conv_prompt(x, st, layer, weights, job=None, cast_layer=None):
    n, t, _ = x.shape
    tiles = t // ROW_TILE
    row = pl.BlockSpec((None, ROW_TILE, D_MODEL), lambda b, i: (b, i, 0))
    st_spec = pl.BlockSpec((None, HIST_PAD, CONV_CH), lambda b, i: (b, 0, 0))
    return _call_with_compaction(
        _conv_prompt_kernel, job, n * tiles, lambda b, i: b * tiles + i,
        out_shape=[jax.ShapeDtypeStruct((n, t, D_MODEL), F32),
                   jax.ShapeDtypeStruct((n, HIST_PAD, CONV_CH), F32)],
        grid=(n, tiles),
        in_specs=[row, st_spec] + [_layer_spec(a, layer) for a in weights],
        out_specs=[row, st_spec],
        scratch_shapes=[pltpu.VMEM((CONV_CH // LANES, HIST_PAD + ROW_TILE, LANES), F32)],
        args=(x, st, *weights),
        semantics=("arbitrary", "arbitrary"), vmem_mib=32, name="conv_prompt", cast_layer=cast_layer,
    )


def _conv_sample_kernel(x_ref, st_ref, g_ref, w1_ref, b1_ref, wdw_ref, bdw_ref, lng_ref, lnb_ref,
                        w2_ref, b2_ref, *rest):
    *prev_refs, o_ref, sto_ref = rest
    t_new, nb, _ = x_ref.shape
    x = x_ref[...].reshape(t_new * nb, D_MODEL)
    h = _rms(x, g_ref[...]).astype(BF16)
    glu = _glu(h, w1_ref, b1_ref)
    new = [glu[t * nb:(t + 1) * nb, :] for t in range(t_new)]

    def full(j):
        return st_ref[j] if j < CONV_HIST else new[j - CONV_HIST]

    outs = []
    for t in range(t_new):
        c = full(t) * wdw_ref[0:1, :]
        for k in range(1, CONV_WIDTH):
            c = c + full(t + k) * wdw_ref[k:k + 1, :]
        outs.append(c)
    c = jnp.concatenate(outs, axis=0)
    o_ref[...] = _conv_tail(c, x, bdw_ref, lng_ref, lnb_ref, w2_ref, b2_ref).reshape(t_new, nb, D_MODEL)
    if prev_refs:
        for l, prev_ref in enumerate(prev_refs):
            sto_ref[l] = prev_ref[...]
        for j in range(CONV_HIST):
            sto_ref[len(prev_refs), j] = full(j + t_new)
    else:
        for j in range(CONV_HIST):
            sto_ref[j] = full(j + t_new)


def _conv_sample(x, st_all, layer, weights, prev_states=()):
    t_new, n, _ = x.shape
    nb = SAMPLE_CONV_NB
    x_spec = pl.BlockSpec((t_new, nb, D_MODEL), lambda i: (0, i, 0))
    st_spec = pl.BlockSpec((None, CONV_HIST, nb, CONV_CH), lambda i: (layer, 0, i, 0))
    one_state = pl.BlockSpec((CONV_HIST, nb, CONV_CH), lambda i: (0, i, 0))
    if prev_states:
        n_stack = len(prev_states) + 1
        st_sds = jax.ShapeDtypeStruct((n_stack, CONV_HIST, n, CONV_CH), F32)
        sto_spec = pl.BlockSpec((n_stack, CONV_HIST, nb, CONV_CH), lambda i: (0, 0, i, 0))
    else:
        st_sds = jax.ShapeDtypeStruct((CONV_HIST, n, CONV_CH), F32)
        sto_spec = one_state
    return pl.pallas_call(
        _conv_sample_kernel,
        out_shape=(jax.ShapeDtypeStruct((t_new, n, D_MODEL), F32), st_sds),
        grid=(n // nb,),
        in_specs=[x_spec, st_spec] + [_layer_spec(a, layer) for a in weights] + [one_state] * len(prev_states),
        out_specs=(x_spec, sto_spec),
        compiler_params=_params(("arbitrary",), 48),
        name="conv_sample",
    )(x, st_all, *weights, *prev_states)


def _kv_rows(x_ref, g_ref, wkv_ref, e_ref, gk_ref):
    h = _rms(x_ref[...], g_ref[...]).astype(BF16)
    k = jnp.dot(h, wkv_ref[:, :HEADS_W], preferred_element_type=F32)
    v = jnp.dot(h, wkv_ref[:, HEADS_W:], preferred_element_type=F32)
    k = k * lax.rsqrt(_head_mean_sq(k, e_ref) + NORM_EPS) * gk_ref[...]
    return jnp.concatenate([k, v], axis=1)


def _kv_sample_kernel(x_ref, g_ref, wkv_ref, e_ref, gk_ref, kv_ref):
    kv_ref[...] = _kv_rows(x_ref, g_ref, wkv_ref, e_ref, gk_ref)


def _kvq_prompt_kernel(x_ref, g_ref, wkv_ref, e_ref, gk_ref, gm_ref, wq_ref, gq_ref,
                       kv_ref, kb0_ref, kb1_ref, kb2_ref, q0_ref, q1_ref, q2_ref, slab_ref):
    kv = _kv_rows(x_ref, g_ref, wkv_ref, e_ref, gk_ref)
    kv_ref[...] = kv
    kb0_ref[0] = kv.astype(BF16)
    _deinterleave_store(kv, slab_ref, kb1_ref, WINDOWS[1][1])
    _deinterleave_store(kv, slab_ref, kb2_ref, WINDOWS[2][1])
    h = _rms(x_ref[...], gm_ref[...]).astype(BF16)
    _q_prompt_store(h, wq_ref, e_ref, gq_ref, q0_ref, q1_ref, q2_ref, slab_ref)


def _proj_weight_specs(layer, g, w, e, gh):
    return [_layer_spec(g, layer), _layer_spec(w, layer), _const_spec(e.shape), _layer_spec(gh, layer)]


def _kv_proj_sample(x, g, wkv, e, gk):
    m = x.shape[0]
    return pl.pallas_call(
        _kv_sample_kernel,
        out_shape=jax.ShapeDtypeStruct((m, KV_W), F32),
        grid=(m // ROW_TILE,),
        in_specs=[pl.BlockSpec((ROW_TILE, D_MODEL), lambda i: (i, 0))] + _proj_weight_specs(0, g, wkv, e, gk),
        out_specs=pl.BlockSpec((ROW_TILE, KV_W), lambda i: (i, 0)),
        compiler_params=_params(("arbitrary",), 32),
        name="kv_proj_sample",
    )(x, g, wkv, e, gk)


def _dilated_shapes(n, t, width, dtype):
    sds, specs = [], []
    for _, d in WINDOWS:
        sds.append(jax.ShapeDtypeStruct((n, d, t // d, width), dtype))
        specs.append(pl.BlockSpec((None, d, ROW_TILE // d, width), lambda b, i: (b, 0, i, 0)))
    return sds, specs


def _kvq_proj_prompt(x, g, wkv, e, gk, win, layer, g_mix, wq, gq):
    n, t, _ = x.shape
    kv_sds, kv_specs = _dilated_shapes(n, t, KV_W, BF16)
    q_sds, q_specs = _dilated_shapes(n, t, HEADS_W, BF16)
    skip = (t - win) // ROW_TILE
    assert win % ROW_TILE == 0 and t % ROW_TILE == 0
    f32_spec = pl.BlockSpec((None, ROW_TILE, KV_W), lambda b, i: (b, jnp.maximum(i - skip, 0), 0))
    res = pl.pallas_call(
        _kvq_prompt_kernel,
        out_shape=[jax.ShapeDtypeStruct((n, win, KV_W), F32)] + kv_sds + q_sds,
        grid=(n, t // ROW_TILE),
        in_specs=[pl.BlockSpec((None, ROW_TILE, D_MODEL), lambda b, i: (b, i, 0))]
        + _proj_weight_specs(0, g, wkv, e, gk)
        + [_layer_spec(g_mix, layer), _layer_spec(wq, 0), _layer_spec(gq, 0)],
        out_specs=[f32_spec] + kv_specs + q_specs,
        scratch_shapes=[pltpu.VMEM((KV_W // LANES, ROW_TILE, LANES), F32)],
        compiler_params=_params(("arbitrary", "arbitrary"), 48),
        name="kvq_proj_prompt",
    )(x, g, wkv, e, gk, g_mix, wq, gq)
    return res[0], res[1:4], res[4:7]


def _q_group(h, wq_ref, e_ref, gq_ref, grp):
    q = jnp.dot(h, wq_ref[:, _lane_block(grp, HEADS_W)], preferred_element_type=F32)
    return q * lax.rsqrt(_head_mean_sq(q, e_ref) + NORM_EPS) * gq_ref[...]


def _q_sample_kernel(x_ref, g_ref, wq_ref, e_ref, gq_ref, q_ref):
    h = _rms(x_ref[...], g_ref[...]).astype(BF16)
    for grp in range(N_GROUPS):
        q = _q_group(h, wq_ref, e_ref, gq_ref, grp)
        q_ref[:, _lane_block(grp, HEADS_W)] = q.astype(BF16).astype(F32)


def _q_prompt_store(h, wq_ref, e_ref, gq_ref, q0_ref, q1_ref, q2_ref, slab_ref):
    q0_ref[0] = _q_group(h, wq_ref, e_ref, gq_ref, 0).astype(BF16)
    _deinterleave_store(_q_group(h, wq_ref, e_ref, gq_ref, 1), slab_ref, q1_ref, WINDOWS[1][1])
    _deinterleave_store(_q_group(h, wq_ref, e_ref, gq_ref, 2), slab_ref, q2_ref, WINDOWS[2][1])


def _q_proj_sample(x, layer, j, g, wq, e, gq):
    m = x.shape[0]
    qw = N_GROUPS * HEADS_W
    specs = [_layer_spec(g, layer), _layer_spec(wq, j), _const_spec(e.shape), _layer_spec(gq, j)]
    return pl.pallas_call(
        _q_sample_kernel,
        out_shape=jax.ShapeDtypeStruct((m, qw), F32),
        grid=(m // ROW_TILE,),
        in_specs=[pl.BlockSpec((ROW_TILE, D_MODEL), lambda i: (i, 0))] + specs,
        out_specs=pl.BlockSpec((ROW_TILE, qw), lambda i: (i, 0)),
        compiler_params=_params(("arbitrary",), 32),
        name="q_proj_sample",
    )(x, g, wq, e, gq)


def _attn_group_kernel(q_ref, kc_ref, kp_ref, vc_ref, vp_ref, bias_ref, acc_ref, m_ref, l_ref):
    for rr in range(q_ref.shape[0]):
        _attn_residue(q_ref.at[rr], kc_ref.at[rr], kp_ref.at[rr], vc_ref.at[rr], vp_ref.at[rr], bias_ref,
                      acc_ref.at[rr], m_ref.at[rr], l_ref.at[rr])


def _attn_residue(q_ref, kc_ref, kp_ref, vc_ref, vp_ref, bias_ref, acc_ref, m_ref, l_ref):
    first_block = pl.program_id(2) == 0
    key_col = lax.broadcasted_iota(jnp.int32, (2 * BAND, 2 * BAND), 1)
    lane = lax.broadcasted_iota(jnp.int32, (BAND, LANES), 1)
    left = lane < HEAD_DIM
    lane_row = lax.broadcasted_iota(jnp.int32, (1, LANES), 1)
    head_keep = [jnp.where(lane_row < HEAD_DIM, 1.0, 0.0).astype(BF16),
                 jnp.where(lane_row < HEAD_DIM, 0.0, 1.0).astype(BF16)]
    no_history = jnp.logical_and(first_block, key_col < BAND)

    for sb in range(q_ref.shape[0] // BAND):
        r0 = sb * BAND
        m_tile = jnp.zeros((BAND, LANES), F32)
        l_tile = jnp.ones((BAND, LANES), F32)
        for pair in range(N_HEADS // 2):
            ls = _lane_block(pair)
            q2 = q_ref[r0:r0 + BAND, ls]
            if sb == 0:
                k2 = jnp.concatenate([kp_ref[:, ls], kc_ref[0:BAND, ls]], axis=0)
                v2 = jnp.concatenate([vp_ref[:, ls], vc_ref[0:BAND, ls]], axis=0)
            else:
                k2 = kc_ref[r0 - BAND:r0 + BAND, ls]
                v2 = vc_ref[r0 - BAND:r0 + BAND, ls]
            qq = jnp.concatenate([q2 * head_keep[0], q2 * head_keep[1]], axis=0)
            s = _dot_t(qq, k2) + bias_ref[pair]
            if sb == 0:
                s = jnp.where(no_history, NEG_INF, s)
            m = jnp.max(s, axis=-1, keepdims=True)
            p = jnp.exp2(s - m)
            l = jnp.sum(p, axis=-1, keepdims=True)
            a = jnp.dot(p.astype(BF16), v2, preferred_element_type=F32)
            acc_ref[r0:r0 + BAND, ls] = jnp.where(left, a[:BAND], a[BAND:])
            for hh in range(2):
                rows = slice(hh * BAND, (hh + 1) * BAND)
                m_tile = jnp.where(lane == 2 * pair + hh, m[rows], m_tile)
                l_tile = jnp.where(lane == 2 * pair + hh, l[rows], l_tile)
        m_ref[r0:r0 + BAND, :] = m_tile
        l_ref[r0:r0 + BAND, :] = l_tile


def _attn_group(q, kvb, bias, grp):
    n, d, rows, _ = q.shape
    qb = min(ATT_QB, rows)
    per_blk = qb // BAND
    res = min(d, ATT_QB // qb)
    stat_sds = jax.ShapeDtypeStruct((n, d, rows, LANES), F32)
    stat_spec = pl.BlockSpec((None, res, qb, LANES), lambda b, r, i: (b, r, i, 0))

    def prev_idx(i):
        return jnp.maximum(i * per_blk - 1, 0)

    def cur(col):
        return pl.BlockSpec((None, res, qb, HEADS_W), lambda b, r, i: (b, r, i, col))

    def prev(col):
        return pl.BlockSpec((None, res, BAND, HEADS_W), lambda b, r, i: (b, r, prev_idx(i), col))

    return pl.pallas_call(
        _attn_group_kernel,
        out_shape=(jax.ShapeDtypeStruct((n, d, rows, HEADS_W), F32), stat_sds, stat_sds),
        grid=(n, d // res, rows // qb),
        in_specs=[cur(0), cur(0), prev(0), cur(1), prev(1),
                  _const_spec((N_HEADS // 2, 2 * BAND, 2 * BAND))],
        out_specs=(cur(0), stat_spec, stat_spec),
        compiler_params=_params(("arbitrary", "arbitrary", "arbitrary"), 40),
        name=f"attn_group{grp}",
    )(q, kvb, kvb, kvb, kvb, bias)


def _compact_cache_kernel(c_ref, eye_ref, sel_ref, recent_ref, far_ref, *, t_new):
    p = c_ref.shape[-1]
    d1, d2 = WINDOWS[1][1], WINDOWS[2][1]
    x = c_ref[...].reshape(KV_W, p).astype(BF16)
    lo = p - d1 * BAND
    for j in range(d1 * BAND // MXU_DIM):
        cols = x[:, lo + j * MXU_DIM:lo + (j + 1) * MXU_DIM]
        recent_ref[j * MXU_DIM:(j + 1) * MXU_DIM, :] = _dot_t(eye_ref[...], cols).astype(BF16)
    per = MXU_DIM // d2
    for c in range(p // MXU_DIM):
        picked = _dot_t(sel_ref[...], x[:, c * MXU_DIM:(c + 1) * MXU_DIM]).astype(BF16)
        for t in range(t_new):
            far_ref[t * BAND + c * per:t * BAND + (c + 1) * per, :] = picked[t * per:(t + 1) * per, :]


def _compaction_job(cache_kv, t_new, mlp_f32):
    n, p = cache_kv.shape[:2]
    d1, d2 = WINDOWS[1][1], WINDOWS[2][1]
    per = MXU_DIM // d2
    assert p == WINDOWS[2][0] and p // d2 == BAND and t_new <= d1 and p % MXU_DIM == 0
    feat_major = jnp.transpose(cache_kv, (0, 2, 3, 4, 1))
    eye = jnp.eye(MXU_DIM, dtype=BF16)
    row = jnp.arange(t_new * per)[:, None]
    col = jnp.arange(MXU_DIM)[None, :]
    sel = jnp.where(col == (row % per) * d2 + row // per, 1.0, 0.0).astype(BF16)
    return dict(inputs=(feat_major, eye, sel), t_new=t_new, n=n, next=0, chunks=[],
                mlp_f32=tuple(mlp_f32), mlp_w={})


def _fused_body(*refs, body, n_in, n_out, t_new, n_cast):
    c_ref, eye_ref, sel_ref = refs[n_in:n_in + 3]
    cast_in = refs[n_in + 3:n_in + 3 + n_cast]
    rest = refs[n_in + 3 + n_cast:]
    recent_ref, far_ref = rest[n_out:n_out + 2]
    cast_out = rest[n_out + 2:n_out + 2 + n_cast]
    body(*refs[:n_in], *rest[:n_out], *rest[n_out + 2 + n_cast:])
    _compact_cache_kernel(c_ref, eye_ref, sel_ref, recent_ref, far_ref, t_new=t_new)
    for src, dst in zip(cast_in, cast_out):
        dst[...] = src[...].astype(BF16)


def _call_with_compaction(body, job, steps, step_of, *, grid, in_specs, out_specs, out_shape, args,
                          semantics, vmem_mib, name, scratch_shapes=(), cast_layer=None):
    if job is None:
        return pl.pallas_call(body, out_shape=out_shape, grid=grid, in_specs=in_specs, out_specs=out_specs,
                              scratch_shapes=scratch_shapes, compiler_params=_params(semantics, vmem_mib),
                              name=name)(*args)
    base, t_new = job["next"], job["t_new"]
    assert base + steps <= job["n"]
    job["next"] = base + steps
    feat, eye, sel = job["inputs"]
    n_out = len(out_shape)
    rows = (WINDOWS[1][1] * BAND, t_new * BAND)
    stacks = job["mlp_f32"] if cast_layer is not None else ()
    cast_in, cast_sds, cast_out = [], [], []
    for w in stacks:
        blk = (None, w.shape[1] // steps, w.shape[2])
        cast_in.append(pl.BlockSpec(blk, lambda *idx: (cast_layer, step_of(*idx), 0)))
        cast_sds.append(jax.ShapeDtypeStruct((1,) + w.shape[1:], BF16))
        cast_out.append(pl.BlockSpec(blk, lambda *idx: (0, step_of(*idx), 0)))
    res = pl.pallas_call(
        functools.partial(_fused_body, body=body, n_in=len(args), n_out=n_out, t_new=t_new,
                          n_cast=len(stacks)),
        out_shape=list(out_shape) + [jax.ShapeDtypeStruct((steps, r, KV_W), BF16) for r in rows] + cast_sds,
        grid=grid,
        in_specs=list(in_specs)
        + [pl.BlockSpec((None,) + feat.shape[1:], lambda *idx: (base + step_of(*idx), 0, 0, 0, 0)),
           _const_spec(eye.shape), _const_spec(sel.shape)] + cast_in,
        out_specs=list(out_specs)
        + [pl.BlockSpec((None, r, KV_W), lambda *idx: (step_of(*idx), 0, 0)) for r in rows] + cast_out,
        scratch_shapes=scratch_shapes,
        compiler_params=_params(semantics, vmem_mib + COMPACT_VMEM_MIB),
        name=name,
    )(*args, feat, eye, sel, *stacks)
    job["chunks"].append((base, *res[n_out:n_out + 2]))
    if stacks:
        job["mlp_w"][cast_layer] = tuple(res[n_out + 2:])
    return res[:n_out]


def _attn_sample_kernel(q_ref, kvn_ref, recent_ref, far_ref, b0_ref, b1_ref, b2_ref, b2n_ref, o_ref):
    t_new = q_ref.shape[0]
    n_seq = SAMPLE_SEQ_STEP
    base = (pl.program_id(0) % (SAMPLE_ROWS // n_seq)) * n_seq
    n_rows = t_new * N_HEADS
    hsel = (lax.broadcasted_iota(jnp.int32, (N_HEADS, HEADS_W), 1) // HEAD_DIM
            == lax.broadcasted_iota(jnp.int32, (N_HEADS, HEADS_W), 0))
    row8 = lax.broadcasted_iota(jnp.int32, (SUBLANES, KV_W), 0)

    qs, kas, vas, kbs, vbs = [], [], [], [], []
    for j in range(n_seq):
        b = base + j
        qs.append([jnp.concatenate(
            [jnp.where(hsel, q_ref[t, pl.ds(b, 1), _lane_block(grp, HEADS_W)], 0.0) for t in range(t_new)],
            axis=0).astype(BF16) for grp in range(N_GROUPS)])
        new8 = jnp.zeros((SUBLANES, KV_W), F32)
        for t in range(t_new):
            new8 = jnp.where(row8 == t, kvn_ref[t, pl.ds(b, 1), :], new8)
        new = jnp.concatenate([new8, jnp.zeros((BAND - SUBLANES, KV_W), F32)], axis=0).astype(BF16)
        recent = recent_ref[j]
        kas.append(jnp.concatenate([recent[:, :HEADS_W], new[:, :HEADS_W]], axis=0))
        vas.append(jnp.concatenate([recent[:, HEADS_W:], new[:, HEADS_W:]], axis=0))
        kbs.append(far_ref[j, :, :HEADS_W])
        vbs.append(far_ref[j, :, HEADS_W:])
    n_a = kas[0].shape[0]

    def scores(grp, keys, lo, bias_ref):
        return jnp.concatenate([_dot_t(qs[j][grp], keys[j][lo:]) for j in range(n_seq)],
                               axis=0) + bias_ref[...]

    def softmax_pv(pieces):
        m = functools.reduce(jnp.maximum, [jnp.max(s, axis=-1, keepdims=True) for s, _, _ in pieces])
        l = jnp.zeros((n_seq * n_rows, 1), F32)
        acc = jnp.zeros((n_seq * n_rows, HEADS_W), F32)
        for s, vals, lo in pieces:
            p = jnp.exp2(s - m)
            l = l + jnp.sum(p, axis=-1, keepdims=True)
            pb = p.astype(BF16)
            acc = acc + jnp.concatenate(
                [jnp.dot(pb[j * n_rows:(j + 1) * n_rows], vals[j][lo:], preferred_element_type=F32)
                 for j in range(n_seq)], axis=0)
        return acc, m, l

    lo0, lo_new = n_a - 2 * BAND, n_a - BAND
    parts = [
        softmax_pv([(scores(0, kas, lo0, b0_ref), vas, lo0)]),
        softmax_pv([(scores(1, kas, 0, b1_ref), vas, 0)]),
        softmax_pv([(scores(2, kbs, 0, b2_ref), vbs, 0), (scores(2, kas, lo_new, b2n_ref), vas, lo_new)]),
    ]
    m_all = functools.reduce(jnp.maximum, [m for _, m, _ in parts])
    num = jnp.zeros((n_seq * n_rows, HEADS_W), F32)
    den = jnp.zeros((n_seq * n_rows, 1), F32)
    for acc, m, l in parts:
        e = jnp.exp2(m - m_all)
        num = num + e * acc
        den = den + e * l
    out = num * (1.0 / den)
    for j in range(n_seq):
        for t in range(t_new):
            r0 = j * n_rows + t * N_HEADS
            own = jnp.where(hsel, out[r0:r0 + N_HEADS, :], 0.0)
            o_ref[t, pl.ds(base + j, 1), :] = jnp.sum(own, axis=0, keepdims=True)


def _attn_sample(q, kvn, chunks, biases):
    t_new, n, qw = q.shape
    step = SAMPLE_SEQ_STEP
    per_blk = SAMPLE_ROWS // step
    biases = [jnp.tile(b, (step, 1)) for b in biases]
    outs = []
    for first, recent, far in chunks:
        count = recent.shape[0]
        assert first % SAMPLE_ROWS == 0 and count % SAMPLE_ROWS == 0
        blk0 = first // SAMPLE_ROWS
        blk = lambda width: pl.BlockSpec((t_new, SAMPLE_ROWS, width), lambda i: (0, blk0 + i // per_blk, 0))
        rows = lambda a: pl.BlockSpec((step,) + a.shape[1:], lambda i: (i, 0, 0))
        outs.append(pl.pallas_call(
            _attn_sample_kernel,
            out_shape=jax.ShapeDtypeStruct((t_new, count, HEADS_W), F32),
            grid=(count // step,),
            in_specs=[blk(qw), blk(KV_W), rows(recent), rows(far)] + [_const_spec(b.shape) for b in biases],
            out_specs=pl.BlockSpec((t_new, SAMPLE_ROWS, HEADS_W), lambda i: (0, i // per_blk, 0)),
            compiler_params=_params(("arbitrary",), 40),
            name="attn_sample",
        )(q, kvn, recent, far, *biases))
    return jnp.concatenate(outs, axis=1)


def _attn_out_sample_kernel(x_ref, o_ref, wo_ref, g_ref, wup_ref, wdn_ref, out_ref):
    x = x_ref[...] + jnp.dot(o_ref[...].astype(BF16), wo_ref[...], preferred_element_type=F32)
    out_ref[...] = _mlp_residual(x, g_ref, wup_ref, wdn_ref)


def _attn_out_prompt_kernel(x_ref, a0_ref, m0_ref, l0_ref, a1_ref, m1_ref, l1_ref, a2_ref, m2_ref, l2_ref,
                            ex_ref, wo_ref, g_ref, wup_ref, wdn_ref, out_ref, slab_ref):
    rows = x_ref.shape[0]
    n_o = HEADS_W // LANES
    for gi, (a_ref, m_ref, l_ref) in enumerate(((a1_ref, m1_ref, l1_ref), (a2_ref, m2_ref, l2_ref))):
        d = WINDOWS[gi + 1][1]
        for r in range(d):
            dst = pl.ds(r, rows // d, stride=d)
            for s in range(n_o):
                slab_ref[gi, s, dst, :] = a_ref[r, :, _lane_block(s)]
            slab_ref[gi, n_o, dst, :] = m_ref[r]
            slab_ref[gi, n_o + 1, dst, :] = l_ref[r]
    accs = [a0_ref[0]] + [jnp.concatenate([slab_ref[gi, s] for s in range(n_o)], axis=1) for gi in range(2)]
    ms = [m0_ref[0], slab_ref[0, n_o], slab_ref[1, n_o]]
    ls = [l0_ref[0], slab_ref[0, n_o + 1], slab_ref[1, n_o + 1]]
    mx = functools.reduce(jnp.maximum, ms)
    es = [jnp.exp2(m - mx) for m in ms]
    inv = 1.0 / functools.reduce(jnp.add, [e * l for e, l in zip(es, ls)])
    o = jnp.zeros((rows, HEADS_W), F32)
    for e, og in zip(es, accs):
        w = e * inv
        hi = w.astype(BF16)
        lo = (w - hi.astype(F32)).astype(BF16)
        wide = jnp.dot(jnp.concatenate([hi, lo], axis=1), ex_ref[...], preferred_element_type=F32)
        o = o + wide * og
    x = x_ref[...] + jnp.dot(o.astype(BF16), wo_ref[...], preferred_element_type=F32)
    out_ref[...] = _mlp_residual(x, g_ref, wup_ref, wdn_ref)


def _attn_out_q_prompt_kernel(*refs):
    n_in = 15
    gm_ref, wq_ref, e_ref, gq_ref = refs[n_in:n_in + 4]
    out_ref, q0_ref, q1_ref, q2_ref, slab_ref, qslab_ref = refs[n_in + 4:]
    _attn_out_prompt_kernel(*refs[:n_in], out_ref, slab_ref)
    h = _rms(out_ref[...], gm_ref[...]).astype(BF16)
    _q_prompt_store(h, wq_ref, e_ref, gq_ref, q0_ref, q1_ref, q2_ref, qslab_ref)


def _mlp_weight_specs(layer, j, wo, g, wup, wdn):
    return [_layer_spec(wo, j), _layer_spec(g, layer), _layer_spec(wup, 0), _layer_spec(wdn, 0)]


def _attn_out_sample(x, o, layer, j, wo, g, wup, wdn):
    m = x.shape[0]
    row = pl.BlockSpec((ROW_TILE, D_MODEL), lambda i: (i, 0))
    return pl.pallas_call(
        _attn_out_sample_kernel,
        out_shape=jax.ShapeDtypeStruct((m, D_MODEL), F32),
        grid=(m // ROW_TILE,),
        in_specs=[row, pl.BlockSpec((ROW_TILE, HEADS_W), lambda i: (i, 0))]
        + _mlp_weight_specs(layer, j, wo, g, wup, wdn),
        out_specs=row,
        compiler_params=_params(("arbitrary",), 48),
        name="attn_out_sample",
    )(x, o, wo, g, wup, wdn)


def _attn_out_prompt(x, parts, ex, layer, j, wo, g, wup, wdn, next_q=None):
    n, t, _ = x.shape
    tile = ATT_OUT_TILE
    assert tile == ROW_TILE
    row = pl.BlockSpec((None, tile, D_MODEL), lambda b, i: (b, i, 0))
    part_specs = []
    for _, d in WINDOWS:
        for width in (HEADS_W, LANES, LANES):
            part_specs.append(pl.BlockSpec((None, d, tile // d, width), lambda b, i: (b, 0, i, 0)))
    flat = [a for part in parts for a in part]
    in_specs = ([row] + part_specs + [_const_spec((2 * LANES, HEADS_W))]
                + _mlp_weight_specs(layer, j, wo, g, wup, wdn))
    args = [x, *flat, ex, wo, g, wup, wdn]
    out_shape, out_specs = [jax.ShapeDtypeStruct((n, t, D_MODEL), F32)], [row]
    scratch = [pltpu.VMEM((2, HEADS_W // LANES + 2, tile, LANES), F32)]
    body = _attn_out_prompt_kernel
    if next_q is not None:
        g_mix, wq, e, gq = next_q
        in_specs += [_layer_spec(g_mix, layer + 1), _layer_spec(wq, j + 1), _const_spec(e.shape),
                     _layer_spec(gq, j + 1)]
        args += [g_mix, wq, e, gq]
        q_sds, q_specs = _dilated_shapes(n, t, HEADS_W, BF16)
        out_shape, out_specs = out_shape + q_sds, out_specs + q_specs
        scratch.append(pltpu.VMEM((HEADS_W // LANES, tile, LANES), F32))
        body = _attn_out_q_prompt_kernel
    res = pl.pallas_call(
        body,
        out_shape=out_shape,
        grid=(n, t // tile),
        in_specs=in_specs,
        out_specs=out_specs,
        scratch_shapes=scratch,
        compiler_params=_params(("arbitrary", "arbitrary"), 56),
        name="attn_out_prompt",
    )(*args)
    return res[0], res[1:]


def _alibi_slopes():
    n = N_GROUPS * N_HEADS
    s = 2.0 ** (-8.0 * jnp.arange(1, n + 1, dtype=F32) / n)
    return s.reshape(N_GROUPS, N_HEADS)


def _prompt_bias(slopes, grp):
    d = WINDOWS[grp][1]
    iq = jnp.arange(BAND, dtype=jnp.int32)[:, None]
    ik = jnp.arange(2 * BAND, dtype=jnp.int32)[None, :]
    steps = iq + BAND - ik
    valid = (steps >= 0) & (steps <= BAND)
    offs = (steps * d).astype(F32)
    per_head = jnp.where(valid[None], -slopes[grp][:, None, None] * offs[None] * LOG2_E, NEG_INF)
    return per_head.reshape(N_HEADS // 2, 2 * BAND, 2 * BAND)


def _sample_bias(slopes, t_new, past):
    d1, d2 = WINDOWS[1][1], WINDOWS[2][1]
    t = jnp.arange(t_new, dtype=jnp.int32)[:, None]

    def table(slope, dist, valid):
        b = jnp.where(valid[:, None, :], -slope[None, :, None] * dist.astype(F32)[:, None, :] * LOG2_E,
                      NEG_INF)
        return b.reshape(t_new * N_HEADS, -1)

    u = jnp.arange(BAND, dtype=jnp.int32)[None, :]
    dist_n = t - u
    ok_n = (u < t_new) & (dist_n >= 0)
    c = jnp.arange(BAND, dtype=jnp.int32)[None, :]
    dist_c = BAND + t - c
    b0 = table(slopes[0], jnp.concatenate([dist_c, dist_n], axis=1),
               jnp.concatenate([dist_c <= WINDOWS[0][0], ok_n], axis=1))
    c = jnp.arange(d1 * BAND, dtype=jnp.int32)[None, :]
    dist_c = d1 * BAND + t - c
    ok_c = (dist_c % d1 == 0) & (dist_c <= WINDOWS[1][0])
    b1 = table(slopes[1], jnp.concatenate([dist_c, dist_n], axis=1),
               jnp.concatenate([ok_c, ok_n & (dist_n % d1 == 0)], axis=1))
    per = past // d2
    col = jnp.arange(t_new * per, dtype=jnp.int32)[None, :]
    pos = (col % per) * d2 + col // per
    dist_c = past + t - pos
    ok_c = (dist_c % d2 == 0) & (dist_c <= WINDOWS[2][0])
    b2 = table(slopes[2], dist_c, ok_c)
    b2n = table(slopes[2], dist_n, ok_n & (dist_n % d2 == 0))
    return b0, b1, b2, b2n


def _head_mean_matrix():
    idx = jnp.arange(HEADS_W) // HEAD_DIM
    return jnp.where(idx[:, None] == idx[None, :], 1.0 / HEAD_DIM, 0.0).astype(BF16)


def _head_expand_matrix():
    lane = jnp.arange(2 * LANES)[:, None] % LANES
    col = jnp.arange(HEADS_W)[None, :] // HEAD_DIM
    return jnp.where(lane == col, 1.0, 0.0).astype(BF16)


def kernel(x_prompt, x_sample, state_conv, cache_kv, norm_mix_g, norm_mlp_g, conv_w_pw1, conv_b_pw1,
           conv_w_dw, conv_b_dw, conv_ln_g, conv_ln_b, conv_w_pw2, conv_b_pw2, kv_norm_g, w_kv,
           k_norm_g, attn_w_q, q_norm_g, attn_w_o, mlp_w_up, mlp_w_down):
    n_p, t_p, _ = x_prompt.shape
    n_s, t_s, _ = x_sample.shape
    past = cache_kv.shape[1]
    scale = HEAD_DIM ** -0.5 * LOG2_E

    w1 = conv_w_pw1.astype(BF16)
    w2 = conv_w_pw2.astype(BF16)
    wkv = w_kv.astype(BF16)
    wq = attn_w_q.astype(BF16)
    wo = attn_w_o.astype(BF16)
    vec = lambda a: a.reshape(-1, 1, a.shape[-1])
    g_mix, g_mlp, g_kv = vec(norm_mix_g), vec(norm_mlp_g), vec(kv_norm_g)
    e_mat = _head_mean_matrix()
    ex_mat = _head_expand_matrix()
    gk = vec(jnp.tile(k_norm_g, N_HEADS))
    gq = vec(jnp.tile(q_norm_g, (1, N_HEADS)) * scale)
    wkv = wkv[None]
    slopes = _alibi_slopes()
    bias_p = [_prompt_bias(slopes, g) for g in range(N_GROUPS)]
    bias_s = _sample_bias(slopes, t_s, past)
    conv_weights = (g_mix, w1, vec(conv_b_pw1), conv_w_dw, vec(conv_b_dw), vec(conv_ln_g),
                    vec(conv_ln_b), w2, vec(conv_b_pw2))

    job = _compaction_job(cache_kv, t_s, (mlp_w_up, mlp_w_down))
    mlp_w = job["mlp_w"]
    xp = x_prompt
    conv_p = []
    zero_hist = jnp.zeros((n_p, HIST_PAD, CONV_CH), F32)
    for layer in range(N_A_LAYERS):
        xp, st = _conv_prompt(xp, zero_hist, layer, conv_weights, job, cast_layer=layer)
        conv_p.append(st[:, HIST_PAD - CONV_HIST:])
        xp = _mlp(xp.reshape(n_p * t_p, D_MODEL), layer, g_mlp, *mlp_w[layer], job,
                  cast_layer=N_A_LAYERS + layer).reshape(n_p, t_p, D_MODEL)
    assert job["next"] == job["n"] and len(mlp_w) == N_A_LAYERS + N_B_LAYERS
    win = min(WINDOWS[-1][0], t_p)
    kv_p, kvb_p, qs = _kvq_proj_prompt(xp, g_kv, wkv, e_mat, gk, win, N_A_LAYERS, g_mix, wq, gq)
    for j in range(N_B_LAYERS):
        layer = N_A_LAYERS + j
        parts = [_attn_group(qs[g], kvb_p[g], bias_p[g], g) for g in range(N_GROUPS)]
        next_q = (g_mix, wq, e_mat, gq) if j + 1 < N_B_LAYERS else None
        xp, qs = _attn_out_prompt(xp, parts, ex_mat, layer, j, wo, g_mlp, *mlp_w[layer], next_q)
    y_prompt = xp
    conv_prompt = jnp.stack(conv_p, axis=0)
    kv_prompt = kv_p.reshape(n_p, win, 2, N_HEADS, HEAD_DIM)

    xs = jnp.transpose(x_sample, (1, 0, 2))
    st_in = jnp.transpose(state_conv, (0, 2, 1, 3))
    new_states = []
    for layer in range(N_A_LAYERS):
        prev = new_states if layer == N_A_LAYERS - 1 else ()
        xs, st = _conv_sample(xs, st_in, layer, conv_weights, prev)
        new_states.append(st)
        xs = _mlp(xs.reshape(t_s * n_s, D_MODEL), layer, g_mlp, *mlp_w[layer]).reshape(t_s, n_s, D_MODEL)
    xs = xs.reshape(t_s * n_s, D_MODEL)
    kvn = _kv_proj_sample(xs, g_kv, wkv, e_mat, gk).reshape(t_s, n_s, KV_W)
    for j in range(N_B_LAYERS):
        layer = N_A_LAYERS + j
        q = _q_proj_sample(xs, layer, j, g_mix, wq, e_mat, gq)
        o = _attn_sample(q.reshape(t_s, n_s, N_GROUPS * HEADS_W), kvn, job["chunks"], bias_s)
        xs = _attn_out_sample(xs, o.reshape(t_s * n_s, HEADS_W), layer, j, wo, g_mlp, *mlp_w[layer])
    y_sample = jnp.transpose(xs.reshape(t_s, n_s, D_MODEL), (1, 0, 2))
    st_stack = new_states[-1] if N_A_LAYERS > 1 else new_states[-1][None]
    conv_sample = jnp.transpose(st_stack, (0, 2, 1, 3))
    kv_sample = jnp.transpose(kvn, (1, 0, 2)).reshape(n_s, t_s, 2, N_HEADS, HEAD_DIM)

    return (y_prompt, y_sample, conv_prompt, conv_sample, kv_prompt, kv_sample)
```
